```python
import math
import jax, jax.numpy as jnp
from jax import lax
import numpy as np

D_MODEL = 1024
BATCH = 2
SEQ = 8192
DEPTH = 2
DEC_BATCH = 32
DEC_SEQ = 1
PAST_LEN = 16384
PAGE_SIZE = 128

MIX_WIDTH = D_MODEL
S5_WIDTH = MIX_WIDTH // 2
S5_GROUP = 16
S5_GROUPS = S5_WIDTH // S5_GROUP
S5_STATE = 64
DIFF_HEAD_DIM = 64
DIFF_HEADS = (MIX_WIDTH - S5_WIDTH) // (2 * DIFF_HEAD_DIM)
DIFF_QK = 2 * DIFF_HEADS * DIFF_HEAD_DIM
DIFF_V = DIFF_HEADS * 2 * DIFF_HEAD_DIM
AB_IN = S5_WIDTH + 2 * DIFF_QK + DIFF_V
LRU_WIDTH = MIX_WIDTH // 2
LRU_BLOCKS = 8
LRU_BLOCK = LRU_WIDTH // LRU_BLOCKS
LRU_C = 8.0
CONV_WIDTH = 4
POOL_WIDTH = MIX_WIDTH - LRU_WIDTH
POOL_WINDOWS = (2, 4, 8, 16)
POOL_GROUP = POOL_WIDTH // len(POOL_WINDOWS)
POOL_BUF = max(POOL_WINDOWS) - 1
CD_IN = 2 * LRU_WIDTH + POOL_WIDTH
FFN_HIDDEN = -((-8 * D_MODEL) // (3 * 256)) * 256
ROPE_THETA = 10000.0
NORM_EPS = 1e-6
SUBLN_EPS = 1e-5
Q_BLOCK = 128
N_AB_LAYERS = (DEPTH + 1) // 2
N_CD_LAYERS = DEPTH // 2

kernel_name = 'hybrid_s5_diffattn_rglru_pool_step'


def rmsnorm(x, g, eps=NORM_EPS):
    xf = x.astype(jnp.float32)
    y = xf * lax.rsqrt(jnp.mean(xf * xf, axis=-1, keepdims=True) + eps)
    return (y * g.astype(jnp.float32)).astype(x.dtype)


def rope(x, pos):
    half = x.shape[-1] // 2
    inv = ROPE_THETA ** (-jnp.arange(half, dtype=jnp.float32) / half)
    ang = pos.astype(jnp.float32)[:, None] * inv[None, :]
    cos = jnp.cos(ang)[None, :, None, :]
    sin = jnp.sin(ang)[None, :, None, :]
    xf = x.astype(jnp.float32)
    x1, x2 = xf[..., :half], xf[..., half:]
    return jnp.concatenate([x1 * cos - x2 * sin, x2 * cos + x1 * sin], axis=-1).astype(x.dtype)


def linear_scan(a, b, h0):
    def combine(left, right):
        a_l, b_l = left
        a_r, b_r = right
        return a_l * a_r, a_r * b_l + b_r
    a_cum, h = lax.associative_scan(combine, (a, b), axis=1)
    return h + a_cum * h0[:, None]


def s5_ssm(u, h0_re, h0_im, a_re, a_im, log_dt, b_re, b_im, c_re, c_im, d_skip, w_glu, b_glu):
    bsz, L, _ = u.shape
    f32 = jnp.float32
    uf = u.astype(f32)
    lam = lax.complex(a_re.astype(f32), a_im.astype(f32))
    step = jnp.exp(log_dt.astype(f32))[:, None]
    lam_bar = jnp.exp(lam * step)
    b_bar = ((lam_bar - 1.0) / lam)[:, :, None] * lax.complex(b_re.astype(f32), b_im.astype(f32))
    ug = uf.reshape(bsz, L, S5_GROUPS, S5_GROUP).astype(jnp.complex64)
    bu = jnp.einsum('blgc,gpc->blgp', ug, b_bar)
    h0 = lax.complex(h0_re.astype(f32), h0_im.astype(f32))
    h = linear_scan(jnp.broadcast_to(lam_bar, bu.shape), bu, h0)
    c = lax.complex(c_re.astype(f32), c_im.astype(f32))
    y = jnp.real(jnp.einsum('blgp,gcp->blgc', h, c)).reshape(bsz, L, S5_WIDTH) + d_skip.astype(f32) * uf
    z = jax.nn.gelu(y)
    out = z * jax.nn.sigmoid(z @ w_glu.astype(f32) + b_glu.astype(f32))
    h_last = h[:, -1]
    return out.astype(u.dtype), jnp.real(h_last), jnp.imag(h_last)


def diff_attention(q, q_pos, segments, lam, gain, lam_init):
    f32 = jnp.float32
    bsz, lq = q.shape[:2]
    qb = Q_BLOCK if lq % Q_BLOCK == 0 else lq
    nb = lq // qb
    scale = DIFF_HEAD_DIM ** -0.5
    segs = [(k.astype(f32), v.astype(f32), kp) for k, v, kp in segments]
    bounds = np.cumsum([k.shape[1] for k, _, _ in segs])[:-1].tolist()
    q_blocks = (q.astype(f32) * scale).reshape(bsz, nb, qb, 2 * DIFF_HEADS, DIFF_HEAD_DIM).swapaxes(0, 1)
    pos_blocks = q_pos.reshape(nb, qb)

    def block(args):
        qblk, pblk = args
        scores = []
        for k, _, kp in segs:
            s = jnp.einsum('bqhd,bkhd->bhqk', qblk, k)
            scores.append(jnp.where((kp[None, :] <= pblk[:, None])[None, None], s, -jnp.inf))
        p = jax.nn.softmax(jnp.concatenate(scores, axis=-1), axis=-1)
        p = p.reshape(bsz, DIFF_HEADS, 2, qb, -1)
        w = p[:, :, 0] - lam * p[:, :, 1]
        parts = jnp.split(w, bounds, axis=-1)
        o = jnp.einsum('bhqk,bkhe->bqhe', parts[0], segs[0][1])
        for wp, (_, v, _) in zip(parts[1:], segs[1:]):
            o = o + jnp.einsum('bhqk,bkhe->bqhe', wp, v)
        return o

    o = lax.map(block, (q_blocks, pos_blocks))
    o = o.swapaxes(0, 1).reshape(bsz, lq, DIFF_HEADS, 2 * DIFF_HEAD_DIM)
    o = rmsnorm(o, gain, SUBLN_EPS) * (1.0 - lam_init)
    return o.reshape(bsz, lq, DIFF_V).astype(q.dtype)


def ab_mixer(h, pos, past, h0_re, h0_im, w_in, w_out, a_re, a_im, log_dt, b_re, b_im, c_re, c_im,
             d_skip, w_glu, b_glu, lq1, lk1, lq2, lk2, subln, lam_init):
    bsz, L, _ = h.shape
    f32 = jnp.float32
    proj = h @ w_in
    u, q, k, v = jnp.split(proj, [S5_WIDTH, S5_WIDTH + DIFF_QK, S5_WIDTH + 2 * DIFF_QK], axis=-1)
    q = rope(q.reshape(bsz, L, 2 * DIFF_HEADS, DIFF_HEAD_DIM), pos)
    k = rope(k.reshape(bsz, L, 2 * DIFF_HEADS, DIFF_HEAD_DIM), pos)
    v = v.reshape(bsz, L, DIFF_HEADS, 2 * DIFF_HEAD_DIM)
    lam = (jnp.exp(jnp.sum(lq1.astype(f32) * lk1.astype(f32)))
           - jnp.exp(jnp.sum(lq2.astype(f32) * lk2.astype(f32))) + lam_init)
    segments = ([] if past is None else [past]) + [(k, v, pos)]
    attn = diff_attention(q, pos, segments, lam, subln, lam_init)
    s5_out, h_re, h_im = s5_ssm(u, h0_re, h0_im, a_re, a_im, log_dt, b_re, b_im, c_re, c_im, d_skip, w_glu, b_glu)
    out = jnp.concatenate([s5_out.astype(h.dtype), attn.astype(h.dtype)], axis=-1) @ w_out
    return out, k, v, h_re, h_im


def block_diag(x, w, b):
    bsz, L, _ = x.shape
    xb = x.reshape(bsz, L, LRU_BLOCKS, LRU_BLOCK)
    return jnp.einsum('blnc,ncd->blnd', xb, w.astype(jnp.float32)).reshape(bsz, L, LRU_WIDTH) + b.astype(jnp.float32)


def pool_mix(xp, pos, buf, pool_w, pool_scale):
    f32 = jnp.float32
    bsz, L, _ = xp.shape
    ext = jnp.concatenate([buf.astype(xp.dtype), xp], axis=1)
    ef = ext.astype(f32)
    cs = jnp.concatenate([jnp.zeros((bsz, 1, POOL_WIDTH), f32), jnp.cumsum(ef, axis=1)], axis=1)
    end = cs[:, POOL_BUF + 1:]
    xf = ef[:, POOL_BUF:]
    outs = []
    for g, w in enumerate(POOL_WINDOWS):
        sl = slice(g * POOL_GROUP, (g + 1) * POOL_GROUP)
        win = end[..., sl] - cs[:, POOL_BUF + 1 - w:POOL_BUF + 1 - w + L, sl]
        cnt = jnp.minimum(pos + 1, w).astype(f32)[None, :, None]
        outs.append(win / cnt - xf[..., sl])
    pooled = jnp.stack(outs, axis=2)
    y = jnp.einsum('blgc,gcd->blgd', pooled, pool_w.astype(f32)).reshape(bsz, L, POOL_WIDTH)
    return y * pool_scale.astype(f32), ext[:, L:]


def cd_mixer(h, pos, conv_buf, lru_h0, pool_buf, w_in, w_out, conv_w, conv_b, wa, ba, wx, bx,
             lru_lambda, pool_w, pool_scale):
    f32 = jnp.float32
    bsz, L, _ = h.shape
    proj = h @ w_in
    gate, xl, xp = jnp.split(proj, [LRU_WIDTH, 2 * LRU_WIDTH], axis=-1)
    ext = jnp.concatenate([conv_buf.astype(xl.dtype), xl], axis=1)
    xc = conv_b.astype(f32) + ext[:, 0:L].astype(f32) * conv_w[0].astype(f32)
    for j in range(1, CONV_WIDTH):
        xc = xc + ext[:, j:j + L].astype(f32) * conv_w[j].astype(f32)
    r = jax.nn.sigmoid(block_diag(xc, wa, ba))
    i = jax.nn.sigmoid(block_diag(xc, wx, bx))
    log_a = -LRU_C * r * jax.nn.softplus(-lru_lambda.astype(f32))
    a = jnp.exp(log_a)
    b = jnp.sqrt(-jnp.expm1(2.0 * log_a)) * (i * xc)
    hs = linear_scan(a, b, lru_h0.astype(f32))
    lru_out = jax.nn.gelu(gate.astype(f32)) * hs
    pool_out, new_pool = pool_mix(xp, pos, pool_buf, pool_w, pool_scale)
    out = jnp.concatenate([lru_out.astype(h.dtype), pool_out.astype(h.dtype)], axis=-1) @ w_out
    return out, ext[:, L:], hs[:, -1], new_pool


def swiglu(x, wg, wu, wd):
    return (jax.nn.silu(x @ wg) * (x @ wu)) @ wd


def setup_inputs(seed: int = 0) -> dict:
    key = jax.random.key(seed)
    keys = jax.random.split(key, 64)
    counter = [0]

    def nk():
        counter[0] += 1
        return keys[counter[0] - 1]

    def nrm(shape, scale=1.0):
        return jax.random.normal(nk(), shape, jnp.float32) * scale

    f32 = jnp.float32
    n_pages = PAST_LEN // PAGE_SIZE
    n_pool = (5 * DEC_BATCH * n_pages + 3) // 4
    nab, ncd = N_AB_LAYERS, N_CD_LAYERS
    h2, dh = 2 * DIFF_HEADS, DIFF_HEAD_DIM

    x_prompt = nrm((BATCH, SEQ, D_MODEL))
    x_sample = nrm((DEC_BATCH, DEC_SEQ, D_MODEL))
    cache_k = nrm((nab, n_pool, PAGE_SIZE, h2, dh))
    cache_v = nrm((nab, n_pool, PAGE_SIZE, DIFF_HEADS, 2 * dh))
    page_table = jax.random.permutation(nk(), n_pool)[:DEC_BATCH * n_pages].reshape(DEC_BATCH, n_pages).astype(jnp.int32)
    state_s5_re = nrm((nab, DEC_BATCH, S5_GROUPS, S5_STATE), 0.3)
    state_s5_im = nrm((nab, DEC_BATCH, S5_GROUPS, S5_STATE), 0.3)
    state_conv = nrm((ncd, DEC_BATCH, CONV_WIDTH - 1, LRU_WIDTH))
    state_lru = nrm((ncd, DEC_BATCH, LRU_WIDTH), 0.5)
    state_pool = nrm((ncd, DEC_BATCH, POOL_BUF, POOL_WIDTH))

    norm_mix = 1.0 + nrm((DEPTH, D_MODEL), 0.02)
    norm_ffn = 1.0 + nrm((DEPTH, D_MODEL), 0.02)
    norm_final = 1.0 + nrm((D_MODEL,), 0.02)

    w_in_ab = nrm((nab, D_MODEL, AB_IN), D_MODEL ** -0.5)
    w_out_ab = nrm((nab, MIX_WIDTH, D_MODEL), MIX_WIDTH ** -0.5)
    s5_a_re = -0.5 + nrm((nab, S5_GROUPS, S5_STATE), 0.01)
    s5_a_im = math.pi * jnp.arange(S5_STATE, dtype=f32) + nrm((nab, S5_GROUPS, S5_STATE), 0.01)
    s5_log_dt = jax.random.uniform(nk(), (nab, S5_GROUPS), f32, math.log(1e-3), math.log(1e-1))
    s5_b_re = nrm((nab, S5_GROUPS, S5_STATE, S5_GROUP), (2 * S5_GROUP) ** -0.5)
    s5_b_im = nrm((nab, S5_GROUPS, S5_STATE, S5_GROUP), (2 * S5_GROUP) ** -0.5)
    s5_c_re = nrm((nab, S5_GROUPS, S5_GROUP, S5_STATE), (2 * S5_STATE) ** -0.5)
    s5_c_im = nrm((nab, S5_GROUPS, S5_GROUP, S5_STATE), (2 * S5_STATE) ** -0.5)
    s5_d = nrm((nab, S5_WIDTH))
    s5_w_glu = nrm((nab, S5_WIDTH, S5_WIDTH), S5_WIDTH ** -0.5)
    s5_b_glu = nrm((nab, S5_WIDTH), 0.01)
    diff_lq1 = nrm((nab, DIFF_HEAD_DIM), 0.1)
    diff_lk1 = nrm((nab, DIFF_HEAD_DIM), 0.1)
    diff_lq2 = nrm((nab, DIFF_HEAD_DIM), 0.1)
    diff_lk2 = nrm((nab, DIFF_HEAD_DIM), 0.1)
    diff_subln = 1.0 + nrm((nab, 2 * DIFF_HEAD_DIM), 0.02)

    w_in_cd = nrm((ncd, D_MODEL, CD_IN), D_MODEL ** -0.5)
    w_out_cd = nrm((ncd, MIX_WIDTH, D_MODEL), MIX_WIDTH ** -0.5)
    conv_w = nrm((ncd, CONV_WIDTH, LRU_WIDTH), CONV_WIDTH ** -0.5)
    conv_b = nrm((ncd, LRU_WIDTH), 0.01)
    lru_wa = nrm((ncd, LRU_BLOCKS, LRU_BLOCK, LRU_BLOCK), LRU_BLOCK ** -0.5)
    lru_ba = nrm((ncd, LRU_WIDTH), 0.01)
    lru_wx = nrm((ncd, LRU_BLOCKS, LRU_BLOCK, LRU_BLOCK), LRU_BLOCK ** -0.5)
    lru_bx = nrm((ncd, LRU_WIDTH), 0.01)
    lru_u = jax.random.uniform(nk(), (ncd, LRU_WIDTH), f32, 0.9, 0.999)
    lru_sig = lru_u ** (1.0 / LRU_C)
    lru_lambda = jnp.log(lru_sig) - jnp.log1p(-lru_sig)
    pool_w = nrm((ncd, len(POOL_WINDOWS), POOL_GROUP, POOL_GROUP), POOL_GROUP ** -0.5)
    pool_scale = 1.0 + nrm((ncd, POOL_WIDTH), 0.1)

    ffn_w_gate = nrm((DEPTH, D_MODEL, FFN_HIDDEN), D_MODEL ** -0.5)
    ffn_w_up = nrm((DEPTH, D_MODEL, FFN_HIDDEN), D_MODEL ** -0.5)
    ffn_w_down = nrm((DEPTH, FFN_HIDDEN, D_MODEL), FFN_HIDDEN ** -0.5)

    return {'x_prompt': x_prompt, 'x_sample': x_sample, 'cache_k': cache_k, 'cache_v': cache_v,
            'page_table': page_table, 'state_s5_re': state_s5_re, 'state_s5_im': state_s5_im,
            'state_conv': state_conv, 'state_lru': state_lru, 'state_pool': state_pool,
            'norm_mix': norm_mix, 'norm_ffn': norm_ffn, 'norm_final': norm_final,
            'w_in_ab': w_in_ab, 'w_out_ab': w_out_ab, 's5_a_re': s5_a_re, 's5_a_im': s5_a_im,
            's5_log_dt': s5_log_dt, 's5_b_re': s5_b_re, 's5_b_im': s5_b_im, 's5_c_re': s5_c_re,
            's5_c_im': s5_c_im, 's5_d': s5_d, 's5_w_glu': s5_w_glu, 's5_b_glu': s5_b_glu,
            'diff_lq1': diff_lq1, 'diff_lk1': diff_lk1, 'diff_lq2': diff_lq2, 'diff_lk2': diff_lk2,
            'diff_subln': diff_subln, 'w_in_cd': w_in_cd, 'w_out_cd': w_out_cd, 'conv_w': conv_w,
            'conv_b': conv_b, 'lru_wa': lru_wa, 'lru_ba': lru_ba, 'lru_wx': lru_wx, 'lru_bx': lru_bx,
            'lru_lambda': lru_lambda, 'pool_w': pool_w, 'pool_scale': pool_scale,
            'ffn_w_gate': ffn_w_gate, 'ffn_w_up': ffn_w_up, 'ffn_w_down': ffn_w_down}


def reference(x_prompt, x_sample, cache_k, cache_v, page_table, state_s5_re, state_s5_im, state_conv,
              state_lru, state_pool, norm_mix, norm_ffn, norm_final, w_in_ab, w_out_ab, s5_a_re, s5_a_im,
              s5_log_dt, s5_b_re, s5_b_im, s5_c_re, s5_c_im, s5_d, s5_w_glu, s5_b_glu, diff_lq1, diff_lk1,
              diff_lq2, diff_lk2, diff_subln, w_in_cd, w_out_cd, conv_w, conv_b, lru_wa, lru_ba, lru_wx,
              lru_bx, lru_lambda, pool_w, pool_scale, ffn_w_gate, ffn_w_up, ffn_w_down):
    n_seq_dec, n_pages = page_table.shape
    past_len = n_pages * PAGE_SIZE

    def run(x, pos, sample):
        bsz = x.shape[0]
        new = {'k': [], 'v': [], 're': [], 'im': [], 'conv': [], 'lru': [], 'pool': []}
        for l in range(DEPTH):
            j = l // 2
            hn = rmsnorm(x, norm_mix[l])
            if l % 2 == 0:
                if sample:
                    k_past = cache_k[j][page_table].reshape(n_seq_dec, past_len, 2 * DIFF_HEADS, DIFF_HEAD_DIM)
                    v_past = cache_v[j][page_table].reshape(n_seq_dec, past_len, DIFF_HEADS, 2 * DIFF_HEAD_DIM)
                    past = (k_past, v_past, jnp.arange(past_len, dtype=jnp.int32))
                    h0_re, h0_im = state_s5_re[j], state_s5_im[j]
                else:
                    past = None
                    h0_re = jnp.zeros((bsz, S5_GROUPS, S5_STATE), jnp.float32)
                    h0_im = jnp.zeros((bsz, S5_GROUPS, S5_STATE), jnp.float32)
                lam_init = 0.8 - 0.6 * math.exp(-0.3 * l)
                mix, k_new, v_new, h_re, h_im = ab_mixer(
                    hn, pos, past, h0_re, h0_im, w_in_ab[j], w_out_ab[j], s5_a_re[j], s5_a_im[j],
                    s5_log_dt[j], s5_b_re[j], s5_b_im[j], s5_c_re[j], s5_c_im[j], s5_d[j], s5_w_glu[j],
                    s5_b_glu[j], diff_lq1[j], diff_lk1[j], diff_lq2[j], diff_lk2[j], diff_subln[j], lam_init)
                new['k'].append(k_new)
                new['v'].append(v_new)
                new['re'].append(h_re)
                new['im'].append(h_im)
            else:
                if sample:
                    conv_buf, lru_h0, pool_buf = state_conv[j], state_lru[j], state_pool[j]
                else:
                    conv_buf = jnp.zeros((bsz, CONV_WIDTH - 1, LRU_WIDTH), x.dtype)
                    lru_h0 = jnp.zeros((bsz, LRU_WIDTH), jnp.float32)
                    pool_buf = jnp.zeros((bsz, POOL_BUF, POOL_WIDTH), x.dtype)
                mix, c_new, h_new, p_new = cd_mixer(
                    hn, pos, conv_buf, lru_h0, pool_buf, w_in_cd[j], w_out_cd[j], conv_w[j], conv_b[j],
                    lru_wa[j], lru_ba[j], lru_wx[j], lru_bx[j], lru_lambda[j], pool_w[j], pool_scale[j])
                new['conv'].append(c_new)
                new['lru'].append(h_new)
                new['pool'].append(p_new)
            x = x + mix
            x = x + swiglu(rmsnorm(x, norm_ffn[l]), ffn_w_gate[l], ffn_w_up[l], ffn_w_down[l])
        y = rmsnorm(x, norm_final)
        return y, {name: jnp.stack(vals) for name, vals in new.items()}

    pos_p = jnp.arange(x_prompt.shape[1], dtype=jnp.int32)
    pos_s = past_len + jnp.arange(x_sample.shape[1], dtype=jnp.int32)
    y_prompt, sp = run(x_prompt, pos_p, False)
    y_sample, ss = run(x_sample, pos_s, True)
    return (y_prompt, y_sample, sp['k'], sp['v'], ss['k'], ss['v'], sp['re'], sp['im'], ss['re'], ss['im'],
            sp['conv'], ss['conv'], sp['lru'], ss['lru'], sp['pool'], ss['pool'])
```

```python
import functools
import math

import jax
import jax.numpy as jnp
from jax import lax
from jax.experimental import pallas as pl
from jax.experimental.pallas import tpu as pltpu

F32 = jnp.float32
BF16 = jnp.bfloat16

HEAD_DIM = 64
ROPE_THETA = 10000.0
NORM_EPS = 1e-6
SUBLN_EPS = 1e-5
S5_GROUP = 16
S5_CHUNK = 16
S5_SEGS = 8
LRU_C = 8.0
CONV_WIDTH = 4
POOL_WINDOWS = (2, 4, 8, 16)
POOL_BUF = max(POOL_WINDOWS) - 1

LANES = 128
SUBLANES = 8
MXU_DIM = 256
VMEM_LIMIT = 56 * 1024 * 1024

NEG_BIG = -1e30


def _cparams(n_axes):
    return pltpu.CompilerParams(dimension_semantics=("arbitrary",) * n_axes,
                                vmem_limit_bytes=VMEM_LIMIT)


def _rms(x, g, eps):
    ms = jnp.mean(x * x, axis=-1, keepdims=True)
    return x * lax.rsqrt(ms + eps) * g


def _lane_tile(x, reps):
    return jnp.concatenate([x] * reps, axis=1)


def _rope(x, cos, sin_signed):
    width = x.shape[1]
    lane = lax.broadcasted_iota(jnp.int32, x.shape, 1)
    half = HEAD_DIM // 2
    rot = jnp.where((lane & half) == 0, pltpu.roll(x, width - half, 1), pltpu.roll(x, half, 1))
    return x * cos + rot * sin_signed


def _inproj_ab_kernel(x_ref, g_ref, w_ref, cos_ref, sin_ref,
                      u_ref, q_ref, k_ref, v_ref, kb_ref, vb_ref, *, s5w, qkw):
    h = _rms(x_ref[...], g_ref[...], NORM_EPS).astype(BF16)
    proj = jnp.dot(h, w_ref[...], preferred_element_type=F32)
    reps = qkw // LANES
    cos = _lane_tile(cos_ref[...], reps)
    sin = _lane_tile(sin_ref[...], reps)
    u_ref[...] = proj[:, :s5w]
    q = _rope(proj[:, s5w:s5w + qkw], cos, sin)
    k = _rope(proj[:, s5w + qkw:s5w + 2 * qkw], cos, sin)
    v = proj[:, s5w + 2 * qkw:]
    q_ref[...] = (q * (HEAD_DIM ** -0.5)).astype(BF16)
    k_ref[...] = k
    v_ref[...] = v
    kb_ref[...] = k.astype(BF16)
    vb_ref[...] = v.astype(BF16)


def _inproj_ab(x, g, w_bf, cos_tab, sin_tab, tm, s5w, qkw):
    n, d = x.shape
    wn = w_bf.shape[1]
    vw = wn - s5w - 2 * qkw
    period = cos_tab.shape[0] // tm
    row = lambda i: (i, 0)
    const = lambda i: (0, 0)
    tab = lambda i: (i % period, 0)
    return pl.pallas_call(
        functools.partial(_inproj_ab_kernel, s5w=s5w, qkw=qkw),
        out_shape=(jax.ShapeDtypeStruct((n, s5w), F32), jax.ShapeDtypeStruct((n, qkw), BF16),
                   jax.ShapeDtypeStruct((n, qkw), F32), jax.ShapeDtypeStruct((n, vw), F32),
                   jax.ShapeDtypeStruct((n, qkw), BF16), jax.ShapeDtypeStruct((n, vw), BF16)),
        grid=(n // tm,),
        in_specs=[pl.BlockSpec((tm, d), row), pl.BlockSpec((1, d), const),
                  pl.BlockSpec((d, wn), const), pl.BlockSpec((tm, LANES), tab),
                  pl.BlockSpec((tm, LANES), tab)],
        out_specs=(pl.BlockSpec((tm, s5w), row), pl.BlockSpec((tm, qkw), row),
                   pl.BlockSpec((tm, qkw), row), pl.BlockSpec((tm, vw), row),
                   pl.BlockSpec((tm, qkw), row), pl.BlockSpec((tm, vw), row)),
        compiler_params=_cparams(1),
        name="inproj_ab",
    )(x, g, w_bf, cos_tab, sin_tab)


def _diff_lambda(lamp, lam_init):
    s1 = jnp.sum(lamp[0:1] * lamp[1:2], axis=1, keepdims=True)
    s2 = jnp.sum(lamp[2:3] * lamp[3:4], axis=1, keepdims=True)
    return jnp.exp(s1) - jnp.exp(s2) + lam_init


def _subln(o, gain, lam_init):
    ms = jnp.mean(o * o, axis=-1, keepdims=True)
    return o * lax.rsqrt(ms + SUBLN_EPS) * gain * (1.0 - lam_init)


def _attn_prompt_kernel(lamp_ref, q_ref, k_ref, v_ref, gain_ref, o_ref,
                        vext_ref, m_ref, acc_ref, *, tq, lam_init):
    qi = pl.program_id(2)
    vw = v_ref.shape[1]

    @pl.when(qi == 0)
    def _():
        vext_ref[:, :vw] = v_ref[...]
        vext_ref[:, vw:] = jnp.ones((v_ref.shape[0], vw), BF16)

    q = q_ref[...]
    lane = lax.broadcasted_iota(jnp.int32, q.shape, 1)
    zero = jnp.zeros_like(q)
    qs = (jnp.where(lane < HEAD_DIM, q, zero), jnp.where(lane >= HEAD_DIM, q, zero))
    m_ref[...] = jnp.full(m_ref.shape, NEG_BIG, F32)
    acc_ref[...] = jnp.zeros(acc_ref.shape, F32)

    def chunk(c, masked):
        off = pl.multiple_of(c * tq, tq)
        kc = k_ref[pl.ds(off, tq), :]
        vc = vext_ref[pl.ds(off, tq), :]
        for j in range(2):
            s = lax.dot_general(qs[j], kc, (((1,), (1,)), ((), ())), preferred_element_type=F32)
            if masked:
                row = lax.broadcasted_iota(jnp.int32, s.shape, 0)
                col = lax.broadcasted_iota(jnp.int32, s.shape, 1)
                s = jnp.where(col <= row, s, -jnp.inf)
            m_prev = m_ref[j]
            m_new = jnp.maximum(m_prev, jnp.max(s, axis=1, keepdims=True))
            alpha = jnp.exp(m_prev - m_new)
            p = jnp.exp(s - _lane_tile(m_new, tq // LANES))
            pv = jnp.dot(p.astype(BF16), vc, preferred_element_type=F32)
            acc_ref[j] = _lane_tile(alpha, 2 * vw // LANES) * acc_ref[j] + pv
            m_ref[j] = m_new

    def body(c, carry):
        chunk(c, False)
        return carry

    lax.fori_loop(0, qi, body, 0)
    chunk(qi, True)

    lam = _diff_lambda(lamp_ref[...], lam_init)
    a0 = acc_ref[0]
    a1 = acc_ref[1]
    o = a0[:, :vw] / a0[:, vw:] - lam * (a1[:, :vw] / a1[:, vw:])
    o_ref[...] = _subln(o, gain_ref[...], lam_init).astype(o_ref.dtype)


def _attn_prompt(q_bf, k_bf, v_bf, lamp, gain, bsz, seq, tq, lam_init):
    n, qkw = q_bf.shape
    vw_total = v_bf.shape[1]
    heads = qkw // (2 * HEAD_DIM)
    vw = vw_total // heads
    nq = seq // tq
    return pl.pallas_call(
        functools.partial(_attn_prompt_kernel, tq=tq, lam_init=lam_init),
        out_shape=jax.ShapeDtypeStruct((n, vw_total), BF16),
        grid=(bsz, heads, nq),
        in_specs=[pl.BlockSpec((4, HEAD_DIM), lambda b, h, i: (0, 0)),
                  pl.BlockSpec((tq, 2 * HEAD_DIM), lambda b, h, i: (b * nq + i, h)),
                  pl.BlockSpec((seq, 2 * HEAD_DIM), lambda b, h, i: (b, h)),
                  pl.BlockSpec((seq, vw), lambda b, h, i: (b, h)),
                  pl.BlockSpec((1, vw), lambda b, h, i: (0, 0))],
        out_specs=pl.BlockSpec((tq, vw), lambda b, h, i: (b * nq + i, h)),
        scratch_shapes=[pltpu.VMEM((seq, 2 * vw), BF16),
                        pltpu.VMEM((2, tq, LANES), F32),
                        pltpu.VMEM((2, tq, 2 * vw), F32)],
        compiler_params=_cparams(3),
        name="attn_prompt",
    )(lamp, q_bf, k_bf, v_bf, gain)


def _attn_decode_kernel(pt_ref, lamp_ref, q_ref, kself_ref, vself_ref, gain_ref, *rest,
                        pages, lam_init, vw):
    k_refs = rest[:pages]
    v_refs = rest[pages:2 * pages]
    o_ref = rest[2 * pages]
    m_ref, l_ref, acc_ref = rest[2 * pages + 1:]
    j = pl.program_id(1)
    nh2 = m_ref.shape[0]
    width = q_ref.shape[1]

    @pl.when(j == 0)
    def _():
        m_ref[...] = jnp.full(m_ref.shape, NEG_BIG, F32)
        l_ref[...] = jnp.zeros(l_ref.shape, F32)
        acc_ref[...] = jnp.zeros(acc_ref.shape, F32)

    rowh = lax.broadcasted_iota(jnp.int32, (nh2, width), 0)
    laneh = lax.broadcasted_iota(jnp.int32, (nh2, width), 1) // HEAD_DIM
    q8 = jnp.where(rowh == laneh, jnp.broadcast_to(q_ref[...], (nh2, width)), 0.0)
    q8b = q8.astype(BF16)

    s_parts = [lax.dot_general(q8b, k_refs[i][...].astype(BF16), (((1,), (1,)), ((), ())),
                               preferred_element_type=F32) for i in range(pages)]
    s = jnp.concatenate(s_parts, axis=1)
    m_prev = m_ref[...]
    m_new = jnp.maximum(m_prev, jnp.max(s, axis=1, keepdims=True))
    alpha = jnp.exp(m_prev - m_new)
    p = jnp.exp(s - _lane_tile(m_new, s.shape[1] // LANES))
    l_ref[...] = alpha * l_ref[...] + jnp.sum(p, axis=1, keepdims=True)
    ps = k_refs[0].shape[0]
    pv = jnp.dot(p[:, :ps].astype(BF16), v_refs[0][...].astype(BF16), preferred_element_type=F32)
    for i in range(1, pages):
        pv = pv + jnp.dot(p[:, i * ps:(i + 1) * ps].astype(BF16), v_refs[i][...].astype(BF16),
                          preferred_element_type=F32)
    acc_ref[...] = _lane_tile(alpha, width // LANES) * acc_ref[...] + pv
    m_ref[...] = m_new

    @pl.when(j == pl.num_programs(1) - 1)
    def _():
        s_self = jnp.sum(q8 * kself_ref[...], axis=1, keepdims=True)
        m_old = m_ref[...]
        m_fin = jnp.maximum(m_old, s_self)
        a_fin = jnp.exp(m_old - m_fin)
        p_self = jnp.exp(s_self - m_fin)
        l_fin = a_fin * l_ref[...] + p_self
        acc = (_lane_tile(a_fin, width // LANES) * acc_ref[...]
               + _lane_tile(p_self, width // LANES) * vself_ref[...])
        o8 = acc / _lane_tile(l_fin, width // LANES)
        lam = _diff_lambda(lamp_ref[...], lam_init)
        vhead = lax.broadcasted_iota(jnp.int32, (nh2, width), 1) // vw
        sign = jnp.where((rowh & 1) == 0, 1.0, -lam)
        coef = jnp.where(vhead == rowh // 2, sign, 0.0)
        o = jnp.sum(o8 * coef, axis=0, keepdims=True)
        gain = gain_ref[...]
        outs = [_subln(o[:, h * vw:(h + 1) * vw], gain, lam_init) for h in range(width // vw)]
        o_ref[...] = jnp.concatenate(outs, axis=1)


def _attn_decode(q, k_self, v_self, cache_k, cache_v, page_table, lamp, gain, pages, lam_init):
    nseq, width = q.shape
    npages = page_table.shape[1]
    ps = cache_k.shape[1]
    vw = gain.shape[1]
    nh2 = width // HEAD_DIM
    pt = page_table.reshape(-1)
    q3, k3, v3 = (a.reshape(nseq, 1, width) for a in (q, k_self, v_self))
    seq_spec = pl.BlockSpec((None, 1, width), lambda b, j, pt: (b, 0, 0))

    def page_spec(i):
        return pl.BlockSpec((None, ps, width),
                            lambda b, j, pt: (pt[b * npages + j * pages + i], 0, 0))

    grid_spec = pltpu.PrefetchScalarGridSpec(
        num_scalar_prefetch=1,
        grid=(nseq, npages // pages),
        in_specs=[pl.BlockSpec((4, HEAD_DIM), lambda b, j, pt: (0, 0)), seq_spec, seq_spec, seq_spec,
                  pl.BlockSpec((1, vw), lambda b, j, pt: (0, 0))]
                 + [page_spec(i) for i in range(pages)] + [page_spec(i) for i in range(pages)],
        out_specs=seq_spec,
        scratch_shapes=[pltpu.VMEM((nh2, LANES), F32), pltpu.VMEM((nh2, LANES), F32),
                        pltpu.VMEM((nh2, width), F32)],
    )
    out = pl.pallas_call(
        functools.partial(_attn_decode_kernel, pages=pages, lam_init=lam_init, vw=vw),
        out_shape=jax.ShapeDtypeStruct((nseq, 1, width), F32),
        grid_spec=grid_spec,
        compiler_params=_cparams(2),
        name="attn_decode",
    )(pt, lamp, q3, k3, v3, gain, *([cache_k] * pages), *([cache_v] * pages))
    return out.reshape(nseq, width)


def _s5_discretize(a_re, a_im, log_dt, b_re, b_im):
    lam = lax.complex(a_re.astype(F32), a_im.astype(F32))
    step = jnp.exp(log_dt.astype(F32))[:, None]
    lam_bar = jnp.exp(lam * step)
    b_bar = ((lam_bar - 1.0) / lam)[:, :, None] * lax.complex(b_re.astype(F32), b_im.astype(F32))
    return lam * step, b_bar


def _s5_power(log_lam_bar, k):
    return jnp.exp(log_lam_bar * k)


def _reim(z, swap=False):
    parts = (jnp.imag(z), jnp.real(z)) if swap else (jnp.real(z), jnp.imag(z))
    return jnp.concatenate(parts, axis=-1)


def _s5_prompt_tables(params, seq):
    a_re, a_im, log_dt, b_re, b_im, c_re, c_im, d_skip = params
    g, p = a_re.shape
    ch, t = S5_GROUP, S5_CHUNK
    llb, b_bar = _s5_discretize(a_re, a_im, log_dt, b_re, b_im)
    c = lax.complex(c_re.astype(F32), c_im.astype(F32))
    lags = jnp.arange(t + 1, dtype=F32)
    powk = _s5_power(llb[None], lags[:, None, None])
    klag = jnp.real(jnp.einsum('gcp,kgp,gpd->gkcd', c, powk[:t], b_bar))
    s_idx = jnp.arange(t)[:, None]
    t_idx = jnp.arange(t)[None, :]
    lag = t_idx - s_idx
    blocks = klag[:, jnp.clip(lag, 0, t - 1)]
    blocks = jnp.where((lag >= 0)[None, :, :, None, None], blocks, 0.0)
    eye = jnp.eye(ch, dtype=F32)
    skip = d_skip.astype(F32).reshape(g, ch)[:, None, None, :, None] * eye[None, None, None]
    blocks = blocks + jnp.where((lag == 0)[None, :, :, None, None], skip, 0.0)
    m_mat = blocks.transpose(0, 1, 4, 2, 3).reshape(g, t * ch, t * ch)
    e_c = powk[:t][::-1].transpose(1, 0, 2)[:, :, None, :] * b_bar.transpose(0, 2, 1)[:, None]
    e_c = e_c.reshape(g, t * ch, p)
    e_mat = jnp.concatenate([_reim(e_c), _reim(e_c, swap=True)], axis=-1)
    cp = c[:, None] * powk[1:t + 1].transpose(1, 0, 2)[:, :, None, :]
    cp = cp.reshape(g, t * ch, p)
    f_mat = jnp.concatenate([jnp.real(cp), -jnp.imag(cp)], axis=-1).transpose(0, 2, 1)
    nsub = seq // t
    r_steps = nsub // S5_SEGS
    ks = jnp.concatenate([jnp.arange(r_steps + 1, dtype=F32) * t,
                          jnp.array([1.0, 2.0, 4.0], F32) * (r_steps * t)])
    pw = _s5_power(llb[None], ks[:, None, None]).transpose(1, 0, 2)
    pad = (-pw.shape[1]) % SUBLANES
    pw = jnp.pad(pw, ((0, 0), (0, pad), (0, 0)))
    pa1 = jnp.concatenate([jnp.real(pw), jnp.real(pw)], axis=-1)
    pa2 = jnp.concatenate([-jnp.imag(pw), jnp.imag(pw)], axis=-1)
    return m_mat.astype(BF16), e_mat.astype(BF16), f_mat.astype(BF16), pa1, pa2


def _shift_rows(x, d):
    row = lax.broadcasted_iota(jnp.int32, x.shape, 0)
    return jnp.where(row >= d, pltpu.roll(x, d, 0), 0.0)


def _s5_prompt_kernel(u_ref, e_ref, m_ref, f_ref, pa1_ref, pa2_ref, y_ref, hl_ref,
                      ee_ref, hloc_ref, hin_ref, *, bsz, r_steps):
    sw = 2 * (e_ref.shape[1] // 4)
    u = u_ref[...]
    ee_ref[...] = jnp.dot(u, e_ref[...], preferred_element_type=F32)
    rows_b = r_steps * S5_SEGS
    a1 = pa1_ref[1:2, :]
    a2 = pa2_ref[1:2, :]
    for b in range(bsz):
        base = b * rows_b

        def step(r, carry):
            h, hs = carry
            row = pl.multiple_of(base + r * S5_SEGS, S5_SEGS)
            e_r = ee_ref[pl.ds(row, S5_SEGS), :sw]
            es_r = ee_ref[pl.ds(row, S5_SEGS), sw:]
            hn = a1 * h + a2 * hs + e_r
            hsn = a1 * hs - a2 * h + es_r
            hloc_ref[pl.ds(row, S5_SEGS), :] = hn
            return hn, hsn

        zero = jnp.zeros((S5_SEGS, sw), F32)
        c, cs = lax.fori_loop(0, r_steps, step, (zero, zero))
        for i, d in enumerate((1, 2, 4)):
            s1 = pa1_ref[r_steps + 1 + i:r_steps + 2 + i, :]
            s2 = pa2_ref[r_steps + 1 + i:r_steps + 2 + i, :]
            sh = _shift_rows(c, d)
            shs = _shift_rows(cs, d)
            c, cs = c + s1 * sh + s2 * shs, cs + s1 * shs - s2 * sh
        hl_ref[b:b + 1, :] = c[S5_SEGS - 1:S5_SEGS, :]
        cin = _shift_rows(c, 1)
        cins = _shift_rows(cs, 1)
        hin_ref[pl.ds(base, S5_SEGS), :] = cin

        def fix(r, carry):
            row = pl.multiple_of(base + r * S5_SEGS, S5_SEGS)
            prev = hloc_ref[pl.ds(row - S5_SEGS, S5_SEGS), :]
            hin = prev + pa1_ref[pl.ds(r, 1), :] * cin + pa2_ref[pl.ds(r, 1), :] * cins
            hin_ref[pl.ds(row, S5_SEGS), :] = hin
            return carry

        lax.fori_loop(1, r_steps, fix, 0)
    y_ref[...] = (jnp.dot(u, m_ref[...], preferred_element_type=F32)
                  + jnp.dot(hin_ref[...].astype(BF16), f_ref[...], preferred_element_type=F32))


def _s5_prompt(u, tables, bsz, seq):
    m_mat, e_mat, f_mat, pa1, pa2 = tables
    g = m_mat.shape[0]
    p = f_mat.shape[1] // 2
    t, ch, segs = S5_CHUNK, S5_GROUP, S5_SEGS
    r_steps = seq // (t * segs)
    rows = bsz * segs * r_steps
    ug = (u.astype(BF16).reshape(bsz, segs, r_steps, t, g, ch)
          .transpose(4, 0, 2, 1, 3, 5).reshape(g, rows, t * ch))
    npw = pa1.shape[1]
    grp = lambda i: (i, 0, 0)
    y, hl = pl.pallas_call(
        functools.partial(_s5_prompt_kernel, bsz=bsz, r_steps=r_steps),
        out_shape=(jax.ShapeDtypeStruct((g, rows, t * ch), F32),
                   jax.ShapeDtypeStruct((g, bsz, 2 * p), F32)),
        grid=(g,),
        in_specs=[pl.BlockSpec((None, rows, t * ch), grp),
                  pl.BlockSpec((None, t * ch, 4 * p), grp),
                  pl.BlockSpec((None, t * ch, t * ch), grp),
                  pl.BlockSpec((None, 2 * p, t * ch), grp),
                  pl.BlockSpec((None, npw, 2 * p), grp),
                  pl.BlockSpec((None, npw, 2 * p), grp)],
        out_specs=(pl.BlockSpec((None, rows, t * ch), grp),
                   pl.BlockSpec((None, bsz, 2 * p), grp)),
        scratch_shapes=[pltpu.VMEM((rows, 4 * p), F32), pltpu.VMEM((rows, 2 * p), F32),
                        pltpu.VMEM((rows, 2 * p), F32)],
        compiler_params=_cparams(1),
        name="s5_prompt",
    )(ug, e_mat, m_mat, f_mat, pa1, pa2)
    y = (y.reshape(g, bsz, r_steps, segs, t, ch).transpose(1, 3, 2, 4, 0, 5)
         .reshape(bsz * seq, g * ch))
    hl = hl.transpose(1, 0, 2)
    return y, hl[..., :p], hl[..., p:]


def _s5_step_kernel(u_ref, h_ref, hs_ref, bt_ref, ct_ref, a1_ref, a2_ref, d_ref, y_ref, hn_ref):
    hp = lax.Precision.HIGHEST
    u = u_ref[...]
    bu = jnp.einsum('gbc,gcp->gbp', u, bt_ref[...], precision=hp, preferred_element_type=F32)
    hn = a1_ref[...] * h_ref[...] + a2_ref[...] * hs_ref[...] + bu
    hn_ref[...] = hn
    y = jnp.einsum('gbp,gpc->gbc', hn, ct_ref[...], precision=hp, preferred_element_type=F32)
    y_ref[...] = y + d_ref[...] * u


def _s5_step(u, h0_re, h0_im, params):
    a_re, a_im, log_dt, b_re, b_im, c_re, c_im, d_skip = params
    g, p = a_re.shape
    ch = S5_GROUP
    bsz = u.shape[0]
    llb, b_bar = _s5_discretize(a_re, a_im, log_dt, b_re, b_im)
    lam_bar = jnp.exp(llb)
    bt = _reim(b_bar.transpose(0, 2, 1))
    c = lax.complex(c_re.astype(F32), c_im.astype(F32))
    ct = jnp.concatenate([jnp.real(c), -jnp.imag(c)], axis=-1).transpose(0, 2, 1)
    a1 = jnp.concatenate([jnp.real(lam_bar)] * 2, axis=-1)[:, None, :]
    a2 = jnp.concatenate([-jnp.imag(lam_bar), jnp.imag(lam_bar)], axis=-1)[:, None, :]
    h = jnp.concatenate([h0_re, h0_im], axis=-1).transpose(1, 0, 2).astype(F32)
    hs = jnp.concatenate([h0_im, h0_re], axis=-1).transpose(1, 0, 2).astype(F32)
    ug = u.reshape(bsz, g, ch).transpose(1, 0, 2)
    dg = d_skip.astype(F32).reshape(g, 1, ch)
    y, hn = pl.pallas_call(
        _s5_step_kernel,
        out_shape=(jax.ShapeDtypeStruct((g, bsz, ch), F32), jax.ShapeDtypeStruct((g, bsz, 2 * p), F32)),
        compiler_params=pltpu.CompilerParams(vmem_limit_bytes=VMEM_LIMIT),
        name="s5_step",
    )(ug, h, hs, bt, ct, a1, a2, dg)
    hn = hn.transpose(1, 0, 2)
    return y.transpose(1, 0, 2).reshape(bsz, g * ch), hn[..., :p], hn[..., p:]


def _post_ab_kernel(x_ref, y_ref, a_ref, wglu_ref, bglu_ref, wo_ref, o_ref, *, s5w):
    z = jax.nn.gelu(y_ref[...])
    gl = jnp.dot(z.astype(BF16), wglu_ref[...], preferred_element_type=F32) + bglu_ref[...]
    s5o = (z * jax.nn.sigmoid(gl)).astype(BF16)
    mix = (jnp.dot(s5o, wo_ref[:s5w, :], preferred_element_type=F32)
           + jnp.dot(a_ref[...].astype(BF16), wo_ref[s5w:, :], preferred_element_type=F32))
    o_ref[...] = x_ref[...] + mix


def _post_ab(x, y, attn, wglu_bf, bglu, wo_bf, tm):
    n, d = x.shape
    s5w = y.shape[1]
    aw = attn.shape[1]
    row = lambda i: (i, 0)
    const = lambda i: (0, 0)
    return pl.pallas_call(
        functools.partial(_post_ab_kernel, s5w=s5w),
        out_shape=jax.ShapeDtypeStruct((n, d), F32),
        grid=(n // tm,),
        in_specs=[pl.BlockSpec((tm, d), row), pl.BlockSpec((tm, s5w), row), pl.BlockSpec((tm, aw), row),
                  pl.BlockSpec((s5w, s5w), const), pl.BlockSpec((1, s5w), const),
                  pl.BlockSpec((s5w + aw, d), const)],
        out_specs=pl.BlockSpec((tm, d), row),
        compiler_params=_cparams(1),
        name="post_ab",
    )(x, y, attn, wglu_bf, bglu, wo_bf)


def _ffn_kernel(x_ref, g_ref, wg_ref, wu_ref, wd_ref, gf_ref, o_ref, hn_ref, acc_ref, *, final_norm):
    c = pl.program_id(1)

    @pl.when(c == 0)
    def _():
        hn_ref[...] = _rms(x_ref[...], g_ref[...], NORM_EPS).astype(BF16)
        acc_ref[...] = jnp.zeros(acc_ref.shape, F32)

    hn = hn_ref[...]
    gate = jnp.dot(hn, wg_ref[...], preferred_element_type=F32)
    up = jnp.dot(hn, wu_ref[...], preferred_element_type=F32)
    act = (jax.nn.silu(gate) * up).astype(BF16)
    acc_ref[...] += jnp.dot(act, wd_ref[...], preferred_element_type=F32)

    @pl.when(c == pl.num_programs(1) - 1)
    def _():
        out = x_ref[...] + acc_ref[...]
        if final_norm:
            out = _rms(out, gf_ref[...], NORM_EPS)
        o_ref[...] = out


def _ffn(x, g, wg_bf, wu_bf, wd_bf, g_final, tm, th, final_norm):
    n, d = x.shape
    hidden = wg_bf.shape[1]
    return pl.pallas_call(
        functools.partial(_ffn_kernel, final_norm=final_norm),
        out_shape=jax.ShapeDtypeStruct((n, d), F32),
        grid=(n // tm, hidden // th),
        in_specs=[pl.BlockSpec((tm, d), lambda i, c: (i, 0)), pl.BlockSpec((1, d), lambda i, c: (0, 0)),
                  pl.BlockSpec((d, th), lambda i, c: (0, c)), pl.BlockSpec((d, th), lambda i, c: (0, c)),
                  pl.BlockSpec((th, d), lambda i, c: (c, 0)), pl.BlockSpec((1, d), lambda i, c: (0, 0))],
        out_specs=pl.BlockSpec((tm, d), lambda i, c: (i, 0)),
        scratch_shapes=[pltpu.VMEM((tm, d), BF16), pltpu.VMEM((tm, d), F32)],
        compiler_params=_cparams(2),
        name="ffn",
    )(x, g, wg_bf, wu_bf, wd_bf, g_final)


def _lru_gates(xc, wg_ref, ba, bx, lam_param):
    xcb = xc.astype(BF16)
    halves = [jnp.dot(xcb[:, h * MXU_DIM:(h + 1) * MXU_DIM], wg_ref[h], preferred_element_type=F32)
              for h in range(wg_ref.shape[0])]
    r_pre = jnp.concatenate([hv[:, :MXU_DIM] for hv in halves], axis=1)
    i_pre = jnp.concatenate([hv[:, MXU_DIM:] for hv in halves], axis=1)
    r = jax.nn.sigmoid(r_pre + ba)
    i = jax.nn.sigmoid(i_pre + bx)
    log_a = -LRU_C * r * jax.nn.softplus(-lam_param)
    a = jnp.exp(log_a)
    b = jnp.sqrt(-jnp.tanh(log_a) * (a * a + 1.0)) * (i * xc)
    return a, b


def _pool_project(pooled, pw_ref, scale):
    pb = pooled.astype(BF16)
    halves = [jnp.dot(pb[:, h * MXU_DIM:(h + 1) * MXU_DIM], pw_ref[h], preferred_element_type=F32)
              for h in range(pw_ref.shape[0])]
    return jnp.concatenate(halves, axis=1) * scale


def _cd_prompt_kernel(x_ref, g_ref, win_ref, cw_ref, cb_ref, wg_ref, ba_ref, bx_ref, lam_ref,
                      pw_ref, psc_ref, wo_ref,
                      o_ref, conv_ref, lru_ref, pool_ref,
                      xl_ref, xp_ref, a_ref, b_ref, hs_ref, carry_ref, *, tm, w):
    ti = pl.program_id(1)
    hist = POOL_BUF + 1

    @pl.when(ti == 0)
    def _():
        xl_ref[:hist, :] = jnp.zeros((hist, w), F32)
        xp_ref[:hist, :] = jnp.zeros((hist, w), F32)
        carry_ref[...] = jnp.zeros(carry_ref.shape, F32)

    x = x_ref[...]
    hn = _rms(x, g_ref[...], NORM_EPS).astype(BF16)
    proj = jnp.dot(hn, win_ref[...], preferred_element_type=F32)
    gate = proj[:, :w]
    xl_ref[hist:, :] = proj[:, w:2 * w]
    xp_ref[hist:, :] = proj[:, 2 * w:]

    xc = cb_ref[...] + xl_ref[pl.ds(hist - (CONV_WIDTH - 1), tm), :] * cw_ref[0:1, :]
    for j in range(1, CONV_WIDTH):
        xc = xc + xl_ref[pl.ds(hist - (CONV_WIDTH - 1) + j, tm), :] * cw_ref[j:j + 1, :]

    a, b = _lru_gates(xc, wg_ref, ba_ref[...], bx_ref[...], lam_ref[...])
    a_ref[...] = a
    b_ref[...] = b

    def group(gi, carry):
        row = pl.multiple_of(gi * SUBLANES, SUBLANES)
        ag = a_ref[pl.ds(row, SUBLANES), :]
        bg = b_ref[pl.ds(row, SUBLANES), :]
        ridx = lax.broadcasted_iota(jnp.int32, ag.shape, 0)
        for d in (1, 2, 4):
            a_sh = jnp.where(ridx >= d, pltpu.roll(ag, d, 0), 1.0)
            b_sh = jnp.where(ridx >= d, pltpu.roll(bg, d, 0), 0.0)
            bg = ag * b_sh + bg
            ag = ag * a_sh
        h = bg + ag * carry
        hs_ref[pl.ds(row, SUBLANES), :] = h
        return jnp.broadcast_to(h[SUBLANES - 1:SUBLANES, :], h.shape)

    carry = lax.fori_loop(0, tm // SUBLANES, group, jnp.broadcast_to(carry_ref[...], (SUBLANES, w)))
    carry_ref[...] = carry[0:1, :]
    lru_out = jax.nn.gelu(gate) * hs_ref[...]

    pg = w // len(POOL_WINDOWS)
    pos = ti * tm + lax.broadcasted_iota(jnp.int32, (tm, pg), 0)
    pooled = []
    for gidx, win in enumerate(POOL_WINDOWS):
        cols = slice(gidx * pg, (gidx + 1) * pg)
        cur = xp_ref[pl.ds(hist, tm), cols]
        tot = cur
        for kk in range(1, win):
            tot = tot + xp_ref[pl.ds(hist - kk, tm), cols]
        cnt = jnp.minimum(pos + 1, win).astype(F32)
        pooled.append(tot / cnt - cur)
    pool_out = _pool_project(jnp.concatenate(pooled, axis=1), pw_ref, psc_ref[...])

    mix = (jnp.dot(lru_out.astype(BF16), wo_ref[:w, :], preferred_element_type=F32)
           + jnp.dot(pool_out.astype(BF16), wo_ref[w:, :], preferred_element_type=F32))
    o_ref[...] = x + mix

    conv_ref[...] = xl_ref[pl.ds(hist + tm - (CONV_WIDTH - 1), CONV_WIDTH - 1), :]
    lru_ref[...] = carry_ref[...]
    pool_ref[...] = xp_ref[pl.ds(hist + tm - POOL_BUF, POOL_BUF), :]
    xl_ref[:hist, :] = xl_ref[pl.ds(tm, hist), :]
    xp_ref[:hist, :] = xp_ref[pl.ds(tm, hist), :]


def _cd_prompt(x, g, cdw, bsz, seq, tm):
    n, d = x.shape
    win_bf, cw, cb, wg_bf, ba, bx, lam, pw_bf, psc, wo_bf = cdw
    w = cw.shape[1]
    nt = seq // tm
    hist = POOL_BUF + 1
    row = lambda b, i: (b * nt + i, 0)
    c2 = lambda b, i: (0, 0)
    c3 = lambda b, i: (0, 0, 0)
    st = lambda b, i: (b, 0, 0)
    return pl.pallas_call(
        functools.partial(_cd_prompt_kernel, tm=tm, w=w),
        out_shape=(jax.ShapeDtypeStruct((n, d), F32),
                   jax.ShapeDtypeStruct((bsz, CONV_WIDTH - 1, w), F32),
                   jax.ShapeDtypeStruct((bsz, 1, w), F32),
                   jax.ShapeDtypeStruct((bsz, POOL_BUF, w), F32)),
        grid=(bsz, nt),
        in_specs=[pl.BlockSpec((tm, d), row), pl.BlockSpec((1, d), c2),
                  pl.BlockSpec(win_bf.shape, c2), pl.BlockSpec(cw.shape, c2), pl.BlockSpec((1, w), c2),
                  pl.BlockSpec(wg_bf.shape, c3), pl.BlockSpec((1, w), c2), pl.BlockSpec((1, w), c2),
                  pl.BlockSpec((1, w), c2), pl.BlockSpec(pw_bf.shape, c3), pl.BlockSpec((1, w), c2),
                  pl.BlockSpec(wo_bf.shape, c2)],
        out_specs=(pl.BlockSpec((tm, d), row),
                   pl.BlockSpec((None, CONV_WIDTH - 1, w), st),
                   pl.BlockSpec((None, 1, w), st),
                   pl.BlockSpec((None, POOL_BUF, w), st)),
        scratch_shapes=[pltpu.VMEM((hist + tm, w), F32), pltpu.VMEM((hist + tm, w), F32),
                        pltpu.VMEM((tm, w), F32), pltpu.VMEM((tm, w), F32), pltpu.VMEM((tm, w), F32),
                        pltpu.VMEM((1, w), F32)],
        compiler_params=_cparams(2),
        name="cd_prompt",
    )(x, g, win_bf, cw, cb, wg_bf, ba, bx, lam, pw_bf, psc, wo_bf)


def _cd_step_kernel(x_ref, g_ref, win_ref, cw_ref, cb_ref, wg_ref, ba_ref, bx_ref, lam_ref,
                    pw_ref, psc_ref, wo_ref, sc_ref, sl_ref, sp_ref,
                    o_ref, conv_ref, lru_ref, pool_ref, *, w):
    x = x_ref[...]
    hn = _rms(x, g_ref[...], NORM_EPS).astype(BF16)
    proj = jnp.dot(hn, win_ref[...], preferred_element_type=F32)
    gate = proj[:, :w]
    xl = proj[:, w:2 * w]
    xp = proj[:, 2 * w:]
    xc = cb_ref[...] + xl * cw_ref[CONV_WIDTH - 1:CONV_WIDTH, :]
    for j in range(CONV_WIDTH - 1):
        xc = xc + sc_ref[j] * cw_ref[j:j + 1, :]
    a, b = _lru_gates(xc, wg_ref, ba_ref[...], bx_ref[...], lam_ref[...])
    h = a * sl_ref[...] + b
    lru_out = jax.nn.gelu(gate) * h

    pg = w // len(POOL_WINDOWS)
    pooled = []
    for gidx, win in enumerate(POOL_WINDOWS):
        cols = slice(gidx * pg, (gidx + 1) * pg)
        cur = xp[:, cols]
        tot = cur
        for kk in range(1, win):
            tot = tot + sp_ref[POOL_BUF - kk][:, cols]
        pooled.append(tot / float(win) - cur)
    pool_out = _pool_project(jnp.concatenate(pooled, axis=1), pw_ref, psc_ref[...])

    mix = (jnp.dot(lru_out.astype(BF16), wo_ref[:w, :], preferred_element_type=F32)
           + jnp.dot(pool_out.astype(BF16), wo_ref[w:, :], preferred_element_type=F32))
    o_ref[...] = x + mix
    for j in range(CONV_WIDTH - 2):
        conv_ref[j] = sc_ref[j + 1]
    conv_ref[CONV_WIDTH - 2] = xl
    lru_ref[...] = h
    for j in range(POOL_BUF - 1):
        pool_ref[j] = sp_ref[j + 1]
    pool_ref[POOL_BUF - 1] = xp


def _cd_step(x, g, cdw, conv_buf, lru_h0, pool_buf):
    n, d = x.shape
    win_bf, cw, cb, wg_bf, ba, bx, lam, pw_bf, psc, wo_bf = cdw
    w = cw.shape[1]
    sc = conv_buf.transpose(1, 0, 2)
    sp = pool_buf.transpose(1, 0, 2)
    o, conv_n, lru_n, pool_n = pl.pallas_call(
        functools.partial(_cd_step_kernel, w=w),
        out_shape=(jax.ShapeDtypeStruct((n, d), F32),
                   jax.ShapeDtypeStruct((CONV_WIDTH - 1, n, w), F32),
                   jax.ShapeDtypeStruct((n, w), F32),
                   jax.ShapeDtypeStruct((POOL_BUF, n, w), F32)),
        compiler_params=pltpu.CompilerParams(vmem_limit_bytes=VMEM_LIMIT),
        name="cd_step",
    )(x, g, win_bf, cw, cb, wg_bf, ba, bx, lam, pw_bf, psc, wo_bf, sc, lru_h0, sp)
    return o, conv_n.transpose(1, 0, 2), lru_n, pool_n.transpose(1, 0, 2)


def _block_diag_tiles(w, tile):
    nb, c, _ = w.shape
    per = tile // c
    out = jnp.zeros((nb // per, tile, tile), w.dtype)
    for i in range(nb):
        t, o = divmod(i, per)
        out = out.at[t, o * c:(o + 1) * c, o * c:(o + 1) * c].set(w[i])
    return out


def _rope_tables(pos, rows):
    half = HEAD_DIM // 2
    inv = ROPE_THETA ** (-jnp.arange(half, dtype=F32) / half)
    ang = pos.astype(F32)[:, None] * inv[None, :]
    cos = jnp.cos(ang)
    sin = jnp.sin(ang)
    reps = LANES // HEAD_DIM
    cos_t = jnp.tile(jnp.concatenate([cos, cos], axis=1), (1, reps))
    sin_t = jnp.tile(jnp.concatenate([-sin, sin], axis=1), (1, reps))
    if cos_t.shape[0] != rows:
        cos_t = jnp.broadcast_to(cos_t, (rows, LANES))
        sin_t = jnp.broadcast_to(sin_t, (rows, LANES))
    return cos_t, sin_t


def _row_tile(n, target):
    t = min(n, target)
    while n % t:
        t //= 2
    return t


def kernel(x_prompt, x_sample, cache_k, cache_v, page_table, state_s5_re, state_s5_im, state_conv, state_lru, state_pool, norm_mix, norm_ffn, norm_final, w_in_ab, w_out_ab, s5_a_re, s5_a_im, s5_log_dt, s5_b_re, s5_b_im, s5_c_re, s5_c_im, s5_d, s5_w_glu, s5_b_glu, diff_lq1, diff_lk1, diff_lq2, diff_lk2, diff_subln, w_in_cd, w_out_cd, conv_w, conv_b, lru_wa, lru_ba, lru_wx, lru_bx, lru_lambda, pool_w, pool_scale, ffn_w_gate, ffn_w_up, ffn_w_down):
    bsz, seq, d = x_prompt.shape
    nseq, dec_seq, _ = x_sample.shape
    assert dec_seq == 1
    depth = norm_mix.shape[0]
    npages = page_table.shape[1]
    page_size = cache_k.shape[2]
    past_len = npages * page_size
    assert past_len >= POOL_BUF
    s5w = s5_d.shape[1]
    qkw = cache_k.shape[3] * cache_k.shape[4]
    vw_total = cache_v.shape[3] * cache_v.shape[4]
    heads = cache_v.shape[3]
    n_p = bsz * seq
    row2 = lambda a: a.reshape(1, -1)

    xp = x_prompt.reshape(n_p, d)
    xs = x_sample.reshape(nseq, d)
    tm_p = _row_tile(seq, 512)
    cos_p, sin_p = _rope_tables(jnp.arange(seq, dtype=jnp.int32), seq)
    cos_s, sin_s = _rope_tables(jnp.full((1,), past_len, jnp.int32), nseq)

    outs_p = {'k': [], 'v': [], 're': [], 'im': [], 'conv': [], 'lru': [], 'pool': []}
    outs_s = {'k': [], 'v': [], 're': [], 'im': [], 'conv': [], 'lru': [], 'pool': []}
    hidden = ffn_w_gate.shape[2]
    th = hidden // 2 if (hidden // 2) % LANES == 0 else hidden

    for l in range(depth):
        j = l // 2
        g_mix = row2(norm_mix[l])
        if l % 2 == 0:
            lam_init = 0.8 - 0.6 * math.exp(-0.3 * l)
            w_in = w_in_ab[j].astype(BF16)
            w_out = w_out_ab[j].astype(BF16)
            w_glu = s5_w_glu[j].astype(BF16)
            b_glu = row2(s5_b_glu[j])
            lamp = jnp.stack([diff_lq1[j], diff_lk1[j], diff_lq2[j], diff_lk2[j]]).astype(F32)
            gain = row2(diff_subln[j])
            s5p = (s5_a_re[j], s5_a_im[j], s5_log_dt[j], s5_b_re[j], s5_b_im[j],
                   s5_c_re[j], s5_c_im[j], s5_d[j])
            u, q_bf, k, v, k_bf, v_bf = _inproj_ab(xp, g_mix, w_in, cos_p, sin_p, tm_p, s5w, qkw)
            attn = _attn_prompt(q_bf, k_bf, v_bf, lamp, gain, bsz, seq, tm_p, lam_init)
            y, h_re, h_im = _s5_prompt(u, _s5_prompt_tables(s5p, seq), bsz, seq)
            xp = _post_ab(xp, y, attn, w_glu, b_glu, w_out, tm_p)
            outs_p['k'].append(k.reshape(bsz, seq, qkw // HEAD_DIM, HEAD_DIM))
            outs_p['v'].append(v.reshape(bsz, seq, heads, vw_total // heads))
            outs_p['re'].append(h_re)
            outs_p['im'].append(h_im)
            u, q_bf, k, v, _, _ = _inproj_ab(xs, g_mix, w_in, cos_s, sin_s, nseq, s5w, qkw)
            ck = cache_k[j].reshape(-1, page_size, qkw)
            cv = cache_v[j].reshape(-1, page_size, vw_total)
            attn = _attn_decode(q_bf.astype(F32), k, v, ck, cv, page_table, lamp, gain,
                                math.gcd(npages, 8), lam_init)
            y, h_re, h_im = _s5_step(u, state_s5_re[j], state_s5_im[j], s5p)
            xs = _post_ab(xs, y, attn, w_glu, b_glu, w_out, nseq)
            outs_s['k'].append(k.reshape(nseq, 1, qkw // HEAD_DIM, HEAD_DIM))
            outs_s['v'].append(v.reshape(nseq, 1, heads, vw_total // heads))
            outs_s['re'].append(h_re)
            outs_s['im'].append(h_im)
        else:
            w = conv_w.shape[2]
            wg = jnp.concatenate([_block_diag_tiles(lru_wa[j], MXU_DIM),
                                  _block_diag_tiles(lru_wx[j], MXU_DIM)], axis=2).astype(BF16)
            cdw = (w_in_cd[j].astype(BF16), conv_w[j], row2(conv_b[j]), wg, row2(lru_ba[j]),
                   row2(lru_bx[j]), row2(lru_lambda[j]),
                   _block_diag_tiles(pool_w[j], MXU_DIM).astype(BF16), row2(pool_scale[j]),
                   w_out_cd[j].astype(BF16))
            xp, c_new, h_new, p_new = _cd_prompt(xp, g_mix, cdw, bsz, seq, tm_p)
            outs_p['conv'].append(c_new)
            outs_p['lru'].append(h_new.reshape(bsz, w))
            outs_p['pool'].append(p_new)
            xs, c_new, h_new, p_new = _cd_step(xs, g_mix, cdw, state_conv[j], state_lru[j], state_pool[j])
            outs_s['conv'].append(c_new)
            outs_s['lru'].append(h_new)
            outs_s['pool'].append(p_new)
        last = l == depth - 1
        g_ffn = row2(norm_ffn[l])
        g_fin = row2(norm_final)
        wg_f = ffn_w_gate[l].astype(BF16)
        wu_f = ffn_w_up[l].astype(BF16)
        wd_f = ffn_w_down[l].astype(BF16)
        xp = _ffn(xp, g_ffn, wg_f, wu_f, wd_f, g_fin, _row_tile(n_p, 512), th, last)
        xs = _ffn(xs, g_ffn, wg_f, wu_f, wd_f, g_fin, nseq, th, last)

    sp = {name: jnp.stack(vals) for name, vals in outs_p.items()}
    ss = {name: jnp.stack(vals) for name, vals in outs_s.items()}
    y_prompt = xp.reshape(bsz, seq, d)
    y_sample = xs.reshape(nseq, 1, d)
    return (y_prompt, y_sample, sp['k'], sp['v'], ss['k'], ss['v'], sp['re'], sp['im'], ss['re'], ss['im'],
            sp['conv'], ss['conv'], sp['lru'], ss['lru'], sp['pool'], ss['pool'])
```

```python
import functools
import math

import jax
import jax.numpy as jnp
from jax import lax
from jax.experimental import pallas as pl
from jax.experimental.pallas import tpu as pltpu

F32 = jnp.float32
BF16 = jnp.bfloat16

HEAD_DIM = 64
ROPE_THETA = 10000.0
NORM_EPS = 1e-6
SUBLN_EPS = 1e-5
S5_GROUP = 16
S5_CHUNK = 16
S5_SEGS = 8
LRU_C = 8.0
CONV_WIDTH = 4
POOL_WINDOWS = (2, 4, 8, 16)
POOL_BUF = max(POOL_WINDOWS) - 1

LANES = 128
SUBLANES = 8
MXU_DIM = 256
VMEM_LIMIT = 56 * 1024 * 1024

NEG_BIG = -1e30


def _cparams(n_axes):
    return pltpu.CompilerParams(dimension_semantics=("arbitrary",) * n_axes,
                                vmem_limit_bytes=VMEM_LIMIT)


def _rms(x, g, eps):
    ms = jnp.mean(x * x, axis=-1, keepdims=True)
    return x * lax.rsqrt(ms + eps) * g


def _lane_tile(x, reps):
    return jnp.concatenate([x] * reps, axis=1)


def _rope(x, cos, sin_signed):
    width = x.shape[1]
    lane = lax.broadcasted_iota(jnp.int32, x.shape, 1)
    half = HEAD_DIM // 2
    rot = jnp.where((lane & half) == 0, pltpu.roll(x, width - half, 1), pltpu.roll(x, half, 1))
    return x * cos + rot * sin_signed


def _inproj_ab_kernel(x_ref, g_ref, w_ref, cos_ref, sin_ref,
                      u_ref, q_ref, k_ref, v_ref, kb_ref, vb_ref, *, s5w, qkw):
    h = _rms(x_ref[...], g_ref[...], NORM_EPS).astype(BF16)
    proj = jnp.dot(h, w_ref[...], preferred_element_type=F32)
    reps = qkw // LANES
    cos = _lane_tile(cos_ref[...], reps)
    sin = _lane_tile(sin_ref[...], reps)
    u_ref[...] = proj[:, :s5w]
    q = _rope(proj[:, s5w:s5w + qkw], cos, sin)
    k = _rope(proj[:, s5w + qkw:s5w + 2 * qkw], cos, sin)
    v = proj[:, s5w + 2 * qkw:]
    q_ref[...] = (q * (HEAD_DIM ** -0.5)).astype(BF16)
    k_ref[...] = k
    v_ref[...] = v
    kb_ref[...] = k.astype(BF16)
    vb_ref[...] = v.astype(BF16)


def _inproj_ab(x, g, w_bf, cos_tab, sin_tab, tm, s5w, qkw):
    n, d = x.shape
    wn = w_bf.shape[1]
    vw = wn - s5w - 2 * qkw
    period = cos_tab.shape[0] // tm
    row = lambda i: (i, 0)
    const = lambda i: (0, 0)
    tab = lambda i: (i % period, 0)
    return pl.pallas_call(
        functools.partial(_inproj_ab_kernel, s5w=s5w, qkw=qkw),
        out_shape=(jax.ShapeDtypeStruct((n, s5w), F32), jax.ShapeDtypeStruct((n, qkw), BF16),
                   jax.ShapeDtypeStruct((n, qkw), F32), jax.ShapeDtypeStruct((n, vw), F32),
                   jax.ShapeDtypeStruct((n, qkw), BF16), jax.ShapeDtypeStruct((n, vw), BF16)),
        grid=(n // tm,),
        in_specs=[pl.BlockSpec((tm, d), row), pl.BlockSpec((1, d), const),
                  pl.BlockSpec((d, wn), const), pl.BlockSpec((tm, LANES), tab),
                  pl.BlockSpec((tm, LANES), tab)],
        out_specs=(pl.BlockSpec((tm, s5w), row), pl.BlockSpec((tm, qkw), row),
                   pl.BlockSpec((tm, qkw), row), pl.BlockSpec((tm, vw), row),
                   pl.BlockSpec((tm, qkw), row), pl.BlockSpec((tm, vw), row)),
        compiler_params=_cparams(1),
        name="inproj_ab",
    )(x, g, w_bf, cos_tab, sin_tab)


def _diff_lambda(lamp, lam_init):
    s1 = jnp.sum(lamp[0:1] * lamp[1:2], axis=1, keepdims=True)
    s2 = jnp.sum(lamp[2:3] * lamp[3:4], axis=1, keepdims=True)
    return jnp.exp(s1) - jnp.exp(s2) + lam_init


def _subln(o, gain, lam_init):
    ms = jnp.mean(o * o, axis=-1, keepdims=True)
    return o * lax.rsqrt(ms + SUBLN_EPS) * gain * (1.0 - lam_init)


def _attn_prompt_kernel(lamp_ref, q_ref, k_ref, v_ref, gain_ref, o_ref,
                        vext_ref, m_ref, acc_ref, *, tq, lam_init):
    qi = pl.program_id(2)
    vw = v_ref.shape[1]

    @pl.when(qi == 0)
    def _():
        vext_ref[:, :vw] = v_ref[...]
        vext_ref[:, vw:] = jnp.ones((v_ref.shape[0], vw), BF16)

    q = q_ref[...]
    lane = lax.broadcasted_iota(jnp.int32, q.shape, 1)
    zero = jnp.zeros_like(q)
    qs = (jnp.where(lane < HEAD_DIM, q, zero), jnp.where(lane >= HEAD_DIM, q, zero))
    m_ref[...] = jnp.full(m_ref.shape, NEG_BIG, F32)
    acc_ref[...] = jnp.zeros(acc_ref.shape, F32)

    def chunk(c, masked):
        off = pl.multiple_of(c * tq, tq)
        kc = k_ref[pl.ds(off, tq), :]
        vc = vext_ref[pl.ds(off, tq), :]
        for j in range(2):
            s = lax.dot_general(qs[j], kc, (((1,), (1,)), ((), ())), preferred_element_type=F32)
            if masked:
                row = lax.broadcasted_iota(jnp.int32, s.shape, 0)
                col = lax.broadcasted_iota(jnp.int32, s.shape, 1)
                s = jnp.where(col <= row, s, -jnp.inf)
            m_prev = m_ref[j]
            m_new = jnp.maximum(m_prev, jnp.max(s, axis=1, keepdims=True))
            alpha = jnp.exp(m_prev - m_new)
            p = jnp.exp(s - _lane_tile(m_new, tq // LANES))
            pv = jnp.dot(p.astype(BF16), vc, preferred_element_type=F32)
            acc_ref[j] = _lane_tile(alpha, 2 * vw // LANES) * acc_ref[j] + pv
            m_ref[j] = m_new

    def body(c, carry):
        chunk(c, False)
        return carry

    lax.fori_loop(0, qi, body, 0)
    chunk(qi, True)

    lam = _diff_lambda(lamp_ref[...], lam_init)
    a0 = acc_ref[0]
    a1 = acc_ref[1]
    o = a0[:, :vw] / a0[:, vw:] - lam * (a1[:, :vw] / a1[:, vw:])
    o_ref[...] = _subln(o, gain_ref[...], lam_init).astype(o_ref.dtype)


def _attn_prompt(q_bf, k_bf, v_bf, lamp, gain, bsz, seq, tq, lam_init):
    n, qkw = q_bf.shape
    vw_total = v_bf.shape[1]
    heads = qkw // (2 * HEAD_DIM)
    vw = vw_total // heads
    nq = seq // tq
    return pl.pallas_call(
        functools.partial(_attn_prompt_kernel, tq=tq, lam_init=lam_init),
        out_shape=jax.ShapeDtypeStruct((n, vw_total), BF16),
        grid=(bsz, heads, nq),
        in_specs=[pl.BlockSpec((4, HEAD_DIM), lambda b, h, i: (0, 0)),
                  pl.BlockSpec((tq, 2 * HEAD_DIM), lambda b, h, i: (b * nq + i, h)),
                  pl.BlockSpec((seq, 2 * HEAD_DIM), lambda b, h, i: (b, h)),
                  pl.BlockSpec((seq, vw), lambda b, h, i: (b, h)),
                  pl.BlockSpec((1, vw), lambda b, h, i: (0, 0))],
        out_specs=pl.BlockSpec((tq, vw), lambda b, h, i: (b * nq + i, h)),
        scratch_shapes=[pltpu.VMEM((seq, 2 * vw), BF16),
                        pltpu.VMEM((2, tq, LANES), F32),
                        pltpu.VMEM((2, tq, 2 * vw), F32)],
        compiler_params=_cparams(3),
        name="attn_prompt",
    )(lamp, q_bf, k_bf, v_bf, gain)


def _attn_decode_kernel(pt_ref, lamp_ref, q_ref, kself_ref, vself_ref, gain_ref, *rest,
                        pages, lam_init, vw):
    k_refs = rest[:pages]
    v_refs = rest[pages:2 * pages]
    o_ref = rest[2 * pages]
    m_ref, l_ref, acc_ref = rest[2 * pages + 1:]
    j = pl.program_id(1)
    nh2 = m_ref.shape[0]
    width = q_ref.shape[1]

    @pl.when(j == 0)
    def _():
        m_ref[...] = jnp.full(m_ref.shape, NEG_BIG, F32)
        l_ref[...] = jnp.zeros(l_ref.shape, F32)
        acc_ref[...] = jnp.zeros(acc_ref.shape, F32)

    rowh = lax.broadcasted_iota(jnp.int32, (nh2, width), 0)
    laneh = lax.broadcasted_iota(jnp.int32, (nh2, width), 1) // HEAD_DIM
    q8 = jnp.where(rowh == laneh, jnp.broadcast_to(q_ref[...], (nh2, width)), 0.0)
    q8b = q8.astype(BF16)

    s_parts = [jnp.dot(q8b, k_refs[i][...].astype(BF16), preferred_element_type=F32)
               for i in range(pages)]
    s = jnp.concatenate(s_parts, axis=1)
    m_prev = m_ref[...]
    m_new = jnp.maximum(m_prev, jnp.max(s, axis=1, keepdims=True))
    alpha = jnp.exp(m_prev - m_new)
    p = jnp.exp(s - _lane_tile(m_new, s.shape[1] // LANES))
    l_ref[...] = alpha * l_ref[...] + jnp.sum(p, axis=1, keepdims=True)
    ps = k_refs[0].shape[1]
    nvh = width // vw
    pb = p.astype(BF16)
    pv = None
    for i in range(pages):
        cols = [jnp.dot(pb[:, i * ps:(i + 1) * ps],
                        v_refs[i][pl.ds(h, ps, stride=nvh), :].astype(BF16),
                        preferred_element_type=F32) for h in range(nvh)]
        part = jnp.concatenate(cols, axis=1)
        pv = part if pv is None else pv + part
    acc_ref[...] = _lane_tile(alpha, width // LANES) * acc_ref[...] + pv
    m_ref[...] = m_new

    @pl.when(j == pl.num_programs(1) - 1)
    def _():
        s_self = jnp.sum(q8 * kself_ref[...], axis=1, keepdims=True)
        m_old = m_ref[...]
        m_fin = jnp.maximum(m_old, s_self)
        a_fin = jnp.exp(m_old - m_fin)
        p_self = jnp.exp(s_self - m_fin)
        l_fin = a_fin * l_ref[...] + p_self
        acc = (_lane_tile(a_fin, width // LANES) * acc_ref[...]
               + _lane_tile(p_self, width // LANES) * vself_ref[...])
        o8 = acc / _lane_tile(l_fin, width // LANES)
        lam = _diff_lambda(lamp_ref[...], lam_init)
        vhead = lax.broadcasted_iota(jnp.int32, (nh2, width), 1) // vw
        sign = jnp.where((rowh & 1) == 0, 1.0, -lam)
        coef = jnp.where(vhead == rowh // 2, sign, 0.0)
        o = jnp.sum(o8 * coef, axis=0, keepdims=True)
        gain = gain_ref[...]
        outs = [_subln(o[:, h * vw:(h + 1) * vw], gain, lam_init) for h in range(width // vw)]
        o_ref[...] = jnp.concatenate(outs, axis=1)


def _attn_decode(q, k_self, v_self, cache_kt, cache_v2, page_table, lamp, gain, pages, lam_init):
    nseq, width = q.shape
    npages = page_table.shape[1]
    ps = cache_kt.shape[2]
    vw = gain.shape[1]
    nh2 = width // HEAD_DIM
    pt = page_table.reshape(-1)
    q3, k3, v3 = (a.reshape(nseq, 1, width) for a in (q, k_self, v_self))
    seq_spec = pl.BlockSpec((None, 1, width), lambda b, j, pt: (b, 0, 0))

    def page_spec(i, shape):
        return pl.BlockSpec((None,) + shape,
                            lambda b, j, pt: (pt[b * npages + j * pages + i], 0, 0))

    grid_spec = pltpu.PrefetchScalarGridSpec(
        num_scalar_prefetch=1,
        grid=(nseq, npages // pages),
        in_specs=[pl.BlockSpec((4, HEAD_DIM), lambda b, j, pt: (0, 0)), seq_spec, seq_spec, seq_spec,
                  pl.BlockSpec((1, vw), lambda b, j, pt: (0, 0))]
                 + [page_spec(i, cache_kt.shape[1:]) for i in range(pages)]
                 + [page_spec(i, cache_v2.shape[1:]) for i in range(pages)],
        out_specs=seq_spec,
        scratch_shapes=[pltpu.VMEM((nh2, LANES), F32), pltpu.VMEM((nh2, LANES), F32),
                        pltpu.VMEM((nh2, width), F32)],
    )
    out = pl.pallas_call(
        functools.partial(_attn_decode_kernel, pages=pages, lam_init=lam_init, vw=vw),
        out_shape=jax.ShapeDtypeStruct((nseq, 1, width), F32),
        grid_spec=grid_spec,
        compiler_params=_cparams(2),
        name="attn_decode",
    )(pt, lamp, q3, k3, v3, gain, *([cache_kt] * pages), *([cache_v2] * pages))
    return out.reshape(nseq, width)


def _s5_discretize(a_re, a_im, log_dt, b_re, b_im):
    lam = lax.complex(a_re.astype(F32), a_im.astype(F32))
    step = jnp.exp(log_dt.astype(F32))[:, None]
    lam_bar = jnp.exp(lam * step)
    b_bar = ((lam_bar - 1.0) / lam)[:, :, None] * lax.complex(b_re.astype(F32), b_im.astype(F32))
    return lam * step, b_bar


def _s5_power(log_lam_bar, k):
    return jnp.exp(log_lam_bar * k)


def _reim(z, swap=False):
    parts = (jnp.imag(z), jnp.real(z)) if swap else (jnp.real(z), jnp.imag(z))
    return jnp.concatenate(parts, axis=-1)


def _s5_prompt_tables(params, seq):
    a_re, a_im, log_dt, b_re, b_im, c_re, c_im, d_skip = params
    g, p = a_re.shape
    ch, t = S5_GROUP, S5_CHUNK
    llb, b_bar = _s5_discretize(a_re, a_im, log_dt, b_re, b_im)
    c = lax.complex(c_re.astype(F32), c_im.astype(F32))
    lags = jnp.arange(t + 1, dtype=F32)
    powk = _s5_power(llb[None], lags[:, None, None])
    klag = jnp.real(jnp.einsum('gcp,kgp,gpd->gkcd', c, powk[:t], b_bar))
    s_idx = jnp.arange(t)[:, None]
    t_idx = jnp.arange(t)[None, :]
    lag = t_idx - s_idx
    blocks = klag[:, jnp.clip(lag, 0, t - 1)]
    blocks = jnp.where((lag >= 0)[None, :, :, None, None], blocks, 0.0)
    eye = jnp.eye(ch, dtype=F32)
    skip = d_skip.astype(F32).reshape(g, ch)[:, None, None, :, None] * eye[None, None, None]
    blocks = blocks + jnp.where((lag == 0)[None, :, :, None, None], skip, 0.0)
    m_mat = blocks.transpose(0, 1, 4, 2, 3).reshape(g, t * ch, t * ch)
    e_c = powk[:t][::-1].transpose(1, 0, 2)[:, :, None, :] * b_bar.transpose(0, 2, 1)[:, None]
    e_c = e_c.reshape(g, t * ch, p)
    e_mat = jnp.concatenate([_reim(e_c), _reim(e_c, swap=True)], axis=-1)
    cp = c[:, None] * powk[1:t + 1].transpose(1, 0, 2)[:, :, None, :]
    cp = cp.reshape(g, t * ch, p)
    f_mat = jnp.concatenate([jnp.real(cp), -jnp.imag(cp)], axis=-1).transpose(0, 2, 1)
    nsub = seq // t
    r_steps = nsub // S5_SEGS
    ks = jnp.concatenate([jnp.arange(r_steps + 1, dtype=F32) * t,
                          jnp.array([1.0, 2.0, 4.0], F32) * (r_steps * t)])
    pw = _s5_power(llb[None], ks[:, None, None]).transpose(1, 0, 2)
    pad = (-pw.shape[1]) % SUBLANES
    pw = jnp.pad(pw, ((0, 0), (0, pad), (0, 0)))
    pa1 = jnp.concatenate([jnp.real(pw), jnp.real(pw)], axis=-1)
    pa2 = jnp.concatenate([-jnp.imag(pw), jnp.imag(pw)], axis=-1)
    return m_mat.astype(BF16), e_mat.astype(BF16), f_mat.astype(BF16), pa1, pa2


def _shift_rows(x, d):
    row = lax.broadcasted_iota(jnp.int32, x.shape, 0)
    return jnp.where(row >= d, pltpu.roll(x, d, 0), 0.0)


def _s5_prompt_kernel(u_ref, e_ref, m_ref, f_ref, pa1_ref, pa2_ref, y_ref, hl_ref,
                      ee_ref, hloc_ref, hin_ref, *, bsz, r_steps):
    sw = 2 * (e_ref.shape[1] // 4)
    u = u_ref[...]
    ee_ref[...] = jnp.dot(u, e_ref[...], preferred_element_type=F32)
    rows_b = r_steps * S5_SEGS
    a1 = pa1_ref[1:2, :]
    a2 = pa2_ref[1:2, :]
    for b in range(bsz):
        base = b * rows_b

        def step(r, carry):
            h, hs = carry
            row = pl.multiple_of(base + r * S5_SEGS, S5_SEGS)
            e_r = ee_ref[pl.ds(row, S5_SEGS), :sw]
            es_r = ee_ref[pl.ds(row, S5_SEGS), sw:]
            hn = a1 * h + a2 * hs + e_r
            hsn = a1 * hs - a2 * h + es_r
            hloc_ref[pl.ds(row, S5_SEGS), :] = hn
            return hn, hsn

        zero = jnp.zeros((S5_SEGS, sw), F32)
        c, cs = lax.fori_loop(0, r_steps, step, (zero, zero))
        for i, d in enumerate((1, 2, 4)):
            s1 = pa1_ref[r_steps + 1 + i:r_steps + 2 + i, :]
            s2 = pa2_ref[r_steps + 1 + i:r_steps + 2 + i, :]
            sh = _shift_rows(c, d)
            shs = _shift_rows(cs, d)
            c, cs = c + s1 * sh + s2 * shs, cs + s1 * shs - s2 * sh
        hl_ref[b:b + 1, :] = c[S5_SEGS - 1:S5_SEGS, :]
        cin = _shift_rows(c, 1)
        cins = _shift_rows(cs, 1)
        hin_ref[pl.ds(base, S5_SEGS), :] = cin

        def fix(r, carry):
            row = pl.multiple_of(base + r * S5_SEGS, S5_SEGS)
            prev = hloc_ref[pl.ds(row - S5_SEGS, S5_SEGS), :]
            hin = prev + pa1_ref[pl.ds(r, 1), :] * cin + pa2_ref[pl.ds(r, 1), :] * cins
            hin_ref[pl.ds(row, S5_SEGS), :] = hin
            return carry

        lax.fori_loop(1, r_steps, fix, 0)
    y_ref[...] = (jnp.dot(u, m_ref[...], preferred_element_type=F32)
                  + jnp.dot(hin_ref[...].astype(BF16), f_ref[...], preferred_element_type=F32))


def _s5_prompt(u, tables, bsz, seq):
    m_mat, e_mat, f_mat, pa1, pa2 = tables
    g = m_mat.shape[0]
    p = f_mat.shape[1] // 2
    t, ch, segs = S5_CHUNK, S5_GROUP, S5_SEGS
    r_steps = seq // (t * segs)
    rows = bsz * segs * r_steps
    ug = (u.astype(BF16).reshape(bsz, segs, r_steps, t, g, ch)
          .transpose(4, 0, 2, 1, 3, 5).reshape(g, rows, t * ch))
    npw = pa1.shape[1]
    grp = lambda i: (i, 0, 0)
    y, hl = pl.pallas_call(
        functools.partial(_s5_prompt_kernel, bsz=bsz, r_steps=r_steps),
        out_shape=(jax.ShapeDtypeStruct((g, rows, t * ch), F32),
                   jax.ShapeDtypeStruct((g, bsz, 2 * p), F32)),
        grid=(g,),
        in_specs=[pl.BlockSpec((None, rows, t * ch), grp),
                  pl.BlockSpec((None, t * ch, 4 * p), grp),
                  pl.BlockSpec((None, t * ch, t * ch), grp),
                  pl.BlockSpec((None, 2 * p, t * ch), grp),
                  pl.BlockSpec((None, npw, 2 * p), grp),
                  pl.BlockSpec((None, npw, 2 * p), grp)],
        out_specs=(pl.BlockSpec((None, rows, t * ch), grp),
                   pl.BlockSpec((None, bsz, 2 * p), grp)),
        scratch_shapes=[pltpu.VMEM((rows, 4 * p), F32), pltpu.VMEM((rows, 2 * p), F32),
                        pltpu.VMEM((rows, 2 * p), F32)],
        compiler_params=_cparams(1),
        name="s5_prompt",
    )(ug, e_mat, m_mat, f_mat, pa1, pa2)
    y = (y.reshape(g, bsz, r_steps, segs, t, ch).transpose(1, 3, 2, 4, 0, 5)
         .reshape(bsz * seq, g * ch))
    hl = hl.transpose(1, 0, 2)
    return y, hl[..., :p], hl[..., p:]


def _s5_step_kernel(u_ref, h_ref, hs_ref, bt_ref, ct_ref, a1_ref, a2_ref, d_ref, y_ref, hn_ref):
    hp = lax.Precision.HIGHEST
    u = u_ref[...]
    bu = jnp.einsum('gbc,gcp->gbp', u, bt_ref[...], precision=hp, preferred_element_type=F32)
    hn = a1_ref[...] * h_ref[...] + a2_ref[...] * hs_ref[...] + bu
    hn_ref[...] = hn
    y = jnp.einsum('gbp,gpc->gbc', hn, ct_ref[...], precision=hp, preferred_element_type=F32)
    y_ref[...] = y + d_ref[...] * u


def _s5_step(u, h0_re, h0_im, params):
    a_re, a_im, log_dt, b_re, b_im, c_re, c_im, d_skip = params
    g, p = a_re.shape
    ch = S5_GROUP
    bsz = u.shape[0]
    llb, b_bar = _s5_discretize(a_re, a_im, log_dt, b_re, b_im)
    lam_bar = jnp.exp(llb)
    bt = _reim(b_bar.transpose(0, 2, 1))
    c = lax.complex(c_re.astype(F32), c_im.astype(F32))
    ct = jnp.concatenate([jnp.real(c), -jnp.imag(c)], axis=-1).transpose(0, 2, 1)
    a1 = jnp.concatenate([jnp.real(lam_bar)] * 2, axis=-1)[:, None, :]
    a2 = jnp.concatenate([-jnp.imag(lam_bar), jnp.imag(lam_bar)], axis=-1)[:, None, :]
    h = jnp.concatenate([h0_re, h0_im], axis=-1).transpose(1, 0, 2).astype(F32)
    hs = jnp.concatenate([h0_im, h0_re], axis=-1).transpose(1, 0, 2).astype(F32)
    ug = u.reshape(bsz, g, ch).transpose(1, 0, 2)
    dg = d_skip.astype(F32).reshape(g, 1, ch)
    y, hn = pl.pallas_call(
        _s5_step_kernel,
        out_shape=(jax.ShapeDtypeStruct((g, bsz, ch), F32), jax.ShapeDtypeStruct((g, bsz, 2 * p), F32)),
        compiler_params=pltpu.CompilerParams(vmem_limit_bytes=VMEM_LIMIT),
        name="s5_step",
    )(ug, h, hs, bt, ct, a1, a2, dg)
    hn = hn.transpose(1, 0, 2)
    return y.transpose(1, 0, 2).reshape(bsz, g * ch), hn[..., :p], hn[..., p:]


def _post_ab_kernel(x_ref, y_ref, a_ref, wglu_ref, bglu_ref, wo_ref, o_ref, *, s5w):
    z = jax.nn.gelu(y_ref[...])
    gl = jnp.dot(z.astype(BF16), wglu_ref[...], preferred_element_type=F32) + bglu_ref[...]
    s5o = (z * jax.nn.sigmoid(gl)).astype(BF16)
    mix = (jnp.dot(s5o, wo_ref[:s5w, :], preferred_element_type=F32)
           + jnp.dot(a_ref[...].astype(BF16), wo_ref[s5w:, :], preferred_element_type=F32))
    o_ref[...] = x_ref[...] + mix


def _post_ab(x, y, attn, wglu_bf, bglu, wo_bf, tm):
    n, d = x.shape
    s5w = y.shape[1]
    aw = attn.shape[1]
    row = lambda i: (i, 0)
    const = lambda i: (0, 0)
    return pl.pallas_call(
        functools.partial(_post_ab_kernel, s5w=s5w),
        out_shape=jax.ShapeDtypeStruct((n, d), F32),
        grid=(n // tm,),
        in_specs=[pl.BlockSpec((tm, d), row), pl.BlockSpec((tm, s5w), row), pl.BlockSpec((tm, aw), row),
                  pl.BlockSpec((s5w, s5w), const), pl.BlockSpec((1, s5w), const),
                  pl.BlockSpec((s5w + aw, d), const)],
        out_specs=pl.BlockSpec((tm, d), row),
        compiler_params=_cparams(1),
        name="post_ab",
    )(x, y, attn, wglu_bf, bglu, wo_bf)


def _ffn_kernel(x_ref, g_ref, wg_ref, wu_ref, wd_ref, gf_ref, o_ref, hn_ref, acc_ref, *, final_norm):
    c = pl.program_id(1)

    @pl.when(c == 0)
    def _():
        hn_ref[...] = _rms(x_ref[...], g_ref[...], NORM_EPS).astype(BF16)
        acc_ref[...] = jnp.zeros(acc_ref.shape, F32)

    hn = hn_ref[...]
    gate = jnp.dot(hn, wg_ref[...], preferred_element_type=F32)
    up = jnp.dot(hn, wu_ref[...], preferred_element_type=F32)
    act = (jax.nn.silu(gate) * up).astype(BF16)
    acc_ref[...] += jnp.dot(act, wd_ref[...], preferred_element_type=F32)

    @pl.when(c == pl.num_programs(1) - 1)
    def _():
        out = x_ref[...] + acc_ref[...]
        if final_norm:
            out = _rms(out, gf_ref[...], NORM_EPS)
        o_ref[...] = out


def _ffn(x, g, wg_bf, wu_bf, wd_bf, g_final, tm, th, final_norm):
    n, d = x.shape
    hidden = wg_bf.shape[1]
    return pl.pallas_call(
        functools.partial(_ffn_kernel, final_norm=final_norm),
        out_shape=jax.ShapeDtypeStruct((n, d), F32),
        grid=(n // tm, hidden // th),
        in_specs=[pl.BlockSpec((tm, d), lambda i, c: (i, 0)), pl.BlockSpec((1, d), lambda i, c: (0, 0)),
                  pl.BlockSpec((d, th), lambda i, c: (0, c)), pl.BlockSpec((d, th), lambda i, c: (0, c)),
                  pl.BlockSpec((th, d), lambda i, c: (c, 0)), pl.BlockSpec((1, d), lambda i, c: (0, 0))],
        out_specs=pl.BlockSpec((tm, d), lambda i, c: (i, 0)),
        scratch_shapes=[pltpu.VMEM((tm, d), BF16), pltpu.VMEM((tm, d), F32)],
        compiler_params=_cparams(2),
        name="ffn",
    )(x, g, wg_bf, wu_bf, wd_bf, g_final)


def _lru_gates(xc, wg_ref, ba, bx, lam_param):
    xcb = xc.astype(BF16)
    halves = [jnp.dot(xcb[:, h * MXU_DIM:(h + 1) * MXU_DIM], wg_ref[h], preferred_element_type=F32)
              for h in range(wg_ref.shape[0])]
    r_pre = jnp.concatenate([hv[:, :MXU_DIM] for hv in halves], axis=1)
    i_pre = jnp.concatenate([hv[:, MXU_DIM:] for hv in halves], axis=1)
    r = jax.nn.sigmoid(r_pre + ba)
    i = jax.nn.sigmoid(i_pre + bx)
    log_a = -LRU_C * r * jax.nn.softplus(-lam_param)
    a = jnp.exp(log_a)
    b = jnp.sqrt(-jnp.tanh(log_a) * (a * a + 1.0)) * (i * xc)
    return a, b


def _pool_project(pooled, pw_ref, scale):
    pb = pooled.astype(BF16)
    halves = [jnp.dot(pb[:, h * MXU_DIM:(h + 1) * MXU_DIM], pw_ref[h], preferred_element_type=F32)
              for h in range(pw_ref.shape[0])]
    return jnp.concatenate(halves, axis=1) * scale


def _cd_prompt_kernel(x_ref, g_ref, win_ref, cw_ref, cb_ref, wg_ref, ba_ref, bx_ref, lam_ref,
                      pw_ref, psc_ref, wo_ref,
                      o_ref, conv_ref, lru_ref, pool_ref,
                      xl_ref, xp_ref, a_ref, b_ref, hs_ref, carry_ref, *, tm, w):
    ti = pl.program_id(1)
    hist = POOL_BUF + 1

    @pl.when(ti == 0)
    def _():
        xl_ref[:hist, :] = jnp.zeros((hist, w), F32)
        xp_ref[:hist, :] = jnp.zeros((hist, w), F32)
        carry_ref[...] = jnp.zeros(carry_ref.shape, F32)

    x = x_ref[...]
    hn = _rms(x, g_ref[...], NORM_EPS).astype(BF16)
    proj = jnp.dot(hn, win_ref[...], preferred_element_type=F32)
    gate = proj[:, :w]
    xl_ref[hist:, :] = proj[:, w:2 * w]
    xp_ref[hist:, :] = proj[:, 2 * w:]

    xc = cb_ref[...] + xl_ref[pl.ds(hist - (CONV_WIDTH - 1), tm), :] * cw_ref[0:1, :]
    for j in range(1, CONV_WIDTH):
        xc = xc + xl_ref[pl.ds(hist - (CONV_WIDTH - 1) + j, tm), :] * cw_ref[j:j + 1, :]

    a, b = _lru_gates(xc, wg_ref, ba_ref[...], bx_ref[...], lam_ref[...])
    a_ref[...] = a
    b_ref[...] = b

    def group(gi, carry):
        row = pl.multiple_of(gi * SUBLANES, SUBLANES)
        ag = a_ref[pl.ds(row, SUBLANES), :]
        bg = b_ref[pl.ds(row, SUBLANES), :]
        ridx = lax.broadcasted_iota(jnp.int32, ag.shape, 0)
        for d in (1, 2, 4):
            a_sh = jnp.where(ridx >= d, pltpu.roll(ag, d, 0), 1.0)
            b_sh = jnp.where(ridx >= d, pltpu.roll(bg, d, 0), 0.0)
            bg = ag * b_sh + bg
            ag = ag * a_sh
        h = bg + ag * carry
        hs_ref[pl.ds(row, SUBLANES), :] = h
        return jnp.broadcast_to(h[SUBLANES - 1:SUBLANES, :], h.shape)

    carry = lax.fori_loop(0, tm // SUBLANES, group, jnp.broadcast_to(carry_ref[...], (SUBLANES, w)))
    carry_ref[...] = carry[0:1, :]
    lru_out = jax.nn.gelu(gate) * hs_ref[...]

    pg = w // len(POOL_WINDOWS)
    pos = ti * tm + lax.broadcasted_iota(jnp.int32, (tm, pg), 0)
    pooled = []
    for gidx, win in enumerate(POOL_WINDOWS):
        cols = slice(gidx * pg, (gidx + 1) * pg)
        cur = xp_ref[pl.ds(hist, tm), cols]
        tot = cur
        for kk in range(1, win):
            tot = tot + xp_ref[pl.ds(hist - kk, tm), cols]
        cnt = jnp.minimum(pos + 1, win).astype(F32)
        pooled.append(tot / cnt - cur)
    pool_out = _pool_project(jnp.concatenate(pooled, axis=1), pw_ref, psc_ref[...])

    mix = (jnp.dot(lru_out.astype(BF16), wo_ref[:w, :], preferred_element_type=F32)
           + jnp.dot(pool_out.astype(BF16), wo_ref[w:, :], preferred_element_type=F32))
    o_ref[...] = x + mix

    conv_ref[...] = xl_ref[pl.ds(hist + tm - (CONV_WIDTH - 1), CONV_WIDTH - 1), :]
    lru_ref[...] = carry_ref[...]
    pool_ref[...] = xp_ref[pl.ds(hist + tm - POOL_BUF, POOL_BUF), :]
    xl_ref[:hist, :] = xl_ref[pl.ds(tm, hist), :]
    xp_ref[:hist, :] = xp_ref[pl.ds(tm, hist), :]


def _cd_prompt(x, g, cdw, bsz, seq, tm):
    n, d = x.shape
    win_bf, cw, cb, wg_bf, ba, bx, lam, pw_bf, psc, wo_bf = cdw
    w = cw.shape[1]
    nt = seq // tm
    hist = POOL_BUF + 1
    row = lambda b, i: (b * nt + i, 0)
    c2 = lambda b, i: (0, 0)
    c3 = lambda b, i: (0, 0, 0)
    st = lambda b, i: (b, 0, 0)
    return pl.pallas_call(
        functools.partial(_cd_prompt_kernel, tm=tm, w=w),
        out_shape=(jax.ShapeDtypeStruct((n, d), F32),
                   jax.ShapeDtypeStruct((bsz, CONV_WIDTH - 1, w), F32),
                   jax.ShapeDtypeStruct((bsz, 1, w), F32),
                   jax.ShapeDtypeStruct((bsz, POOL_BUF, w), F32)),
        grid=(bsz, nt),
        in_specs=[pl.BlockSpec((tm, d), row), pl.BlockSpec((1, d), c2),
                  pl.BlockSpec(win_bf.shape, c2), pl.BlockSpec(cw.shape, c2), pl.BlockSpec((1, w), c2),
                  pl.BlockSpec(wg_bf.shape, c3), pl.BlockSpec((1, w), c2), pl.BlockSpec((1, w), c2),
                  pl.BlockSpec((1, w), c2), pl.BlockSpec(pw_bf.shape, c3), pl.BlockSpec((1, w), c2),
                  pl.BlockSpec(wo_bf.shape, c2)],
        out_specs=(pl.BlockSpec((tm, d), row),
                   pl.BlockSpec((None, CONV_WIDTH - 1, w), st),
                   pl.BlockSpec((None, 1, w), st),
                   pl.BlockSpec((None, POOL_BUF, w), st)),
        scratch_shapes=[pltpu.VMEM((hist + tm, w), F32), pltpu.VMEM((hist + tm, w), F32),
                        pltpu.VMEM((tm, w), F32), pltpu.VMEM((tm, w), F32), pltpu.VMEM((tm, w), F32),
                        pltpu.VMEM((1, w), F32)],
        compiler_params=_cparams(2),
        name="cd_prompt",
    )(x, g, win_bf, cw, cb, wg_bf, ba, bx, lam, pw_bf, psc, wo_bf)


def _cd_step_kernel(x_ref, g_ref, win_ref, cw_ref, cb_ref, wg_ref, ba_ref, bx_ref, lam_ref,
                    pw_ref, psc_ref, wo_ref, sc_ref, sl_ref, sp_ref,
                    o_ref, conv_ref, lru_ref, pool_ref, *, w):
    x = x_ref[...]
    hn = _rms(x, g_ref[...], NORM_EPS).astype(BF16)
    proj = jnp.dot(hn, win_ref[...], preferred_element_type=F32)
    gate = proj[:, :w]
    xl = proj[:, w:2 * w]
    xp = proj[:, 2 * w:]
    xc = cb_ref[...] + xl * cw_ref[CONV_WIDTH - 1:CONV_WIDTH, :]
    for j in range(CONV_WIDTH - 1):
        xc = xc + sc_ref[j] * cw_ref[j:j + 1, :]
    a, b = _lru_gates(xc, wg_ref, ba_ref[...], bx_ref[...], lam_ref[...])
    h = a * sl_ref[...] + b
    lru_out = jax.nn.gelu(gate) * h

    pg = w // len(POOL_WINDOWS)
    pooled = []
    for gidx, win in enumerate(POOL_WINDOWS):
        cols = slice(gidx * pg, (gidx + 1) * pg)
        cur = xp[:, cols]
        tot = cur
        for kk in range(1, win):
            tot = tot + sp_ref[POOL_BUF - kk][:, cols]
        pooled.append(tot / float(win) - cur)
    pool_out = _pool_project(jnp.concatenate(pooled, axis=1), pw_ref, psc_ref[...])

    mix = (jnp.dot(lru_out.astype(BF16), wo_ref[:w, :], preferred_element_type=F32)
           + jnp.dot(pool_out.astype(BF16), wo_ref[w:, :], preferred_element_type=F32))
    o_ref[...] = x + mix
    for j in range(CONV_WIDTH - 2):
        conv_ref[j] = sc_ref[j + 1]
    conv_ref[CONV_WIDTH - 2] = xl
    lru_ref[...] = h
    for j in range(POOL_BUF - 1):
        pool_ref[j] = sp_ref[j + 1]
    pool_ref[POOL_BUF - 1] = xp


def _cd_step(x, g, cdw, conv_buf, lru_h0, pool_buf):
    n, d = x.shape
    win_bf, cw, cb, wg_bf, ba, bx, lam, pw_bf, psc, wo_bf = cdw
    w = cw.shape[1]
    sc = conv_buf.transpose(1, 0, 2)
    sp = pool_buf.transpose(1, 0, 2)
    o, conv_n, lru_n, pool_n = pl.pallas_call(
        functools.partial(_cd_step_kernel, w=w),
        out_shape=(jax.ShapeDtypeStruct((n, d), F32),
                   jax.ShapeDtypeStruct((CONV_WIDTH - 1, n, w), F32),
                   jax.ShapeDtypeStruct((n, w), F32),
                   jax.ShapeDtypeStruct((POOL_BUF, n, w), F32)),
        compiler_params=pltpu.CompilerParams(vmem_limit_bytes=VMEM_LIMIT),
        name="cd_step",
    )(x, g, win_bf, cw, cb, wg_bf, ba, bx, lam, pw_bf, psc, wo_bf, sc, lru_h0, sp)
    return o, conv_n.transpose(1, 0, 2), lru_n, pool_n.transpose(1, 0, 2)


def _block_diag_tiles(w, tile):
    nb, c, _ = w.shape
    per = tile // c
    out = jnp.zeros((nb // per, tile, tile), w.dtype)
    for i in range(nb):
        t, o = divmod(i, per)
        out = out.at[t, o * c:(o + 1) * c, o * c:(o + 1) * c].set(w[i])
    return out


def _rope_tables(pos, rows):
    half = HEAD_DIM // 2
    inv = ROPE_THETA ** (-jnp.arange(half, dtype=F32) / half)
    ang = pos.astype(F32)[:, None] * inv[None, :]
    cos = jnp.cos(ang)
    sin = jnp.sin(ang)
    reps = LANES // HEAD_DIM
    cos_t = jnp.tile(jnp.concatenate([cos, cos], axis=1), (1, reps))
    sin_t = jnp.tile(jnp.concatenate([-sin, sin], axis=1), (1, reps))
    if cos_t.shape[0] != rows:
        cos_t = jnp.broadcast_to(cos_t, (rows, LANES))
        sin_t = jnp.broadcast_to(sin_t, (rows, LANES))
    return cos_t, sin_t


def _row_tile(n, target):
    t = min(n, target)
    while n % t:
        t //= 2
    return t


def kernel(x_prompt, x_sample, cache_k, cache_v, page_table, state_s5_re, state_s5_im, state_conv, state_lru, state_pool, norm_mix, norm_ffn, norm_final, w_in_ab, w_out_ab, s5_a_re, s5_a_im, s5_log_dt, s5_b_re, s5_b_im, s5_c_re, s5_c_im, s5_d, s5_w_glu, s5_b_glu, diff_lq1, diff_lk1, diff_lq2, diff_lk2, diff_subln, w_in_cd, w_out_cd, conv_w, conv_b, lru_wa, lru_ba, lru_wx, lru_bx, lru_lambda, pool_w, pool_scale, ffn_w_gate, ffn_w_up, ffn_w_down):
    bsz, seq, d = x_prompt.shape
    nseq, dec_seq, _ = x_sample.shape
    assert dec_seq == 1
    depth = norm_mix.shape[0]
    npages = page_table.shape[1]
    page_size = cache_k.shape[2]
    past_len = npages * page_size
    assert past_len >= POOL_BUF
    s5w = s5_d.shape[1]
    qkw = cache_k.shape[3] * cache_k.shape[4]
    vw_total = cache_v.shape[3] * cache_v.shape[4]
    heads = cache_v.shape[3]
    n_p = bsz * seq
    row2 = lambda a: a.reshape(1, -1)

    xp = x_prompt.reshape(n_p, d)
    xs = x_sample.reshape(nseq, d)
    tm_p = _row_tile(seq, 512)
    cos_p, sin_p = _rope_tables(jnp.arange(seq, dtype=jnp.int32), seq)
    cos_s, sin_s = _rope_tables(jnp.full((1,), past_len, jnp.int32), nseq)

    outs_p = {'k': [], 'v': [], 're': [], 'im': [], 'conv': [], 'lru': [], 'pool': []}
    outs_s = {'k': [], 'v': [], 're': [], 'im': [], 'conv': [], 'lru': [], 'pool': []}
    hidden = ffn_w_gate.shape[2]
    th = hidden // 2 if (hidden // 2) % LANES == 0 else hidden

    for l in range(depth):
        j = l // 2
        g_mix = row2(norm_mix[l])
        if l % 2 == 0:
            lam_init = 0.8 - 0.6 * math.exp(-0.3 * l)
            w_in = w_in_ab[j].astype(BF16)
            w_out = w_out_ab[j].astype(BF16)
            w_glu = s5_w_glu[j].astype(BF16)
            b_glu = row2(s5_b_glu[j])
            lamp = jnp.stack([diff_lq1[j], diff_lk1[j], diff_lq2[j], diff_lk2[j]]).astype(F32)
            gain = row2(diff_subln[j])
            s5p = (s5_a_re[j], s5_a_im[j], s5_log_dt[j], s5_b_re[j], s5_b_im[j],
                   s5_c_re[j], s5_c_im[j], s5_d[j])
            u, q_bf, k, v, k_bf, v_bf = _inproj_ab(xp, g_mix, w_in, cos_p, sin_p, tm_p, s5w, qkw)
            attn = _attn_prompt(q_bf, k_bf, v_bf, lamp, gain, bsz, seq, tm_p, lam_init)
            y, h_re, h_im = _s5_prompt(u, _s5_prompt_tables(s5p, seq), bsz, seq)
            xp = _post_ab(xp, y, attn, w_glu, b_glu, w_out, tm_p)
            outs_p['k'].append(k.reshape(bsz, seq, qkw // HEAD_DIM, HEAD_DIM))
            outs_p['v'].append(v.reshape(bsz, seq, heads, vw_total // heads))
            outs_p['re'].append(h_re)
            outs_p['im'].append(h_im)
            u, q_bf, k, v, _, _ = _inproj_ab(xs, g_mix, w_in, cos_s, sin_s, nseq, s5w, qkw)
            ck = jnp.transpose(cache_k[j], (0, 2, 3, 1)).reshape(-1, qkw, page_size)
            cv = cache_v[j].reshape(-1, page_size * heads, vw_total // heads)
            attn = _attn_decode(q_bf.astype(F32), k, v, ck, cv, page_table, lamp, gain,
                                math.gcd(npages, 16), lam_init)
            y, h_re, h_im = _s5_step(u, state_s5_re[j], state_s5_im[j], s5p)
            xs = _post_ab(xs, y, attn, w_glu, b_glu, w_out, nseq)
            outs_s['k'].append(k.reshape(nseq, 1, qkw // HEAD_DIM, HEAD_DIM))
            outs_s['v'].append(v.reshape(nseq, 1, heads, vw_total // heads))
            outs_s['re'].append(h_re)
            outs_s['im'].append(h_im)
        else:
            w = conv_w.shape[2]
            wg = jnp.concatenate([_block_diag_tiles(lru_wa[j], MXU_DIM),
                                  _block_diag_tiles(lru_wx[j], MXU_DIM)], axis=2).astype(BF16)
            cdw = (w_in_cd[j].astype(BF16), conv_w[j], row2(conv_b[j]), wg, row2(lru_ba[j]),
                   row2(lru_bx[j]), row2(lru_lambda[j]),
                   _block_diag_tiles(pool_w[j], MXU_DIM).astype(BF16), row2(pool_scale[j]),
                   w_out_cd[j].astype(BF16))
            xp, c_new, h_new, p_new = _cd_prompt(xp, g_mix, cdw, bsz, seq, tm_p)
            outs_p['conv'].append(c_new)
            outs_p['lru'].append(h_new.reshape(bsz, w))
            outs_p['pool'].append(p_new)
            xs, c_new, h_new, p_new = _cd_step(xs, g_mix, cdw, state_conv[j], state_lru[j], state_pool[j])
            outs_s['conv'].append(c_new)
            outs_s['lru'].append(h_new)
            outs_s['pool'].append(p_new)
        last = l == depth - 1
        g_ffn = row2(norm_ffn[l])
        g_fin = row2(norm_final)
        wg_f = ffn_w_gate[l].astype(BF16)
        wu_f = ffn_w_up[l].astype(BF16)
        wd_f = ffn_w_down[l].astype(BF16)
        xp = _ffn(xp, g_ffn, wg_f, wu_f, wd_f, g_fin, _row_tile(n_p, 512), th, last)
        xs = _ffn(xs, g_ffn, wg_f, wu_f, wd_f, g_fin, nseq, th, last)

    sp = {name: jnp.stack(vals) for name, vals in outs_p.items()}
    ss = {name: jnp.stack(vals) for name, vals in outs_s.items()}
    y_prompt = xp.reshape(bsz, seq, d)
    y_sample = xs.reshape(nseq, 1, d)
    return (y_prompt, y_sample, sp['k'], sp['v'], ss['k'], ss['v'], sp['re'], sp['im'], ss['re'], ss['im'],
            sp['conv'], ss['conv'], sp['lru'], ss['lru'], sp['pool'], ss['pool'])
```

```python
import functools
import math

import jax
import jax.numpy as jnp
from jax import lax
from jax.experimental import pallas as pl
from jax.experimental.pallas import tpu as pltpu

F32 = jnp.float32
BF16 = jnp.bfloat16

HEAD_DIM = 64
ROPE_THETA = 10000.0
NORM_EPS = 1e-6
SUBLN_EPS = 1e-5
S5_GROUP = 16
S5_CHUNK = 16
S5_SCAN_STRIDES = (1, 2, 4)
LRU_C = 8.0
CONV_WIDTH = 4
POOL_WINDOWS = (2, 4, 8, 16)
POOL_BUF = max(POOL_WINDOWS) - 1

LANES = 128
SUBLANES = 8
MXU_DIM = 256
VMEM_LIMIT = 56 * 1024 * 1024

NEG_BIG = -1e30
LOG2E = math.log2(math.e)


def _cparams(n_axes, flags=None):
    return pltpu.CompilerParams(dimension_semantics=("arbitrary",) * n_axes,
                                vmem_limit_bytes=VMEM_LIMIT, flags=flags)


def _rms(x, g, eps):
    ms = jnp.mean(x * x, axis=-1, keepdims=True)
    return x * lax.rsqrt(ms + eps) * g


def _lane_tile(x, reps):
    return jnp.concatenate([x] * reps, axis=1)


def _rope(x, cos, sin_signed):
    width = x.shape[1]
    lane = lax.broadcasted_iota(jnp.int32, x.shape, 1)
    half = HEAD_DIM // 2
    rot = jnp.where((lane & half) == 0, pltpu.roll(x, width - half, 1), pltpu.roll(x, half, 1))
    return x * cos + rot * sin_signed


def _inproj_ab_kernel(x_ref, g_ref, w_ref, cos_ref, sin_ref,
                      u_ref, q_ref, k_ref, v_ref, kb_ref, vb_ref, *, s5w, qkw):
    h = _rms(x_ref[...], g_ref[...], NORM_EPS).astype(BF16)
    proj = jnp.dot(h, w_ref[...], preferred_element_type=F32)
    reps = qkw // LANES
    cos = _lane_tile(cos_ref[...], reps)
    sin = _lane_tile(sin_ref[...], reps)
    u_ref[...] = proj[:, :s5w]
    q = _rope(proj[:, s5w:s5w + qkw], cos, sin)
    k = _rope(proj[:, s5w + qkw:s5w + 2 * qkw], cos, sin)
    v = proj[:, s5w + 2 * qkw:]
    q_ref[...] = (q * (HEAD_DIM ** -0.5 * LOG2E)).astype(BF16)
    k_ref[...] = k
    v_ref[...] = v
    kb_ref[...] = k.astype(BF16)
    vb_ref[...] = v.astype(BF16)


def _inproj_ab(x, g, w_bf, cos_tab, sin_tab, tm, s5w, qkw):
    n, d = x.shape
    wn = w_bf.shape[1]
    vw = wn - s5w - 2 * qkw
    period = cos_tab.shape[0] // tm
    row = lambda i: (i, 0)
    const = lambda i: (0, 0)
    tab = lambda i: (i % period, 0)
    return pl.pallas_call(
        functools.partial(_inproj_ab_kernel, s5w=s5w, qkw=qkw),
        out_shape=(jax.ShapeDtypeStruct((n, s5w), F32), jax.ShapeDtypeStruct((n, qkw), BF16),
                   jax.ShapeDtypeStruct((n, qkw), F32), jax.ShapeDtypeStruct((n, vw), F32),
                   jax.ShapeDtypeStruct((n, qkw), BF16), jax.ShapeDtypeStruct((n, vw), BF16)),
        grid=(n // tm,),
        in_specs=[pl.BlockSpec((tm, d), row), pl.BlockSpec((1, d), const),
                  pl.BlockSpec((d, wn), const), pl.BlockSpec((tm, LANES), tab),
                  pl.BlockSpec((tm, LANES), tab)],
        out_specs=(pl.BlockSpec((tm, s5w), row), pl.BlockSpec((tm, qkw), row),
                   pl.BlockSpec((tm, qkw), row), pl.BlockSpec((tm, vw), row),
                   pl.BlockSpec((tm, qkw), row), pl.BlockSpec((tm, vw), row)),
        compiler_params=_cparams(1),
        name="inproj_ab",
    )(x, g, w_bf, cos_tab, sin_tab)


def _diff_lambda(lamp, lam_init):
    s1 = jnp.sum(lamp[0:1] * lamp[1:2], axis=1, keepdims=True)
    s2 = jnp.sum(lamp[2:3] * lamp[3:4], axis=1, keepdims=True)
    return jnp.exp(s1) - jnp.exp(s2) + lam_init


def _subln(o, gain, lam_init):
    ms = jnp.mean(o * o, axis=-1, keepdims=True)
    return o * lax.rsqrt(ms + SUBLN_EPS) * gain * (1.0 - lam_init)


ONES_ROWS = 16

def _attn_prompt_kernel(lamp_ref, q_ref, k_ref, v_ref, gain_ref, o_ref,
                        vt_ref, m_ref, acc_ref, *, tq, tk, lam_init):
    qi = pl.program_id(2)
    seq, vw = v_ref.shape
    per_q = tq // tk

    @pl.when(qi == 0)
    def _():
        for c in range(seq // tk):
            blk = v_ref[c * tk:(c + 1) * tk, :].astype(F32)
            vt_ref[c, :vw, :] = blk.T.astype(BF16)
            vt_ref[c, vw:, :] = jnp.ones((ONES_ROWS, tk), BF16)

    q = q_ref[...]
    lane = lax.broadcasted_iota(jnp.int32, q.shape, 1)
    zero = jnp.zeros_like(q)
    qs = (jnp.where(lane < HEAD_DIM, q, zero), jnp.where(lane >= HEAD_DIM, q, zero))
    m_ref[...] = jnp.full(m_ref.shape, NEG_BIG, F32)
    acc_ref[...] = jnp.zeros(acc_ref.shape, F32)

    def chunk(c, diag):
        off = pl.multiple_of(c * tk, tk)
        kc = k_ref[pl.ds(off, tk), :]
        vc = vt_ref[c]
        sts = [lax.dot_general(kc, qs[j], (((1,), (1,)), ((), ())), preferred_element_type=F32)
               for j in range(2)]
        if diag is not None:
            key = lax.broadcasted_iota(jnp.int32, sts[0].shape, 0) + diag * tk
            qry = lax.broadcasted_iota(jnp.int32, sts[0].shape, 1)
            sts = [jnp.where(key <= qry, st, -jnp.inf) for st in sts]

        ps = []
        alphas = []
        for j in range(2):
            m_prev = m_ref[j]
            m_new = jnp.maximum(m_prev, jnp.max(sts[j], axis=0, keepdims=True))
            alphas.append(jnp.exp2(m_prev - m_new))
            ps.append(jnp.exp2(sts[j] - m_new).astype(BF16))
            m_ref[j] = m_new
        for j in range(2):
            pv = jnp.dot(vc, ps[j], preferred_element_type=F32)
            acc_ref[j] = alphas[j] * acc_ref[j] + pv

    def body(c, carry):
        chunk(c, None)
        return carry

    lax.fori_loop(0, qi * per_q, body, 0)
    for d in range(per_q):
        chunk(qi * per_q + d, d)

    lam = _diff_lambda(lamp_ref[...], lam_init)
    a0 = acc_ref[0]
    a1 = acc_ref[1]
    ot = a0[:vw] / a0[vw:vw + 1] - lam * (a1[:vw] / a1[vw:vw + 1])
    o_ref[...] = _subln(ot.T, gain_ref[...], lam_init).astype(o_ref.dtype)


def _attn_prompt(q_bf, k_bf, v_bf, lamp, gain, bsz, seq, tq, tk, lam_init):
    n, qkw = q_bf.shape
    vw_total = v_bf.shape[1]
    heads = qkw // (2 * HEAD_DIM)
    vw = vw_total // heads
    nq = seq // tq
    return pl.pallas_call(
        functools.partial(_attn_prompt_kernel, tq=tq, tk=tk, lam_init=lam_init),
        out_shape=jax.ShapeDtypeStruct((n, vw_total), BF16),
        grid=(bsz, heads, nq),
        in_specs=[pl.BlockSpec((4, HEAD_DIM), lambda b, h, i: (0, 0)),
                  pl.BlockSpec((tq, 2 * HEAD_DIM), lambda b, h, i: (b * nq + i, h)),
                  pl.BlockSpec((seq, 2 * HEAD_DIM), lambda b, h, i: (b, h)),
                  pl.BlockSpec((seq, vw), lambda b, h, i: (b, h)),
                  pl.BlockSpec((1, vw), lambda b, h, i: (0, 0))],
        out_specs=pl.BlockSpec((tq, vw), lambda b, h, i: (b * nq + i, h)),
        scratch_shapes=[pltpu.VMEM((seq // tk, vw + ONES_ROWS, tk), BF16),
                        pltpu.VMEM((2, 1, tq), F32),
                        pltpu.VMEM((2, vw + ONES_ROWS, tq), F32)],
        compiler_params=_cparams(3),
        name="attn_prompt",
    )(lamp, q_bf, k_bf, v_bf, gain)


def _attn_decode_kernel(pt_ref, lamp_ref, q_ref, kself_ref, vself_ref, gain_ref, *rest,
                        pages, lam_init, vw):
    k_refs = rest[:pages]
    v_refs = rest[pages:2 * pages]
    o_ref = rest[2 * pages]
    m_ref, l_ref, acc_ref = rest[2 * pages + 1:]
    j = pl.program_id(1)
    nh2 = m_ref.shape[0]
    width = q_ref.shape[1]

    @pl.when(j == 0)
    def _():
        m_ref[...] = jnp.full(m_ref.shape, NEG_BIG, F32)
        l_ref[...] = jnp.zeros(l_ref.shape, F32)
        acc_ref[...] = jnp.zeros(acc_ref.shape, F32)

    rowh = lax.broadcasted_iota(jnp.int32, (nh2, width), 0)
    laneh = lax.broadcasted_iota(jnp.int32, (nh2, width), 1) // HEAD_DIM
    q8 = jnp.where(rowh == laneh, jnp.broadcast_to(q_ref[...], (nh2, width)), 0.0)
    q8b = q8.astype(BF16)

    s_parts = [jnp.dot(q8b, k_refs[i][...].astype(BF16), preferred_element_type=F32)
               for i in range(pages)]
    s = jnp.concatenate(s_parts, axis=1)
    m_prev = m_ref[...]
    m_new = jnp.maximum(m_prev, jnp.max(s, axis=1, keepdims=True))
    alpha = jnp.exp2(m_prev - m_new)
    p = jnp.exp2(s - _lane_tile(m_new, s.shape[1] // LANES))
    l_ref[...] = alpha * l_ref[...] + jnp.sum(p, axis=1, keepdims=True)
    ps = k_refs[0].shape[1]
    nvh = width // vw
    pb = p.astype(BF16)
    pv = None
    for i in range(pages):
        cols = [jnp.dot(pb[:, i * ps:(i + 1) * ps],
                        v_refs[i][pl.ds(h, ps, stride=nvh), :].astype(BF16),
                        preferred_element_type=F32) for h in range(nvh)]
        part = jnp.concatenate(cols, axis=1)
        pv = part if pv is None else pv + part
    acc_ref[...] = _lane_tile(alpha, width // LANES) * acc_ref[...] + pv
    m_ref[...] = m_new

    @pl.when(j == pl.num_programs(1) - 1)
    def _():
        s_self = jnp.sum(q8 * kself_ref[...], axis=1, keepdims=True)
        m_old = m_ref[...]
        m_fin = jnp.maximum(m_old, s_self)
        a_fin = jnp.exp2(m_old - m_fin)
        p_self = jnp.exp2(s_self - m_fin)
        l_fin = a_fin * l_ref[...] + p_self
        acc = (_lane_tile(a_fin, width // LANES) * acc_ref[...]
               + _lane_tile(p_self, width // LANES) * vself_ref[...])
        o8 = acc / _lane_tile(l_fin, width // LANES)
        lam = _diff_lambda(lamp_ref[...], lam_init)
        vhead = lax.broadcasted_iota(jnp.int32, (nh2, width), 1) // vw
        sign = jnp.where((rowh & 1) == 0, 1.0, -lam)
        coef = jnp.where(vhead == rowh // 2, sign, 0.0)
        o = jnp.sum(o8 * coef, axis=0, keepdims=True)
        gain = gain_ref[...]
        outs = [_subln(o[:, h * vw:(h + 1) * vw], gain, lam_init) for h in range(width // vw)]
        o_ref[...] = jnp.concatenate(outs, axis=1)


def _attn_decode(q, k_self, v_self, cache_kt, cache_v2, page_table, lamp, gain, pages, lam_init):
    nseq, width = q.shape
    npages = page_table.shape[1]
    ps = cache_kt.shape[2]
    vw = gain.shape[1]
    nh2 = width // HEAD_DIM
    pt = page_table.reshape(-1)
    q3, k3, v3 = (a.reshape(nseq, 1, width) for a in (q, k_self, v_self))
    seq_spec = pl.BlockSpec((None, 1, width), lambda b, j, pt: (b, 0, 0))

    def page_spec(i, shape):
        return pl.BlockSpec((None,) + shape,
                            lambda b, j, pt: (pt[b * npages + j * pages + i], 0, 0))

    grid_spec = pltpu.PrefetchScalarGridSpec(
        num_scalar_prefetch=1,
        grid=(nseq, npages // pages),
        in_specs=[pl.BlockSpec((4, HEAD_DIM), lambda b, j, pt: (0, 0)), seq_spec, seq_spec, seq_spec,
                  pl.BlockSpec((1, vw), lambda b, j, pt: (0, 0))]
                 + [page_spec(i, cache_kt.shape[1:]) for i in range(pages)]
                 + [page_spec(i, cache_v2.shape[1:]) for i in range(pages)],
        out_specs=seq_spec,
        scratch_shapes=[pltpu.VMEM((nh2, LANES), F32), pltpu.VMEM((nh2, LANES), F32),
                        pltpu.VMEM((nh2, width), F32)],
    )
    out = pl.pallas_call(
        functools.partial(_attn_decode_kernel, pages=pages, lam_init=lam_init, vw=vw),
        out_shape=jax.ShapeDtypeStruct((nseq, 1, width), F32),
        grid_spec=grid_spec,
        compiler_params=_cparams(2),
        name="attn_decode",
    )(pt, lamp, q3, k3, v3, gain, *([cache_kt] * pages), *([cache_v2] * pages))
    return out.reshape(nseq, width)


def _s5_discretize(a_re, a_im, log_dt, b_re, b_im):
    lam = lax.complex(a_re.astype(F32), a_im.astype(F32))
    step = jnp.exp(log_dt.astype(F32))[:, None]
    lam_bar = jnp.exp(lam * step)
    b_bar = ((lam_bar - 1.0) / lam)[:, :, None] * lax.complex(b_re.astype(F32), b_im.astype(F32))
    return lam * step, b_bar


def _s5_power(log_lam_bar, k):
    return jnp.exp(log_lam_bar * k)


def _reim(z, swap=False):
    parts = (jnp.imag(z), jnp.real(z)) if swap else (jnp.real(z), jnp.imag(z))
    return jnp.concatenate(parts, axis=-1)


def _s5_prompt_tables(params, seq):
    a_re, a_im, log_dt, b_re, b_im, c_re, c_im, d_skip = params
    g, p = a_re.shape
    ch, t = S5_GROUP, S5_CHUNK
    llb, b_bar = _s5_discretize(a_re, a_im, log_dt, b_re, b_im)
    c = lax.complex(c_re.astype(F32), c_im.astype(F32))
    lags = jnp.arange(t + 1, dtype=F32)
    powk = _s5_power(llb[None], lags[:, None, None])
    klag = jnp.real(jnp.einsum('gcp,kgp,gpd->gkcd', c, powk[:t], b_bar))
    s_idx = jnp.arange(t)[:, None]
    t_idx = jnp.arange(t)[None, :]
    lag = t_idx - s_idx
    blocks = klag[:, jnp.clip(lag, 0, t - 1)]
    blocks = jnp.where((lag >= 0)[None, :, :, None, None], blocks, 0.0)
    eye = jnp.eye(ch, dtype=F32)
    skip = d_skip.astype(F32).reshape(g, ch)[:, None, None, :, None] * eye[None, None, None]
    blocks = blocks + jnp.where((lag == 0)[None, :, :, None, None], skip, 0.0)
    m_mat = blocks.transpose(0, 1, 4, 2, 3).reshape(g, t * ch, t * ch)
    e_c = powk[:t][::-1].transpose(1, 0, 2)[:, :, None, :] * b_bar.transpose(0, 2, 1)[:, None]
    e_c = e_c.reshape(g, t * ch, p)
    e_mat = jnp.concatenate([_reim(e_c), _reim(e_c, swap=True)], axis=-1)
    cp = c[:, None] * powk[1:t + 1].transpose(1, 0, 2)[:, :, None, :]
    cp = cp.reshape(g, t * ch, p)
    f_mat = jnp.concatenate([jnp.real(cp), -jnp.imag(cp)], axis=-1).transpose(0, 2, 1)
    ks = jnp.arange(1, SUBLANES + 1, dtype=F32) * t
    pw = _s5_power(llb[None], ks[:, None, None]).transpose(1, 0, 2)
    a1 = jnp.concatenate([jnp.real(pw), jnp.real(pw)], axis=-1)
    a2 = jnp.concatenate([-jnp.imag(pw), jnp.imag(pw)], axis=-1)
    rows = jnp.arange(SUBLANES)[None, :, None]
    parts = []
    for d in S5_SCAN_STRIDES:
        parts.append(jnp.where(rows >= d, a1[:, d - 1:d, :], 0.0))
        parts.append(jnp.where(rows >= d, a2[:, d - 1:d, :], 0.0))
    tab = jnp.concatenate(parts + [a1, a2], axis=1)
    return m_mat.astype(BF16), e_mat.astype(BF16), f_mat.astype(BF16), tab


def _s5_lane_permutation():
    n = SUBLANES * SUBLANES * S5_GROUP
    idx = jnp.arange(n).reshape(SUBLANES, SUBLANES, S5_GROUP).transpose(1, 0, 2).reshape(n)
    return jax.nn.one_hot(idx, n, dtype=BF16)


def _s5_prompt_kernel(u_ref, perm_ref, e_ref, m_ref, f_ref, tab_ref, y_ref, hl_ref,
                      us_ref, ee_ref, h_ref, ycat_ref):
    t = S5_CHUNK
    ng = SUBLANES
    nsub = u_ref.shape[0] // t
    sw = tab_ref.shape[2]
    half_t = t // 2
    perm = perm_ref[...]
    for half in range(2):
        xs = [u_ref[pl.ds(half * half_t + s, nsub, stride=t), :].astype(BF16) for s in range(half_t)]
        uh = jnp.dot(jnp.concatenate(xs, axis=1), perm, preferred_element_type=F32).astype(BF16)
        for gl in range(ng):
            us_ref[gl, :, half * LANES:(half + 1) * LANES] = uh[:, gl * LANES:(gl + 1) * LANES]
    for gl in range(ng):
        ee_ref[gl] = jnp.dot(us_ref[gl], e_ref[gl], preferred_element_type=F32)
        h_ref[gl, :SUBLANES, :] = jnp.zeros((SUBLANES, sw), F32)

    def step(i, carry):
        row = pl.multiple_of(i * SUBLANES, SUBLANES)
        out = []
        for gl in range(ng):
            h = ee_ref[gl, pl.ds(row, SUBLANES), :sw]
            hs = ee_ref[gl, pl.ds(row, SUBLANES), sw:]
            for di, d in enumerate(S5_SCAN_STRIDES):
                t1 = tab_ref[gl, 2 * di * SUBLANES:(2 * di + 1) * SUBLANES, :]
                t2 = tab_ref[gl, (2 * di + 1) * SUBLANES:(2 * di + 2) * SUBLANES, :]
                sh = pltpu.roll(h, d, 0)
                shs = pltpu.roll(hs, d, 0)
                h, hs = h + t1 * sh + t2 * shs, hs + t1 * shs - t2 * sh
            nd = 2 * len(S5_SCAN_STRIDES)
            q1 = tab_ref[gl, nd * SUBLANES:(nd + 1) * SUBLANES, :]
            q2 = tab_ref[gl, (nd + 1) * SUBLANES:(nd + 2) * SUBLANES, :]
            cb, cbs = carry[2 * gl], carry[2 * gl + 1]
            hf = h + q1 * cb + q2 * cbs
            hfs = hs + q1 * cbs - q2 * cb
            h_ref[gl, pl.ds(SUBLANES + row, SUBLANES), :] = hf
            out.append(jnp.broadcast_to(hf[SUBLANES - 1:SUBLANES, :], hf.shape))
            out.append(jnp.broadcast_to(hfs[SUBLANES - 1:SUBLANES, :], hfs.shape))
        return tuple(out)

    zero = jnp.zeros((SUBLANES, sw), F32)
    carry = lax.fori_loop(0, nsub // SUBLANES, step, (zero,) * (2 * ng))
    for gl in range(ng):
        hl_ref[gl:gl + 1, :] = carry[2 * gl][0:1, :]
        hin = h_ref[gl, pl.ds(SUBLANES - 1, nsub), :].astype(BF16)
        y = (jnp.dot(us_ref[gl], m_ref[gl], preferred_element_type=F32)
             + jnp.dot(hin, f_ref[gl], preferred_element_type=F32))
        for half in range(2):
            ycat_ref[half, :, gl * LANES:(gl + 1) * LANES] = y[:, half * LANES:(half + 1) * LANES]
    for half in range(2):
        yc = ycat_ref[half]
        hi = yc.astype(BF16)
        lo = (yc - hi.astype(F32)).astype(BF16)
        z = (jnp.dot(hi, perm, preferred_element_type=F32)
             + jnp.dot(lo, perm, preferred_element_type=F32))
        for s in range(half_t):
            y_ref[pl.ds(half * half_t + s, nsub, stride=t), :] = z[:, s * LANES:(s + 1) * LANES]


def _s5_prompt(u, tables, bsz, seq):
    m_mat, e_mat, f_mat, tab = tables
    g = m_mat.shape[0]
    p = f_mat.shape[1] // 2
    t, ch = S5_CHUNK, S5_GROUP
    ng = LANES // ch
    assert ng == SUBLANES and t * ch == 2 * LANES
    n, w = u.shape
    nsub = seq // t
    perm = _s5_lane_permutation()
    npm = perm.shape[0]
    ntab = tab.shape[1]
    slab = lambda s, b: (s, 0, 0)
    y, hl = pl.pallas_call(
        _s5_prompt_kernel,
        out_shape=(jax.ShapeDtypeStruct((n, w), F32), jax.ShapeDtypeStruct((bsz, g, 2 * p), F32)),
        grid=(w // LANES, bsz),
        in_specs=[pl.BlockSpec((seq, LANES), lambda s, b: (b, s)),
                  pl.BlockSpec((npm, npm), lambda s, b: (0, 0)),
                  pl.BlockSpec((ng, t * ch, 4 * p), slab),
                  pl.BlockSpec((ng, t * ch, t * ch), slab),
                  pl.BlockSpec((ng, 2 * p, t * ch), slab),
                  pl.BlockSpec((ng, ntab, 2 * p), slab)],
        out_specs=(pl.BlockSpec((seq, LANES), lambda s, b: (b, s)),
                   pl.BlockSpec((None, ng, 2 * p), lambda s, b: (b, s, 0))),
        scratch_shapes=[pltpu.VMEM((ng, nsub, t * ch), BF16),
                        pltpu.VMEM((ng, nsub, 4 * p), F32),
                        pltpu.VMEM((ng, SUBLANES + nsub, 2 * p), F32),
                        pltpu.VMEM((2, nsub, ng * LANES), F32)],
        compiler_params=_cparams(2),
        name="s5_prompt",
    )(u, perm, e_mat, m_mat, f_mat, tab)
    return y, hl[..., :p], hl[..., p:]


def _s5_step_kernel(u_ref, h_ref, hs_ref, bt_ref, ct_ref, a1_ref, a2_ref, d_ref, y_ref, hn_ref):
    hp = lax.Precision.HIGHEST
    u = u_ref[...]
    bu = jnp.einsum('gbc,gcp->gbp', u, bt_ref[...], precision=hp, preferred_element_type=F32)
    hn = a1_ref[...] * h_ref[...] + a2_ref[...] * hs_ref[...] + bu
    hn_ref[...] = hn
    y = jnp.einsum('gbp,gpc->gbc', hn, ct_ref[...], precision=hp, preferred_element_type=F32)
    y_ref[...] = y + d_ref[...] * u


def _s5_step(u, h0_re, h0_im, params):
    a_re, a_im, log_dt, b_re, b_im, c_re, c_im, d_skip = params
    g, p = a_re.shape
    ch = S5_GROUP
    bsz = u.shape[0]
    llb, b_bar = _s5_discretize(a_re, a_im, log_dt, b_re, b_im)
    lam_bar = jnp.exp(llb)
    bt = _reim(b_bar.transpose(0, 2, 1))
    c = lax.complex(c_re.astype(F32), c_im.astype(F32))
    ct = jnp.concatenate([jnp.real(c), -jnp.imag(c)], axis=-1).transpose(0, 2, 1)
    a1 = jnp.concatenate([jnp.real(lam_bar)] * 2, axis=-1)[:, None, :]
    a2 = jnp.concatenate([-jnp.imag(lam_bar), jnp.imag(lam_bar)], axis=-1)[:, None, :]
    h = jnp.concatenate([h0_re, h0_im], axis=-1).transpose(1, 0, 2).astype(F32)
    hs = jnp.concatenate([h0_im, h0_re], axis=-1).transpose(1, 0, 2).astype(F32)
    ug = u.reshape(bsz, g, ch).transpose(1, 0, 2)
    dg = d_skip.astype(F32).reshape(g, 1, ch)
    y, hn = pl.pallas_call(
        _s5_step_kernel,
        out_shape=(jax.ShapeDtypeStruct((g, bsz, ch), F32), jax.ShapeDtypeStruct((g, bsz, 2 * p), F32)),
        compiler_params=pltpu.CompilerParams(vmem_limit_bytes=VMEM_LIMIT),
        name="s5_step",
    )(ug, h, hs, bt, ct, a1, a2, dg)
    hn = hn.transpose(1, 0, 2)
    return y.transpose(1, 0, 2).reshape(bsz, g * ch), hn[..., :p], hn[..., p:]


def _post_ab_kernel(x_ref, y_ref, a_ref, wglu_ref, bglu_ref, wo_ref, o_ref, *, s5w):
    z = jax.nn.gelu(y_ref[...])
    gl = jnp.dot(z.astype(BF16), wglu_ref[...], preferred_element_type=F32) + bglu_ref[...]
    s5o = (z * jax.nn.sigmoid(gl)).astype(BF16)
    mix = (jnp.dot(s5o, wo_ref[:s5w, :], preferred_element_type=F32)
           + jnp.dot(a_ref[...].astype(BF16), wo_ref[s5w:, :], preferred_element_type=F32))
    o_ref[...] = x_ref[...] + mix


def _post_ab(x, y, attn, wglu_bf, bglu, wo_bf, tm):
    n, d = x.shape
    s5w = y.shape[1]
    aw = attn.shape[1]
    row = lambda i: (i, 0)
    const = lambda i: (0, 0)
    return pl.pallas_call(
        functools.partial(_post_ab_kernel, s5w=s5w),
        out_shape=jax.ShapeDtypeStruct((n, d), F32),
        grid=(n // tm,),
        in_specs=[pl.BlockSpec((tm, d), row), pl.BlockSpec((tm, s5w), row), pl.BlockSpec((tm, aw), row),
                  pl.BlockSpec((s5w, s5w), const), pl.BlockSpec((1, s5w), const),
                  pl.BlockSpec((s5w + aw, d), const)],
        out_specs=pl.BlockSpec((tm, d), row),
        compiler_params=_cparams(1),
        name="post_ab",
    )(x, y, attn, wglu_bf, bglu, wo_bf)


def _ffn_kernel(x_ref, g_ref, wg_ref, wu_ref, wd_ref, gf_ref, o_ref, hn_ref, acc_ref, *, final_norm):
    c = pl.program_id(1)

    @pl.when(c == 0)
    def _():
        hn_ref[...] = _rms(x_ref[...], g_ref[...], NORM_EPS).astype(BF16)
        acc_ref[...] = jnp.zeros(acc_ref.shape, F32)

    hn = hn_ref[...]
    gate = jnp.dot(hn, wg_ref[...], preferred_element_type=F32)
    up = jnp.dot(hn, wu_ref[...], preferred_element_type=F32)
    act = (jax.nn.silu(gate) * up).astype(BF16)
    acc_ref[...] += jnp.dot(act, wd_ref[...], preferred_element_type=F32)

    @pl.when(c == pl.num_programs(1) - 1)
    def _():
        out = x_ref[...] + acc_ref[...]
        if final_norm:
            out = _rms(out, gf_ref[...], NORM_EPS)
        o_ref[...] = out


def _ffn(x, g, wg_bf, wu_bf, wd_bf, g_final, tm, th, final_norm):
    n, d = x.shape
    hidden = wg_bf.shape[1]
    return pl.pallas_call(
        functools.partial(_ffn_kernel, final_norm=final_norm),
        out_shape=jax.ShapeDtypeStruct((n, d), F32),
        grid=(n // tm, hidden // th),
        in_specs=[pl.BlockSpec((tm, d), lambda i, c: (i, 0)), pl.BlockSpec((1, d), lambda i, c: (0, 0)),
                  pl.BlockSpec((d, th), lambda i, c: (0, c)), pl.BlockSpec((d, th), lambda i, c: (0, c)),
                  pl.BlockSpec((th, d), lambda i, c: (c, 0)), pl.BlockSpec((1, d), lambda i, c: (0, 0))],
        out_specs=pl.BlockSpec((tm, d), lambda i, c: (i, 0)),
        scratch_shapes=[pltpu.VMEM((tm, d), BF16), pltpu.VMEM((tm, d), F32)],
        compiler_params=_cparams(2),
        name="ffn",
    )(x, g, wg_bf, wu_bf, wd_bf, g_final)


def _lru_gates(xc, wg_ref, ba, bx, lam_param):
    xcb = xc.astype(BF16)
    halves = [jnp.dot(xcb[:, h * MXU_DIM:(h + 1) * MXU_DIM], wg_ref[h], preferred_element_type=F32)
              for h in range(wg_ref.shape[0])]
    r_pre = jnp.concatenate([hv[:, :MXU_DIM] for hv in halves], axis=1)
    i_pre = jnp.concatenate([hv[:, MXU_DIM:] for hv in halves], axis=1)
    r = jax.nn.sigmoid(r_pre + ba)
    i = jax.nn.sigmoid(i_pre + bx)
    log_a = -LRU_C * r * jax.nn.softplus(-lam_param)
    a = jnp.exp(log_a)
    b = jnp.sqrt(-jnp.tanh(log_a) * (a * a + 1.0)) * (i * xc)
    return a, b


def _pool_project(pooled, pw_ref, scale):
    pb = pooled.astype(BF16)
    halves = [jnp.dot(pb[:, h * MXU_DIM:(h + 1) * MXU_DIM], pw_ref[h], preferred_element_type=F32)
              for h in range(pw_ref.shape[0])]
    return jnp.concatenate(halves, axis=1) * scale


def _cd_prompt_kernel(x_ref, g_ref, win_ref, cw_ref, cb_ref, wg_ref, ba_ref, bx_ref, lam_ref,
                      pw_ref, psc_ref, wo_ref,
                      o_ref, conv_ref, lru_ref, pool_ref,
                      xl_ref, xp_ref, a_ref, b_ref, hs_ref, carry_ref, *, tm, w):
    ti = pl.program_id(1)
    hist = POOL_BUF + 1

    @pl.when(ti == 0)
    def _():
        xl_ref[:hist, :] = jnp.zeros((hist, w), F32)
        xp_ref[:hist, :] = jnp.zeros((hist, w), F32)
        carry_ref[...] = jnp.zeros(carry_ref.shape, F32)

    x = x_ref[...]
    hn = _rms(x, g_ref[...], NORM_EPS).astype(BF16)
    proj = jnp.dot(hn, win_ref[...], preferred_element_type=F32)
    gate = proj[:, :w]
    xl_ref[hist:, :] = proj[:, w:2 * w]
    xp_ref[hist:, :] = proj[:, 2 * w:]

    xc = cb_ref[...] + xl_ref[pl.ds(hist - (CONV_WIDTH - 1), tm), :] * cw_ref[0:1, :]
    for j in range(1, CONV_WIDTH):
        xc = xc + xl_ref[pl.ds(hist - (CONV_WIDTH - 1) + j, tm), :] * cw_ref[j:j + 1, :]

    a, b = _lru_gates(xc, wg_ref, ba_ref[...], bx_ref[...], lam_ref[...])
    a_ref[...] = a
    b_ref[...] = b

    def group(gi, carry):
        row = pl.multiple_of(gi * SUBLANES, SUBLANES)
        ag = a_ref[pl.ds(row, SUBLANES), :]
        bg = b_ref[pl.ds(row, SUBLANES), :]
        ridx = lax.broadcasted_iota(jnp.int32, ag.shape, 0)
        for d in (1, 2, 4):
            a_sh = jnp.where(ridx >= d, pltpu.roll(ag, d, 0), 1.0)
            b_sh = jnp.where(ridx >= d, pltpu.roll(bg, d, 0), 0.0)
            bg = ag * b_sh + bg
            ag = ag * a_sh
        h = bg + ag * carry
        hs_ref[pl.ds(row, SUBLANES), :] = h
        return jnp.broadcast_to(h[SUBLANES - 1:SUBLANES, :], h.shape)

    carry = lax.fori_loop(0, tm // SUBLANES, group, jnp.broadcast_to(carry_ref[...], (SUBLANES, w)))
    carry_ref[...] = carry[0:1, :]
    lru_out = jax.nn.gelu(gate) * hs_ref[...]

    pg = w // len(POOL_WINDOWS)
    pos = ti * tm + lax.broadcasted_iota(jnp.int32, (tm, pg), 0)
    pooled = []
    for gidx, win in enumerate(POOL_WINDOWS):
        cols = slice(gidx * pg, (gidx + 1) * pg)
        cur = xp_ref[pl.ds(hist, tm), cols]
        tot = cur
        for kk in range(1, win):
            tot = tot + xp_ref[pl.ds(hist - kk, tm), cols]
        cnt = jnp.minimum(pos + 1, win).astype(F32)
        pooled.append(tot / cnt - cur)
    pool_out = _pool_project(jnp.concatenate(pooled, axis=1), pw_ref, psc_ref[...])

    mix = (jnp.dot(lru_out.astype(BF16), wo_ref[:w, :], preferred_element_type=F32)
           + jnp.dot(pool_out.astype(BF16), wo_ref[w:, :], preferred_element_type=F32))
    o_ref[...] = x + mix

    conv_ref[...] = xl_ref[pl.ds(hist + tm - (CONV_WIDTH - 1), CONV_WIDTH - 1), :]
    lru_ref[...] = carry_ref[...]
    pool_ref[...] = xp_ref[pl.ds(hist + tm - POOL_BUF, POOL_BUF), :]
    xl_ref[:hist, :] = xl_ref[pl.ds(tm, hist), :]
    xp_ref[:hist, :] = xp_ref[pl.ds(tm, hist), :]


def _cd_prompt(x, g, cdw, bsz, seq, tm):
    n, d = x.shape
    win_bf, cw, cb, wg_bf, ba, bx, lam, pw_bf, psc, wo_bf = cdw
    w = cw.shape[1]
    nt = seq // tm
    hist = POOL_BUF + 1
    row = lambda b, i: (b * nt + i, 0)
    c2 = lambda b, i: (0, 0)
    c3 = lambda b, i: (0, 0, 0)
    st = lambda b, i: (b, 0, 0)
    return pl.pallas_call(
        functools.partial(_cd_prompt_kernel, tm=tm, w=w),
        out_shape=(jax.ShapeDtypeStruct((n, d), F32),
                   jax.ShapeDtypeStruct((bsz, CONV_WIDTH - 1, w), F32),
                   jax.ShapeDtypeStruct((bsz, 1, w), F32),
                   jax.ShapeDtypeStruct((bsz, POOL_BUF, w), F32)),
        grid=(bsz, nt),
        in_specs=[pl.BlockSpec((tm, d), row), pl.BlockSpec((1, d), c2),
                  pl.BlockSpec(win_bf.shape, c2), pl.BlockSpec(cw.shape, c2), pl.BlockSpec((1, w), c2),
                  pl.BlockSpec(wg_bf.shape, c3), pl.BlockSpec((1, w), c2), pl.BlockSpec((1, w), c2),
                  pl.BlockSpec((1, w), c2), pl.BlockSpec(pw_bf.shape, c3), pl.BlockSpec((1, w), c2),
                  pl.BlockSpec(wo_bf.shape, c2)],
        out_specs=(pl.BlockSpec((tm, d), row),
                   pl.BlockSpec((None, CONV_WIDTH - 1, w), st),
                   pl.BlockSpec((None, 1, w), st),
                   pl.BlockSpec((None, POOL_BUF, w), st)),
        scratch_shapes=[pltpu.VMEM((hist + tm, w), F32), pltpu.VMEM((hist + tm, w), F32),
                        pltpu.VMEM((tm, w), F32), pltpu.VMEM((tm, w), F32), pltpu.VMEM((tm, w), F32),
                        pltpu.VMEM((1, w), F32)],
        compiler_params=_cparams(2),
        name="cd_prompt",
    )(x, g, win_bf, cw, cb, wg_bf, ba, bx, lam, pw_bf, psc, wo_bf)


def _cd_step_kernel(x_ref, g_ref, win_ref, cw_ref, cb_ref, wg_ref, ba_ref, bx_ref, lam_ref,
                    pw_ref, psc_ref, wo_ref, sc_ref, sl_ref, sp_ref,
                    o_ref, conv_ref, lru_ref, pool_ref, *, w):
    x = x_ref[...]
    hn = _rms(x, g_ref[...], NORM_EPS).astype(BF16)
    proj = jnp.dot(hn, win_ref[...], preferred_element_type=F32)
    gate = proj[:, :w]
    xl = proj[:, w:2 * w]
    xp = proj[:, 2 * w:]
    xc = cb_ref[...] + xl * cw_ref[CONV_WIDTH - 1:CONV_WIDTH, :]
    for j in range(CONV_WIDTH - 1):
        xc = xc + sc_ref[j] * cw_ref[j:j + 1, :]
    a, b = _lru_gates(xc, wg_ref, ba_ref[...], bx_ref[...], lam_ref[...])
    h = a * sl_ref[...] + b
    lru_out = jax.nn.gelu(gate) * h

    pg = w // len(POOL_WINDOWS)
    pooled = []
    for gidx, win in enumerate(POOL_WINDOWS):
        cols = slice(gidx * pg, (gidx + 1) * pg)
        cur = xp[:, cols]
        tot = cur
        for kk in range(1, win):
            tot = tot + sp_ref[POOL_BUF - kk][:, cols]
        pooled.append(tot / float(win) - cur)
    pool_out = _pool_project(jnp.concatenate(pooled, axis=1), pw_ref, psc_ref[...])

    mix = (jnp.dot(lru_out.astype(BF16), wo_ref[:w, :], preferred_element_type=F32)
           + jnp.dot(pool_out.astype(BF16), wo_ref[w:, :], preferred_element_type=F32))
    o_ref[...] = x + mix
    for j in range(CONV_WIDTH - 2):
        conv_ref[j] = sc_ref[j + 1]
    conv_ref[CONV_WIDTH - 2] = xl
    lru_ref[...] = h
    for j in range(POOL_BUF - 1):
        pool_ref[j] = sp_ref[j + 1]
    pool_ref[POOL_BUF - 1] = xp


def _cd_step(x, g, cdw, conv_buf, lru_h0, pool_buf):
    n, d = x.shape
    win_bf, cw, cb, wg_bf, ba, bx, lam, pw_bf, psc, wo_bf = cdw
    w = cw.shape[1]
    sc = conv_buf.transpose(1, 0, 2)
    sp = pool_buf.transpose(1, 0, 2)
    o, conv_n, lru_n, pool_n = pl.pallas_call(
        functools.partial(_cd_step_kernel, w=w),
        out_shape=(jax.ShapeDtypeStruct((n, d), F32),
                   jax.ShapeDtypeStruct((CONV_WIDTH - 1, n, w), F32),
                   jax.ShapeDtypeStruct((n, w), F32),
                   jax.ShapeDtypeStruct((POOL_BUF, n, w), F32)),
        compiler_params=pltpu.CompilerParams(vmem_limit_bytes=VMEM_LIMIT),
        name="cd_step",
    )(x, g, win_bf, cw, cb, wg_bf, ba, bx, lam, pw_bf, psc, wo_bf, sc, lru_h0, sp)
    return o, conv_n.transpose(1, 0, 2), lru_n, pool_n.transpose(1, 0, 2)


def _block_diag_tiles(w, tile):
    nb, c, _ = w.shape
    per = tile // c
    out = jnp.zeros((nb // per, tile, tile), w.dtype)
    for i in range(nb):
        t, o = divmod(i, per)
        out = out.at[t, o * c:(o + 1) * c, o * c:(o + 1) * c].set(w[i])
    return out


def _rope_tables(pos, rows):
    half = HEAD_DIM // 2
    inv = ROPE_THETA ** (-jnp.arange(half, dtype=F32) / half)
    ang = pos.astype(F32)[:, None] * inv[None, :]
    cos = jnp.cos(ang)
    sin = jnp.sin(ang)
    reps = LANES // HEAD_DIM
    cos_t = jnp.tile(jnp.concatenate([cos, cos], axis=1), (1, reps))
    sin_t = jnp.tile(jnp.concatenate([-sin, sin], axis=1), (1, reps))
    if cos_t.shape[0] != rows:
        cos_t = jnp.broadcast_to(cos_t, (rows, LANES))
        sin_t = jnp.broadcast_to(sin_t, (rows, LANES))
    return cos_t, sin_t


def _row_tile(n, target):
    t = min(n, target)
    while n % t:
        t //= 2
    return t


def kernel(x_prompt, x_sample, cache_k, cache_v, page_table, state_s5_re, state_s5_im, state_conv, state_lru, state_pool, norm_mix, norm_ffn, norm_final, w_in_ab, w_out_ab, s5_a_re, s5_a_im, s5_log_dt, s5_b_re, s5_b_im, s5_c_re, s5_c_im, s5_d, s5_w_glu, s5_b_glu, diff_lq1, diff_lk1, diff_lq2, diff_lk2, diff_subln, w_in_cd, w_out_cd, conv_w, conv_b, lru_wa, lru_ba, lru_wx, lru_bx, lru_lambda, pool_w, pool_scale, ffn_w_gate, ffn_w_up, ffn_w_down):
    bsz, seq, d = x_prompt.shape
    nseq, dec_seq, _ = x_sample.shape
    assert dec_seq == 1
    depth = norm_mix.shape[0]
    npages = page_table.shape[1]
    page_size = cache_k.shape[2]
    past_len = npages * page_size
    assert past_len >= POOL_BUF
    s5w = s5_d.shape[1]
    qkw = cache_k.shape[3] * cache_k.shape[4]
    vw_total = cache_v.shape[3] * cache_v.shape[4]
    heads = cache_v.shape[3]
    n_p = bsz * seq
    row2 = lambda a: a.reshape(1, -1)

    xp = x_prompt.reshape(n_p, d)
    xs = x_sample.reshape(nseq, d)
    tm_p = _row_tile(seq, 512)
    cos_p, sin_p = _rope_tables(jnp.arange(seq, dtype=jnp.int32), seq)
    cos_s, sin_s = _rope_tables(jnp.full((1,), past_len, jnp.int32), nseq)

    outs_p = {'k': [], 'v': [], 're': [], 'im': [], 'conv': [], 'lru': [], 'pool': []}
    outs_s = {'k': [], 'v': [], 're': [], 'im': [], 'conv': [], 'lru': [], 'pool': []}
    hidden = ffn_w_gate.shape[2]
    th = hidden // 2 if (hidden // 2) % LANES == 0 else hidden

    for l in range(depth):
        j = l // 2
        g_mix = row2(norm_mix[l])
        if l % 2 == 0:
            lam_init = 0.8 - 0.6 * math.exp(-0.3 * l)
            w_in = w_in_ab[j].astype(BF16)
            w_out = w_out_ab[j].astype(BF16)
            w_glu = s5_w_glu[j].astype(BF16)
            b_glu = row2(s5_b_glu[j])
            lamp = jnp.stack([diff_lq1[j], diff_lk1[j], diff_lq2[j], diff_lk2[j]]).astype(F32)
            gain = row2(diff_subln[j])
            s5p = (s5_a_re[j], s5_a_im[j], s5_log_dt[j], s5_b_re[j], s5_b_im[j],
                   s5_c_re[j], s5_c_im[j], s5_d[j])
            u, q_bf, k, v, k_bf, v_bf = _inproj_ab(xp, g_mix, w_in, cos_p, sin_p, tm_p, s5w, qkw)
            attn = _attn_prompt(q_bf, k_bf, v_bf, lamp, gain, bsz, seq, _row_tile(seq, 1024), tm_p,
                                lam_init)
            y, h_re, h_im = _s5_prompt(u, _s5_prompt_tables(s5p, seq), bsz, seq)
            xp = _post_ab(xp, y, attn, w_glu, b_glu, w_out, tm_p)
            outs_p['k'].append(k.reshape(bsz, seq, qkw // HEAD_DIM, HEAD_DIM))
            outs_p['v'].append(v.reshape(bsz, seq, heads, vw_total // heads))
            outs_p['re'].append(h_re)
            outs_p['im'].append(h_im)
            u, q_bf, k, v, _, _ = _inproj_ab(xs, g_mix, w_in, cos_s, sin_s, nseq, s5w, qkw)
            ck = jnp.transpose(cache_k[j], (0, 2, 3, 1)).reshape(-1, qkw, page_size)
            cv = cache_v[j].reshape(-1, page_size * heads, vw_total // heads)
            attn = _attn_decode(q_bf.astype(F32), k, v, ck, cv, page_table, lamp, gain,
                                math.gcd(npages, 16), lam_init)
            y, h_re, h_im = _s5_step(u, state_s5_re[j], state_s5_im[j], s5p)
            xs = _post_ab(xs, y, attn, w_glu, b_glu, w_out, nseq)
            outs_s['k'].append(k.reshape(nseq, 1, qkw // HEAD_DIM, HEAD_DIM))
            outs_s['v'].append(v.reshape(nseq, 1, heads, vw_total // heads))
            outs_s['re'].append(h_re)
            outs_s['im'].append(h_im)
        else:
            w = conv_w.shape[2]
            wg = jnp.concatenate([_block_diag_tiles(lru_wa[j], MXU_DIM),
                                  _block_diag_tiles(lru_wx[j], MXU_DIM)], axis=2).astype(BF16)
            cdw = (w_in_cd[j].astype(BF16), conv_w[j], row2(conv_b[j]), wg, row2(lru_ba[j]),
                   row2(lru_bx[j]), row2(lru_lambda[j]),
                   _block_diag_tiles(pool_w[j], MXU_DIM).astype(BF16), row2(pool_scale[j]),
                   w_out_cd[j].astype(BF16))
            xp, c_new, h_new, p_new = _cd_prompt(xp, g_mix, cdw, bsz, seq, tm_p)
            outs_p['conv'].append(c_new)
            outs_p['lru'].append(h_new.reshape(bsz, w))
            outs_p['pool'].append(p_new)
            xs, c_new, h_new, p_new = _cd_step(xs, g_mix, cdw, state_conv[j], state_lru[j], state_pool[j])
            outs_s['conv'].append(c_new)
            outs_s['lru'].append(h_new)
            outs_s['pool'].append(p_new)
        last = l == depth - 1
        g_ffn = row2(norm_ffn[l])
        g_fin = row2(norm_final)
        wg_f = ffn_w_gate[l].astype(BF16)
        wu_f = ffn_w_up[l].astype(BF16)
        wd_f = ffn_w_down[l].astype(BF16)
        xp = _ffn(xp, g_ffn, wg_f, wu_f, wd_f, g_fin, _row_tile(n_p, 512), th, last)
        xs = _ffn(xs, g_ffn, wg_f, wu_f, wd_f, g_fin, nseq, th, last)

    sp = {name: jnp.stack(vals) for name, vals in outs_p.items()}
    ss = {name: jnp.stack(vals) for name, vals in outs_s.items()}
    y_prompt = xp.reshape(bsz, seq, d)
    y_sample = xs.reshape(nseq, 1, d)
    return (y_prompt, y_sample, sp['k'], sp['v'], ss['k'], ss['v'], sp['re'], sp['im'], ss['re'], ss['im'],
            sp['conv'], ss['conv'], sp['lru'], ss['lru'], sp['pool'], ss['pool'])
```

```python
import functools
import math

import jax
import jax.numpy as jnp
from jax import lax
from jax.experimental import pallas as pl
from jax.experimental.pallas import tpu as pltpu

F32 = jnp.float32
BF16 = jnp.bfloat16

HEAD_DIM = 64
ROPE_THETA = 10000.0
NORM_EPS = 1e-6
SUBLN_EPS = 1e-5
S5_GROUP = 16
S5_CHUNK = 16
S5_SCAN_STRIDES = (1, 2, 4)
LRU_C = 8.0
CONV_WIDTH = 4
POOL_WINDOWS = (2, 4, 8, 16)
POOL_BUF = max(POOL_WINDOWS) - 1

LANES = 128
SUBLANES = 8
MXU_DIM = 256
VMEM_LIMIT = 56 * 1024 * 1024

NEG_BIG = -1e30
LOG2E = math.log2(math.e)


def _cparams(n_axes, flags=None):
    return pltpu.CompilerParams(dimension_semantics=("arbitrary",) * n_axes,
                                vmem_limit_bytes=VMEM_LIMIT, flags=flags)


def _rms(x, g, eps):
    ms = jnp.mean(x * x, axis=-1, keepdims=True)
    return x * lax.rsqrt(ms + eps) * g


def _lane_tile(x, reps):
    return jnp.concatenate([x] * reps, axis=1)


def _rope(x, cos, sin_signed):
    width = x.shape[1]
    lane = lax.broadcasted_iota(jnp.int32, x.shape, 1)
    half = HEAD_DIM // 2
    rot = jnp.where((lane & half) == 0, pltpu.roll(x, width - half, 1), pltpu.roll(x, half, 1))
    return x * cos + rot * sin_signed


def _inproj_ab_kernel(x_ref, g_ref, w_ref, cos_ref, sin_ref,
                      u_ref, q_ref, k_ref, v_ref, kb_ref, vb_ref, *, s5w, qkw):
    h = _rms(x_ref[...], g_ref[...], NORM_EPS).astype(BF16)
    proj = jnp.dot(h, w_ref[...], preferred_element_type=F32)
    reps = qkw // LANES
    cos = _lane_tile(cos_ref[...], reps)
    sin = _lane_tile(sin_ref[...], reps)
    u_ref[...] = proj[:, :s5w]
    q = _rope(proj[:, s5w:s5w + qkw], cos, sin)
    k = _rope(proj[:, s5w + qkw:s5w + 2 * qkw], cos, sin)
    v = proj[:, s5w + 2 * qkw:]
    q_ref[...] = (q * (HEAD_DIM ** -0.5 * LOG2E)).astype(BF16)
    k_ref[...] = k
    v_ref[...] = v
    kb_ref[...] = k.astype(BF16)
    vb_ref[...] = v.astype(BF16)


def _inproj_ab(x, g, w_bf, cos_tab, sin_tab, tm, s5w, qkw):
    n, d = x.shape
    wn = w_bf.shape[1]
    vw = wn - s5w - 2 * qkw
    period = cos_tab.shape[0] // tm
    row = lambda i: (i, 0)
    const = lambda i: (0, 0)
    tab = lambda i: (i % period, 0)
    return pl.pallas_call(
        functools.partial(_inproj_ab_kernel, s5w=s5w, qkw=qkw),
        out_shape=(jax.ShapeDtypeStruct((n, s5w), F32), jax.ShapeDtypeStruct((n, qkw), BF16),
                   jax.ShapeDtypeStruct((n, qkw), F32), jax.ShapeDtypeStruct((n, vw), F32),
                   jax.ShapeDtypeStruct((n, qkw), BF16), jax.ShapeDtypeStruct((n, vw), BF16)),
        grid=(n // tm,),
        in_specs=[pl.BlockSpec((tm, d), row), pl.BlockSpec((1, d), const),
                  pl.BlockSpec((d, wn), const), pl.BlockSpec((tm, LANES), tab),
                  pl.BlockSpec((tm, LANES), tab)],
        out_specs=(pl.BlockSpec((tm, s5w), row), pl.BlockSpec((tm, qkw), row),
                   pl.BlockSpec((tm, qkw), row), pl.BlockSpec((tm, vw), row),
                   pl.BlockSpec((tm, qkw), row), pl.BlockSpec((tm, vw), row)),
        compiler_params=_cparams(1),
        name="inproj_ab",
    )(x, g, w_bf, cos_tab, sin_tab)


def _diff_lambda(lamp, lam_init):
    s1 = jnp.sum(lamp[0:1] * lamp[1:2], axis=1, keepdims=True)
    s2 = jnp.sum(lamp[2:3] * lamp[3:4], axis=1, keepdims=True)
    return jnp.exp(s1) - jnp.exp(s2) + lam_init


def _subln(o, gain, lam_init):
    ms = jnp.mean(o * o, axis=-1, keepdims=True)
    return o * lax.rsqrt(ms + SUBLN_EPS) * gain * (1.0 - lam_init)


ONES_ROWS = 16

def _attn_prompt_kernel(lamp_ref, q_ref, k_ref, v_ref, gain_ref, o_ref,
                        vt_ref, m_ref, acc_ref, *, tq, tk, lam_init):
    qi = pl.program_id(2)
    seq, vw = v_ref.shape
    per_q = tq // tk

    @pl.when(qi == 0)
    def _():
        for c in range(seq // tk):
            blk = v_ref[c * tk:(c + 1) * tk, :].astype(F32)
            vt_ref[c, :vw, :] = blk.T.astype(BF16)
            vt_ref[c, vw:, :] = jnp.ones((ONES_ROWS, tk), BF16)

    q = q_ref[...]
    lane = lax.broadcasted_iota(jnp.int32, q.shape, 1)
    zero = jnp.zeros_like(q)
    qs = (jnp.where(lane < HEAD_DIM, q, zero), jnp.where(lane >= HEAD_DIM, q, zero))
    m_ref[...] = jnp.full(m_ref.shape, NEG_BIG, F32)
    acc_ref[...] = jnp.zeros(acc_ref.shape, F32)

    def chunk(c, diag):
        off = pl.multiple_of(c * tk, tk)
        kc = k_ref[pl.ds(off, tk), :]
        vc = vt_ref[c]
        def scores(j):
            st = lax.dot_general(kc, qs[j], (((1,), (1,)), ((), ())), preferred_element_type=F32)
            if diag is not None:
                key = lax.broadcasted_iota(jnp.int32, st.shape, 0) + diag * tk
                qry = lax.broadcasted_iota(jnp.int32, st.shape, 1)
                st = jnp.where(key <= qry, st, -jnp.inf)
            return st

        def softmax(j, st):
            m_prev = m_ref[j]
            m_new = jnp.maximum(m_prev, jnp.max(st, axis=0, keepdims=True))
            m_ref[j] = m_new
            return jnp.exp2(m_prev - m_new), jnp.exp2((st - m_new).astype(BF16))

        def accumulate(j, alpha, p):
            pv = jnp.dot(vc, p, preferred_element_type=F32)
            acc_ref[j] = alpha * acc_ref[j] + pv

        alpha0, p0 = softmax(0, scores(0))
        st1 = scores(1)
        accumulate(0, alpha0, p0)
        alpha1, p1 = softmax(1, st1)
        accumulate(1, alpha1, p1)

    def body(c, carry):
        chunk(c, None)
        return carry

    lax.fori_loop(0, qi * per_q, body, 0)
    for d in range(per_q):
        chunk(qi * per_q + d, d)

    lam = _diff_lambda(lamp_ref[...], lam_init)
    a0 = acc_ref[0]
    a1 = acc_ref[1]
    ot = a0[:vw] / a0[vw:vw + 1] - lam * (a1[:vw] / a1[vw:vw + 1])
    o_ref[...] = _subln(ot.T, gain_ref[...], lam_init).astype(o_ref.dtype)


def _attn_prompt(q_bf, k_bf, v_bf, lamp, gain, bsz, seq, tq, tk, lam_init):
    n, qkw = q_bf.shape
    vw_total = v_bf.shape[1]
    heads = qkw // (2 * HEAD_DIM)
    vw = vw_total // heads
    nq = seq // tq
    return pl.pallas_call(
        functools.partial(_attn_prompt_kernel, tq=tq, tk=tk, lam_init=lam_init),
        out_shape=jax.ShapeDtypeStruct((n, vw_total), BF16),
        grid=(bsz, heads, nq),
        in_specs=[pl.BlockSpec((4, HEAD_DIM), lambda b, h, i: (0, 0)),
                  pl.BlockSpec((tq, 2 * HEAD_DIM), lambda b, h, i: (b * nq + i, h)),
                  pl.BlockSpec((seq, 2 * HEAD_DIM), lambda b, h, i: (b, h)),
                  pl.BlockSpec((seq, vw), lambda b, h, i: (b, h)),
                  pl.BlockSpec((1, vw), lambda b, h, i: (0, 0))],
        out_specs=pl.BlockSpec((tq, vw), lambda b, h, i: (b * nq + i, h)),
        scratch_shapes=[pltpu.VMEM((seq // tk, vw + ONES_ROWS, tk), BF16),
                        pltpu.VMEM((2, 1, tq), F32),
                        pltpu.VMEM((2, vw + ONES_ROWS, tq), F32)],
        compiler_params=_cparams(3),
        name="attn_prompt",
    )(lamp, q_bf, k_bf, v_bf, gain)


def _attn_decode_kernel(pt_ref, lamp_ref, q_ref, kself_ref, vself_ref, gain_ref, *rest,
                        pages, lam_init, vw):
    k_refs = rest[:pages]
    v_refs = rest[pages:2 * pages]
    o_ref = rest[2 * pages]
    m_ref, l_ref, acc_ref = rest[2 * pages + 1:]
    j = pl.program_id(1)
    nh2 = m_ref.shape[0]
    width = q_ref.shape[1]

    @pl.when(j == 0)
    def _():
        m_ref[...] = jnp.full(m_ref.shape, NEG_BIG, F32)
        l_ref[...] = jnp.zeros(l_ref.shape, F32)
        acc_ref[...] = jnp.zeros(acc_ref.shape, F32)

    rowh = lax.broadcasted_iota(jnp.int32, (nh2, width), 0)
    laneh = lax.broadcasted_iota(jnp.int32, (nh2, width), 1) // HEAD_DIM
    q8 = jnp.where(rowh == laneh, jnp.broadcast_to(q_ref[...], (nh2, width)), 0.0)
    q8b = q8.astype(BF16)

    s_parts = [jnp.dot(q8b, k_refs[i][...].astype(BF16), preferred_element_type=F32)
               for i in range(pages)]
    s = jnp.concatenate(s_parts, axis=1)
    m_prev = m_ref[...]
    m_new = jnp.maximum(m_prev, jnp.max(s, axis=1, keepdims=True))
    alpha = jnp.exp2(m_prev - m_new)
    p = jnp.exp2(s - _lane_tile(m_new, s.shape[1] // LANES))
    l_ref[...] = alpha * l_ref[...] + jnp.sum(p, axis=1, keepdims=True)
    ps = k_refs[0].shape[1]
    nvh = width // vw
    pb = p.astype(BF16)
    pv = None
    for i in range(pages):
        cols = [jnp.dot(pb[:, i * ps:(i + 1) * ps],
                        v_refs[i][pl.ds(h, ps, stride=nvh), :].astype(BF16),
                        preferred_element_type=F32) for h in range(nvh)]
        part = jnp.concatenate(cols, axis=1)
        pv = part if pv is None else pv + part
    acc_ref[...] = _lane_tile(alpha, width // LANES) * acc_ref[...] + pv
    m_ref[...] = m_new

    @pl.when(j == pl.num_programs(1) - 1)
    def _():
        s_self = jnp.sum(q8 * kself_ref[...], axis=1, keepdims=True)
        m_old = m_ref[...]
        m_fin = jnp.maximum(m_old, s_self)
        a_fin = jnp.exp2(m_old - m_fin)
        p_self = jnp.exp2(s_self - m_fin)
        l_fin = a_fin * l_ref[...] + p_self
        acc = (_lane_tile(a_fin, width // LANES) * acc_ref[...]
               + _lane_tile(p_self, width // LANES) * vself_ref[...])
        o8 = acc / _lane_tile(l_fin, width // LANES)
        lam = _diff_lambda(lamp_ref[...], lam_init)
        vhead = lax.broadcasted_iota(jnp.int32, (nh2, width), 1) // vw
        sign = jnp.where((rowh & 1) == 0, 1.0, -lam)
        coef = jnp.where(vhead == rowh // 2, sign, 0.0)
        o = jnp.sum(o8 * coef, axis=0, keepdims=True)
        gain = gain_ref[...]
        outs = [_subln(o[:, h * vw:(h + 1) * vw], gain, lam_init) for h in range(width // vw)]
        o_ref[...] = jnp.concatenate(outs, axis=1)


def _attn_decode(q, k_self, v_self, cache_kt, cache_v2, page_table, lamp, gain, pages, lam_init):
    nseq, width = q.shape
    npages = page_table.shape[1]
    ps = cache_kt.shape[2]
    vw = gain.shape[1]
    nh2 = width // HEAD_DIM
    pt = page_table.reshape(-1)
    q3, k3, v3 = (a.reshape(nseq, 1, width) for a in (q, k_self, v_self))
    seq_spec = pl.BlockSpec((None, 1, width), lambda b, j, pt: (b, 0, 0))

    def page_spec(i, shape):
        return pl.BlockSpec((None,) + shape,
                            lambda b, j, pt: (pt[b * npages + j * pages + i], 0, 0))

    grid_spec = pltpu.PrefetchScalarGridSpec(
        num_scalar_prefetch=1,
        grid=(nseq, npages // pages),
        in_specs=[pl.BlockSpec((4, HEAD_DIM), lambda b, j, pt: (0, 0)), seq_spec, seq_spec, seq_spec,
                  pl.BlockSpec((1, vw), lambda b, j, pt: (0, 0))]
                 + [page_spec(i, cache_kt.shape[1:]) for i in range(pages)]
                 + [page_spec(i, cache_v2.shape[1:]) for i in range(pages)],
        out_specs=seq_spec,
        scratch_shapes=[pltpu.VMEM((nh2, LANES), F32), pltpu.VMEM((nh2, LANES), F32),
                        pltpu.VMEM((nh2, width), F32)],
    )
    out = pl.pallas_call(
        functools.partial(_attn_decode_kernel, pages=pages, lam_init=lam_init, vw=vw),
        out_shape=jax.ShapeDtypeStruct((nseq, 1, width), F32),
        grid_spec=grid_spec,
        compiler_params=_cparams(2),
        name="attn_decode",
    )(pt, lamp, q3, k3, v3, gain, *([cache_kt] * pages), *([cache_v2] * pages))
    return out.reshape(nseq, width)


def _s5_discretize(a_re, a_im, log_dt, b_re, b_im):
    lam = lax.complex(a_re.astype(F32), a_im.astype(F32))
    step = jnp.exp(log_dt.astype(F32))[:, None]
    lam_bar = jnp.exp(lam * step)
    b_bar = ((lam_bar - 1.0) / lam)[:, :, None] * lax.complex(b_re.astype(F32), b_im.astype(F32))
    return lam * step, b_bar


def _s5_power(log_lam_bar, k):
    return jnp.exp(log_lam_bar * k)


def _reim(z, swap=False):
    parts = (jnp.imag(z), jnp.real(z)) if swap else (jnp.real(z), jnp.imag(z))
    return jnp.concatenate(parts, axis=-1)


def _s5_prompt_tables(params, seq):
    a_re, a_im, log_dt, b_re, b_im, c_re, c_im, d_skip = params
    g, p = a_re.shape
    ch, t = S5_GROUP, S5_CHUNK
    llb, b_bar = _s5_discretize(a_re, a_im, log_dt, b_re, b_im)
    c = lax.complex(c_re.astype(F32), c_im.astype(F32))
    lags = jnp.arange(t + 1, dtype=F32)
    powk = _s5_power(llb[None], lags[:, None, None])
    klag = jnp.real(jnp.einsum('gcp,kgp,gpd->gkcd', c, powk[:t], b_bar))
    s_idx = jnp.arange(t)[:, None]
    t_idx = jnp.arange(t)[None, :]
    lag = t_idx - s_idx
    blocks = klag[:, jnp.clip(lag, 0, t - 1)]
    blocks = jnp.where((lag >= 0)[None, :, :, None, None], blocks, 0.0)
    eye = jnp.eye(ch, dtype=F32)
    skip = d_skip.astype(F32).reshape(g, ch)[:, None, None, :, None] * eye[None, None, None]
    blocks = blocks + jnp.where((lag == 0)[None, :, :, None, None], skip, 0.0)
    m_mat = blocks.transpose(0, 1, 4, 2, 3).reshape(g, t * ch, t * ch)
    e_c = powk[:t][::-1].transpose(1, 0, 2)[:, :, None, :] * b_bar.transpose(0, 2, 1)[:, None]
    e_c = e_c.reshape(g, t * ch, p)
    e_mat = jnp.concatenate([_reim(e_c), _reim(e_c, swap=True)], axis=-1)
    cp = c[:, None] * powk[1:t + 1].transpose(1, 0, 2)[:, :, None, :]
    cp = cp.reshape(g, t * ch, p)
    f_mat = jnp.concatenate([jnp.real(cp), -jnp.imag(cp)], axis=-1).transpose(0, 2, 1)
    ks = jnp.arange(1, SUBLANES + 1, dtype=F32) * t
    pw = _s5_power(llb[None], ks[:, None, None]).transpose(1, 0, 2)
    a1 = jnp.concatenate([jnp.real(pw), jnp.real(pw)], axis=-1)
    a2 = jnp.concatenate([-jnp.imag(pw), jnp.imag(pw)], axis=-1)
    rows = jnp.arange(SUBLANES)[None, :, None]
    parts = []
    for d in S5_SCAN_STRIDES:
        parts.append(jnp.where(rows >= d, a1[:, d - 1:d, :], 0.0))
        parts.append(jnp.where(rows >= d, a2[:, d - 1:d, :], 0.0))
    tab = jnp.concatenate(parts + [a1, a2], axis=1)
    return m_mat.astype(BF16), e_mat.astype(BF16), f_mat.astype(BF16), tab


def _s5_lane_permutation():
    n = SUBLANES * SUBLANES * S5_GROUP
    idx = jnp.arange(n).reshape(SUBLANES, SUBLANES, S5_GROUP).transpose(1, 0, 2).reshape(n)
    return jax.nn.one_hot(idx, n, dtype=BF16)


def _s5_prompt_kernel(u_ref, perm_ref, e_ref, m_ref, f_ref, tab_ref, y_ref, hl_ref,
                      us_ref, ee_ref, h_ref, ycat_ref):
    t = S5_CHUNK
    ng = SUBLANES
    nsub = u_ref.shape[0] // t
    sw = tab_ref.shape[2]
    half_t = t // 2
    perm = perm_ref[...]
    for half in range(2):
        xs = [u_ref[pl.ds(half * half_t + s, nsub, stride=t), :].astype(BF16) for s in range(half_t)]
        uh = jnp.dot(jnp.concatenate(xs, axis=1), perm, preferred_element_type=F32).astype(BF16)
        for gl in range(ng):
            us_ref[gl, :, half * LANES:(half + 1) * LANES] = uh[:, gl * LANES:(gl + 1) * LANES]
    for gl in range(ng):
        ee_ref[gl] = jnp.dot(us_ref[gl], e_ref[gl], preferred_element_type=F32)
        h_ref[gl, :SUBLANES, :] = jnp.zeros((SUBLANES, sw), F32)

    def step(i, carry):
        row = pl.multiple_of(i * SUBLANES, SUBLANES)
        out = []
        for gl in range(ng):
            h = ee_ref[gl, pl.ds(row, SUBLANES), :sw]
            hs = ee_ref[gl, pl.ds(row, SUBLANES), sw:]
            for di, d in enumerate(S5_SCAN_STRIDES):
                t1 = tab_ref[gl, 2 * di * SUBLANES:(2 * di + 1) * SUBLANES, :]
                t2 = tab_ref[gl, (2 * di + 1) * SUBLANES:(2 * di + 2) * SUBLANES, :]
                sh = pltpu.roll(h, d, 0)
                shs = pltpu.roll(hs, d, 0)
                h, hs = h + t1 * sh + t2 * shs, hs + t1 * shs - t2 * sh
            nd = 2 * len(S5_SCAN_STRIDES)
            q1 = tab_ref[gl, nd * SUBLANES:(nd + 1) * SUBLANES, :]
            q2 = tab_ref[gl, (nd + 1) * SUBLANES:(nd + 2) * SUBLANES, :]
            cb, cbs = carry[2 * gl], carry[2 * gl + 1]
            hf = h + q1 * cb + q2 * cbs
            hfs = hs + q1 * cbs - q2 * cb
            h_ref[gl, pl.ds(SUBLANES + row, SUBLANES), :] = hf
            out.append(jnp.broadcast_to(hf[SUBLANES - 1:SUBLANES, :], hf.shape))
            out.append(jnp.broadcast_to(hfs[SUBLANES - 1:SUBLANES, :], hfs.shape))
        return tuple(out)

    zero = jnp.zeros((SUBLANES, sw), F32)
    carry = lax.fori_loop(0, nsub // SUBLANES, step, (zero,) * (2 * ng))
    for gl in range(ng):
        hl_ref[gl:gl + 1, :] = carry[2 * gl][0:1, :]
        hin = h_ref[gl, pl.ds(SUBLANES - 1, nsub), :].astype(BF16)
        y = (jnp.dot(us_ref[gl], m_ref[gl], preferred_element_type=F32)
             + jnp.dot(hin, f_ref[gl], preferred_element_type=F32))
        for half in range(2):
            ycat_ref[half, :, gl * LANES:(gl + 1) * LANES] = y[:, half * LANES:(half + 1) * LANES]
    for half in range(2):
        yc = ycat_ref[half]
        hi = yc.astype(BF16)
        lo = (yc - hi.astype(F32)).astype(BF16)
        z = (jnp.dot(hi, perm, preferred_element_type=F32)
             + jnp.dot(lo, perm, preferred_element_type=F32))
        for s in range(half_t):
            y_ref[pl.ds(half * half_t + s, nsub, stride=t), :] = z[:, s * LANES:(s + 1) * LANES]


def _s5_prompt(u, tables, bsz, seq):
    m_mat, e_mat, f_mat, tab = tables
    g = m_mat.shape[0]
    p = f_mat.shape[1] // 2
    t, ch = S5_CHUNK, S5_GROUP
    ng = LANES // ch
    assert ng == SUBLANES and t * ch == 2 * LANES
    n, w = u.shape
    nsub = seq // t
    perm = _s5_lane_permutation()
    npm = perm.shape[0]
    ntab = tab.shape[1]
    slab = lambda s, b: (s, 0, 0)
    y, hl = pl.pallas_call(
        _s5_prompt_kernel,
        out_shape=(jax.ShapeDtypeStruct((n, w), F32), jax.ShapeDtypeStruct((bsz, g, 2 * p), F32)),
        grid=(w // LANES, bsz),
        in_specs=[pl.BlockSpec((seq, LANES), lambda s, b: (b, s)),
                  pl.BlockSpec((npm, npm), lambda s, b: (0, 0)),
                  pl.BlockSpec((ng, t * ch, 4 * p), slab),
                  pl.BlockSpec((ng, t * ch, t * ch), slab),
                  pl.BlockSpec((ng, 2 * p, t * ch), slab),
                  pl.BlockSpec((ng, ntab, 2 * p), slab)],
        out_specs=(pl.BlockSpec((seq, LANES), lambda s, b: (b, s)),
                   pl.BlockSpec((None, ng, 2 * p), lambda s, b: (b, s, 0))),
        scratch_shapes=[pltpu.VMEM((ng, nsub, t * ch), BF16),
                        pltpu.VMEM((ng, nsub, 4 * p), F32),
                        pltpu.VMEM((ng, SUBLANES + nsub, 2 * p), F32),
                        pltpu.VMEM((2, nsub, ng * LANES), F32)],
        compiler_params=_cparams(2),
        name="s5_prompt",
    )(u, perm, e_mat, m_mat, f_mat, tab)
    return y, hl[..., :p], hl[..., p:]


def _s5_step_kernel(u_ref, h_ref, hs_ref, bt_ref, ct_ref, a1_ref, a2_ref, d_ref, y_ref, hn_ref):
    hp = lax.Precision.HIGHEST
    u = u_ref[...]
    bu = jnp.einsum('gbc,gcp->gbp', u, bt_ref[...], precision=hp, preferred_element_type=F32)
    hn = a1_ref[...] * h_ref[...] + a2_ref[...] * hs_ref[...] + bu
    hn_ref[...] = hn
    y = jnp.einsum('gbp,gpc->gbc', hn, ct_ref[...], precision=hp, preferred_element_type=F32)
    y_ref[...] = y + d_ref[...] * u


def _s5_step(u, h0_re, h0_im, params):
    a_re, a_im, log_dt, b_re, b_im, c_re, c_im, d_skip = params
    g, p = a_re.shape
    ch = S5_GROUP
    bsz = u.shape[0]
    llb, b_bar = _s5_discretize(a_re, a_im, log_dt, b_re, b_im)
    lam_bar = jnp.exp(llb)
    bt = _reim(b_bar.transpose(0, 2, 1))
    c = lax.complex(c_re.astype(F32), c_im.astype(F32))
    ct = jnp.concatenate([jnp.real(c), -jnp.imag(c)], axis=-1).transpose(0, 2, 1)
    a1 = jnp.concatenate([jnp.real(lam_bar)] * 2, axis=-1)[:, None, :]
    a2 = jnp.concatenate([-jnp.imag(lam_bar), jnp.imag(lam_bar)], axis=-1)[:, None, :]
    h = jnp.concatenate([h0_re, h0_im], axis=-1).transpose(1, 0, 2).astype(F32)
    hs = jnp.concatenate([h0_im, h0_re], axis=-1).transpose(1, 0, 2).astype(F32)
    ug = u.reshape(bsz, g, ch).transpose(1, 0, 2)
    dg = d_skip.astype(F32).reshape(g, 1, ch)
    y, hn = pl.pallas_call(
        _s5_step_kernel,
        out_shape=(jax.ShapeDtypeStruct((g, bsz, ch), F32), jax.ShapeDtypeStruct((g, bsz, 2 * p), F32)),
        compiler_params=pltpu.CompilerParams(vmem_limit_bytes=VMEM_LIMIT),
        name="s5_step",
    )(ug, h, hs, bt, ct, a1, a2, dg)
    hn = hn.transpose(1, 0, 2)
    return y.transpose(1, 0, 2).reshape(bsz, g * ch), hn[..., :p], hn[..., p:]


def _post_ab_kernel(x_ref, y_ref, a_ref, wglu_ref, bglu_ref, wo_ref, o_ref, *, s5w):
    z = jax.nn.gelu(y_ref[...])
    gl = jnp.dot(z.astype(BF16), wglu_ref[...], preferred_element_type=F32) + bglu_ref[...]
    s5o = (z * jax.nn.sigmoid(gl)).astype(BF16)
    mix = (jnp.dot(s5o, wo_ref[:s5w, :], preferred_element_type=F32)
           + jnp.dot(a_ref[...].astype(BF16), wo_ref[s5w:, :], preferred_element_type=F32))
    o_ref[...] = x_ref[...] + mix


def _post_ab(x, y, attn, wglu_bf, bglu, wo_bf, tm):
    n, d = x.shape
    s5w = y.shape[1]
    aw = attn.shape[1]
    row = lambda i: (i, 0)
    const = lambda i: (0, 0)
    return pl.pallas_call(
        functools.partial(_post_ab_kernel, s5w=s5w),
        out_shape=jax.ShapeDtypeStruct((n, d), F32),
        grid=(n // tm,),
        in_specs=[pl.BlockSpec((tm, d), row), pl.BlockSpec((tm, s5w), row), pl.BlockSpec((tm, aw), row),
                  pl.BlockSpec((s5w, s5w), const), pl.BlockSpec((1, s5w), const),
                  pl.BlockSpec((s5w + aw, d), const)],
        out_specs=pl.BlockSpec((tm, d), row),
        compiler_params=_cparams(1),
        name="post_ab",
    )(x, y, attn, wglu_bf, bglu, wo_bf)


def _ffn_kernel(x_ref, g_ref, wg_ref, wu_ref, wd_ref, gf_ref, o_ref, *, th, final_norm):
    x = x_ref[...]
    hn = _rms(x, g_ref[...], NORM_EPS).astype(BF16)
    hidden = wg_ref.shape[1]
    acc = None
    for c in range(hidden // th):
        cols = slice(c * th, (c + 1) * th)
        gate = jnp.dot(hn, wg_ref[:, cols], preferred_element_type=F32)
        up = jnp.dot(hn, wu_ref[:, cols], preferred_element_type=F32)
        act = (jax.nn.silu(gate) * up).astype(BF16)
        part = jnp.dot(act, wd_ref[cols, :], preferred_element_type=F32)
        acc = part if acc is None else acc + part
    out = x + acc
    if final_norm:
        out = _rms(out, gf_ref[...], NORM_EPS)
    o_ref[...] = out


def _ffn(x, g, wg_bf, wu_bf, wd_bf, g_final, tm, th, final_norm):
    n, d = x.shape
    hidden = wg_bf.shape[1]
    const = lambda i: (0, 0)
    resident = dict(pipeline_mode=pl.Buffered(1))
    return pl.pallas_call(
        functools.partial(_ffn_kernel, th=th, final_norm=final_norm),
        out_shape=jax.ShapeDtypeStruct((n, d), F32),
        grid=(n // tm,),
        in_specs=[pl.BlockSpec((tm, d), lambda i: (i, 0)), pl.BlockSpec((1, d), const),
                  pl.BlockSpec((d, hidden), const, **resident),
                  pl.BlockSpec((d, hidden), const, **resident),
                  pl.BlockSpec((hidden, d), const, **resident), pl.BlockSpec((1, d), const)],
        out_specs=pl.BlockSpec((tm, d), lambda i: (i, 0)),
        compiler_params=_cparams(1),
        name="ffn",
    )(x, g, wg_bf, wu_bf, wd_bf, g_final)


def _lru_gates(xc, wg_ref, ba, bx, lam_param):
    xcb = xc.astype(BF16)
    halves = [jnp.dot(xcb[:, h * MXU_DIM:(h + 1) * MXU_DIM], wg_ref[h], preferred_element_type=F32)
              for h in range(wg_ref.shape[0])]
    r_pre = jnp.concatenate([hv[:, :MXU_DIM] for hv in halves], axis=1)
    i_pre = jnp.concatenate([hv[:, MXU_DIM:] for hv in halves], axis=1)
    r = jax.nn.sigmoid(r_pre + ba)
    i = jax.nn.sigmoid(i_pre + bx)
    log_a = -LRU_C * r * jax.nn.softplus(-lam_param)
    a = jnp.exp(log_a)
    b = jnp.sqrt(-jnp.tanh(log_a) * (a * a + 1.0)) * (i * xc)
    return a, b


def _pool_project(pooled, pw_ref, scale):
    pb = pooled.astype(BF16)
    halves = [jnp.dot(pb[:, h * MXU_DIM:(h + 1) * MXU_DIM], pw_ref[h], preferred_element_type=F32)
              for h in range(pw_ref.shape[0])]
    return jnp.concatenate(halves, axis=1) * scale


def _cd_prompt_kernel(x_ref, g_ref, win_ref, cw_ref, cb_ref, wg_ref, ba_ref, bx_ref, lam_ref,
                      pw_ref, psc_ref, wo_ref,
                      o_ref, conv_ref, lru_ref, pool_ref,
                      xl_ref, xp_ref, a_ref, b_ref, hs_ref, carry_ref, *, tm, w):
    ti = pl.program_id(1)
    hist = POOL_BUF + 1

    @pl.when(ti == 0)
    def _():
        xl_ref[:hist, :] = jnp.zeros((hist, w), F32)
        xp_ref[:hist, :] = jnp.zeros((hist, w), F32)
        carry_ref[...] = jnp.zeros(carry_ref.shape, F32)

    x = x_ref[...]
    hn = _rms(x, g_ref[...], NORM_EPS).astype(BF16)
    proj = jnp.dot(hn, win_ref[...], preferred_element_type=F32)
    gate = proj[:, :w]
    xl_ref[hist:, :] = proj[:, w:2 * w]
    xp_ref[hist:, :] = proj[:, 2 * w:]

    xc = cb_ref[...] + xl_ref[pl.ds(hist - (CONV_WIDTH - 1), tm), :] * cw_ref[0:1, :]
    for j in range(1, CONV_WIDTH):
        xc = xc + xl_ref[pl.ds(hist - (CONV_WIDTH - 1) + j, tm), :] * cw_ref[j:j + 1, :]

    a, b = _lru_gates(xc, wg_ref, ba_ref[...], bx_ref[...], lam_ref[...])
    a_ref[...] = a
    b_ref[...] = b

    def group(gi, carry):
        row = pl.multiple_of(gi * SUBLANES, SUBLANES)
        ag = a_ref[pl.ds(row, SUBLANES), :]
        bg = b_ref[pl.ds(row, SUBLANES), :]
        ridx = lax.broadcasted_iota(jnp.int32, ag.shape, 0)
        for d in (1, 2, 4):
            a_sh = jnp.where(ridx >= d, pltpu.roll(ag, d, 0), 1.0)
            b_sh = jnp.where(ridx >= d, pltpu.roll(bg, d, 0), 0.0)
            bg = ag * b_sh + bg
            ag = ag * a_sh
        h = bg + ag * carry
        hs_ref[pl.ds(row, SUBLANES), :] = h
        return jnp.broadcast_to(h[SUBLANES - 1:SUBLANES, :], h.shape)

    carry = lax.fori_loop(0, tm // SUBLANES, group, jnp.broadcast_to(carry_ref[...], (SUBLANES, w)))
    carry_ref[...] = carry[0:1, :]
    lru_out = jax.nn.gelu(gate) * hs_ref[...]

    pg = w // len(POOL_WINDOWS)
    pos = ti * tm + lax.broadcasted_iota(jnp.int32, (tm, pg), 0)
    pooled = []
    for gidx, win in enumerate(POOL_WINDOWS):
        cols = slice(gidx * pg, (gidx + 1) * pg)
        cur = xp_ref[pl.ds(hist, tm), cols]
        tot = cur
        for kk in range(1, win):
            tot = tot + xp_ref[pl.ds(hist - kk, tm), cols]
        cnt = jnp.minimum(pos + 1, win).astype(F32)
        pooled.append(tot / cnt - cur)
    pool_out = _pool_project(jnp.concatenate(pooled, axis=1), pw_ref, psc_ref[...])

    mix = (jnp.dot(lru_out.astype(BF16), wo_ref[:w, :], preferred_element_type=F32)
           + jnp.dot(pool_out.astype(BF16), wo_ref[w:, :], preferred_element_type=F32))
    o_ref[...] = x + mix

    conv_ref[...] = xl_ref[pl.ds(hist + tm - (CONV_WIDTH - 1), CONV_WIDTH - 1), :]
    lru_ref[...] = carry_ref[...]
    pool_ref[...] = xp_ref[pl.ds(hist + tm - POOL_BUF, POOL_BUF), :]
    xl_ref[:hist, :] = xl_ref[pl.ds(tm, hist), :]
    xp_ref[:hist, :] = xp_ref[pl.ds(tm, hist), :]


def _cd_prompt(x, g, cdw, bsz, seq, tm):
    n, d = x.shape
    win_bf, cw, cb, wg_bf, ba, bx, lam, pw_bf, psc, wo_bf = cdw
    w = cw.shape[1]
    nt = seq // tm
    hist = POOL_BUF + 1
    row = lambda b, i: (b * nt + i, 0)
    c2 = lambda b, i: (0, 0)
    c3 = lambda b, i: (0, 0, 0)
    st = lambda b, i: (b, 0, 0)
    return pl.pallas_call(
        functools.partial(_cd_prompt_kernel, tm=tm, w=w),
        out_shape=(jax.ShapeDtypeStruct((n, d), F32),
                   jax.ShapeDtypeStruct((bsz, CONV_WIDTH - 1, w), F32),
                   jax.ShapeDtypeStruct((bsz, 1, w), F32),
                   jax.ShapeDtypeStruct((bsz, POOL_BUF, w), F32)),
        grid=(bsz, nt),
        in_specs=[pl.BlockSpec((tm, d), row), pl.BlockSpec((1, d), c2),
                  pl.BlockSpec(win_bf.shape, c2), pl.BlockSpec(cw.shape, c2), pl.BlockSpec((1, w), c2),
                  pl.BlockSpec(wg_bf.shape, c3), pl.BlockSpec((1, w), c2), pl.BlockSpec((1, w), c2),
                  pl.BlockSpec((1, w), c2), pl.BlockSpec(pw_bf.shape, c3), pl.BlockSpec((1, w), c2),
                  pl.BlockSpec(wo_bf.shape, c2)],
        out_specs=(pl.BlockSpec((tm, d), row),
                   pl.BlockSpec((None, CONV_WIDTH - 1, w), st),
                   pl.BlockSpec((None, 1, w), st),
                   pl.BlockSpec((None, POOL_BUF, w), st)),
        scratch_shapes=[pltpu.VMEM((hist + tm, w), F32), pltpu.VMEM((hist + tm, w), F32),
                        pltpu.VMEM((tm, w), F32), pltpu.VMEM((tm, w), F32), pltpu.VMEM((tm, w), F32),
                        pltpu.VMEM((1, w), F32)],
        compiler_params=_cparams(2),
        name="cd_prompt",
    )(x, g, win_bf, cw, cb, wg_bf, ba, bx, lam, pw_bf, psc, wo_bf)


def _cd_step_kernel(x_ref, g_ref, win_ref, cw_ref, cb_ref, wg_ref, ba_ref, bx_ref, lam_ref,
                    pw_ref, psc_ref, wo_ref, sc_ref, sl_ref, sp_ref,
                    o_ref, conv_ref, lru_ref, pool_ref, *, w):
    x = x_ref[...]
    hn = _rms(x, g_ref[...], NORM_EPS).astype(BF16)
    proj = jnp.dot(hn, win_ref[...], preferred_element_type=F32)
    gate = proj[:, :w]
    xl = proj[:, w:2 * w]
    xp = proj[:, 2 * w:]
    xc = cb_ref[...] + xl * cw_ref[CONV_WIDTH - 1:CONV_WIDTH, :]
    for j in range(CONV_WIDTH - 1):
        xc = xc + sc_ref[j] * cw_ref[j:j + 1, :]
    a, b = _lru_gates(xc, wg_ref, ba_ref[...], bx_ref[...], lam_ref[...])
    h = a * sl_ref[...] + b
    lru_out = jax.nn.gelu(gate) * h

    pg = w // len(POOL_WINDOWS)
    pooled = []
    for gidx, win in enumerate(POOL_WINDOWS):
        cols = slice(gidx * pg, (gidx + 1) * pg)
        cur = xp[:, cols]
        tot = cur
        for kk in range(1, win):
            tot = tot + sp_ref[POOL_BUF - kk][:, cols]
        pooled.append(tot / float(win) - cur)
    pool_out = _pool_project(jnp.concatenate(pooled, axis=1), pw_ref, psc_ref[...])

    mix = (jnp.dot(lru_out.astype(BF16), wo_ref[:w, :], preferred_element_type=F32)
           + jnp.dot(pool_out.astype(BF16), wo_ref[w:, :], preferred_element_type=F32))
    o_ref[...] = x + mix
    for j in range(CONV_WIDTH - 2):
        conv_ref[j] = sc_ref[j + 1]
    conv_ref[CONV_WIDTH - 2] = xl
    lru_ref[...] = h
    for j in range(POOL_BUF - 1):
        pool_ref[j] = sp_ref[j + 1]
    pool_ref[POOL_BUF - 1] = xp


def _cd_step(x, g, cdw, conv_buf, lru_h0, pool_buf):
    n, d = x.shape
    win_bf, cw, cb, wg_bf, ba, bx, lam, pw_bf, psc, wo_bf = cdw
    w = cw.shape[1]
    sc = conv_buf.transpose(1, 0, 2)
    sp = pool_buf.transpose(1, 0, 2)
    o, conv_n, lru_n, pool_n = pl.pallas_call(
        functools.partial(_cd_step_kernel, w=w),
        out_shape=(jax.ShapeDtypeStruct((n, d), F32),
                   jax.ShapeDtypeStruct((CONV_WIDTH - 1, n, w), F32),
                   jax.ShapeDtypeStruct((n, w), F32),
                   jax.ShapeDtypeStruct((POOL_BUF, n, w), F32)),
        compiler_params=pltpu.CompilerParams(vmem_limit_bytes=VMEM_LIMIT),
        name="cd_step",
    )(x, g, win_bf, cw, cb, wg_bf, ba, bx, lam, pw_bf, psc, wo_bf, sc, lru_h0, sp)
    return o, conv_n.transpose(1, 0, 2), lru_n, pool_n.transpose(1, 0, 2)


def _block_diag_tiles(w, tile):
    nb, c, _ = w.shape
    per = tile // c
    out = jnp.zeros((nb // per, tile, tile), w.dtype)
    for i in range(nb):
        t, o = divmod(i, per)
        out = out.at[t, o * c:(o + 1) * c, o * c:(o + 1) * c].set(w[i])
    return out


def _rope_tables(pos, rows):
    half = HEAD_DIM // 2
    inv = ROPE_THETA ** (-jnp.arange(half, dtype=F32) / half)
    ang = pos.astype(F32)[:, None] * inv[None, :]
    cos = jnp.cos(ang)
    sin = jnp.sin(ang)
    reps = LANES // HEAD_DIM
    cos_t = jnp.tile(jnp.concatenate([cos, cos], axis=1), (1, reps))
    sin_t = jnp.tile(jnp.concatenate([-sin, sin], axis=1), (1, reps))
    if cos_t.shape[0] != rows:
        cos_t = jnp.broadcast_to(cos_t, (rows, LANES))
        sin_t = jnp.broadcast_to(sin_t, (rows, LANES))
    return cos_t, sin_t


def _row_tile(n, target):
    t = min(n, target)
    while n % t:
        t //= 2
    return t


def kernel(x_prompt, x_sample, cache_k, cache_v, page_table, state_s5_re, state_s5_im, state_conv, state_lru, state_pool, norm_mix, norm_ffn, norm_final, w_in_ab, w_out_ab, s5_a_re, s5_a_im, s5_log_dt, s5_b_re, s5_b_im, s5_c_re, s5_c_im, s5_d, s5_w_glu, s5_b_glu, diff_lq1, diff_lk1, diff_lq2, diff_lk2, diff_subln, w_in_cd, w_out_cd, conv_w, conv_b, lru_wa, lru_ba, lru_wx, lru_bx, lru_lambda, pool_w, pool_scale, ffn_w_gate, ffn_w_up, ffn_w_down):
    bsz, seq, d = x_prompt.shape
    nseq, dec_seq, _ = x_sample.shape
    assert dec_seq == 1
    depth = norm_mix.shape[0]
    npages = page_table.shape[1]
    page_size = cache_k.shape[2]
    past_len = npages * page_size
    assert past_len >= POOL_BUF
    s5w = s5_d.shape[1]
    qkw = cache_k.shape[3] * cache_k.shape[4]
    vw_total = cache_v.shape[3] * cache_v.shape[4]
    heads = cache_v.shape[3]
    n_p = bsz * seq
    row2 = lambda a: a.reshape(1, -1)

    xp = x_prompt.reshape(n_p, d)
    xs = x_sample.reshape(nseq, d)
    tm_p = _row_tile(seq, 512)
    cos_p, sin_p = _rope_tables(jnp.arange(seq, dtype=jnp.int32), seq)
    cos_s, sin_s = _rope_tables(jnp.full((1,), past_len, jnp.int32), nseq)

    outs_p = {'k': [], 'v': [], 're': [], 'im': [], 'conv': [], 'lru': [], 'pool': []}
    outs_s = {'k': [], 'v': [], 're': [], 'im': [], 'conv': [], 'lru': [], 'pool': []}
    hidden = ffn_w_gate.shape[2]
    th = MXU_DIM if hidden % MXU_DIM == 0 else hidden

    for l in range(depth):
        j = l // 2
        g_mix = row2(norm_mix[l])
        if l % 2 == 0:
            lam_init = 0.8 - 0.6 * math.exp(-0.3 * l)
            w_in = w_in_ab[j].astype(BF16)
            w_out = w_out_ab[j].astype(BF16)
            w_glu = s5_w_glu[j].astype(BF16)
            b_glu = row2(s5_b_glu[j])
            lamp = jnp.stack([diff_lq1[j], diff_lk1[j], diff_lq2[j], diff_lk2[j]]).astype(F32)
            gain = row2(diff_subln[j])
            s5p = (s5_a_re[j], s5_a_im[j], s5_log_dt[j], s5_b_re[j], s5_b_im[j],
                   s5_c_re[j], s5_c_im[j], s5_d[j])
            u, q_bf, k, v, k_bf, v_bf = _inproj_ab(xp, g_mix, w_in, cos_p, sin_p, tm_p, s5w, qkw)
            attn = _attn_prompt(q_bf, k_bf, v_bf, lamp, gain, bsz, seq, _row_tile(seq, 1024), tm_p,
                                lam_init)
            y, h_re, h_im = _s5_prompt(u, _s5_prompt_tables(s5p, seq), bsz, seq)
            xp = _post_ab(xp, y, attn, w_glu, b_glu, w_out, tm_p)
            outs_p['k'].append(k.reshape(bsz, seq, qkw // HEAD_DIM, HEAD_DIM))
            outs_p['v'].append(v.reshape(bsz, seq, heads, vw_total // heads))
            outs_p['re'].append(h_re)
            outs_p['im'].append(h_im)
            u, q_bf, k, v, _, _ = _inproj_ab(xs, g_mix, w_in, cos_s, sin_s, nseq, s5w, qkw)
            ck = jnp.transpose(cache_k[j], (0, 2, 3, 1)).reshape(-1, qkw, page_size)
            cv = cache_v[j].reshape(-1, page_size * heads, vw_total // heads)
            attn = _attn_decode(q_bf.astype(F32), k, v, ck, cv, page_table, lamp, gain,
                                math.gcd(npages, 16), lam_init)
            y, h_re, h_im = _s5_step(u, state_s5_re[j], state_s5_im[j], s5p)
            xs = _post_ab(xs, y, attn, w_glu, b_glu, w_out, nseq)
            outs_s['k'].append(k.reshape(nseq, 1, qkw // HEAD_DIM, HEAD_DIM))
            outs_s['v'].append(v.reshape(nseq, 1, heads, vw_total // heads))
            outs_s['re'].append(h_re)
            outs_s['im'].append(h_im)
        else:
            w = conv_w.shape[2]
            wg = jnp.concatenate([_block_diag_tiles(lru_wa[j], MXU_DIM),
                                  _block_diag_tiles(lru_wx[j], MXU_DIM)], axis=2).astype(BF16)
            cdw = (w_in_cd[j].astype(BF16), conv_w[j], row2(conv_b[j]), wg, row2(lru_ba[j]),
                   row2(lru_bx[j]), row2(lru_lambda[j]),
                   _block_diag_tiles(pool_w[j], MXU_DIM).astype(BF16), row2(pool_scale[j]),
                   w_out_cd[j].astype(BF16))
            xp, c_new, h_new, p_new = _cd_prompt(xp, g_mix, cdw, bsz, seq, tm_p)
            outs_p['conv'].append(c_new)
            outs_p['lru'].append(h_new.reshape(bsz, w))
            outs_p['pool'].append(p_new)
            xs, c_new, h_new, p_new = _cd_step(xs, g_mix, cdw, state_conv[j], state_lru[j], state_pool[j])
            outs_s['conv'].append(c_new)
            outs_s['lru'].append(h_new)
            outs_s['pool'].append(p_new)
        last = l == depth - 1
        g_ffn = row2(norm_ffn[l])
        g_fin = row2(norm_final)
        wg_f = ffn_w_gate[l].astype(BF16)
        wu_f = ffn_w_up[l].astype(BF16)
        wd_f = ffn_w_down[l].astype(BF16)
        xp = _ffn(xp, g_ffn, wg_f, wu_f, wd_f, g_fin, _row_tile(n_p, 512), th, last)
        xs = _ffn(xs, g_ffn, wg_f, wu_f, wd_f, g_fin, nseq, th, last)

    sp = {name: jnp.stack(vals) for name, vals in outs_p.items()}
    ss = {name: jnp.stack(vals) for name, vals in outs_s.items()}
    y_prompt = xp.reshape(bsz, seq, d)
    y_sample = xs.reshape(nseq, 1, d)
    return (y_prompt, y_sample, sp['k'], sp['v'], ss['k'], ss['v'], sp['re'], sp['im'], ss['re'], ss['im'],
            sp['conv'], ss['conv'], sp['lru'], ss['lru'], sp['pool'], ss['pool'])
```

```python
import functools
import math

import jax
import jax.numpy as jnp
from jax import lax
from jax.experimental import pallas as pl
from jax.experimental.pallas import tpu as pltpu

F32 = jnp.float32
BF16 = jnp.bfloat16

HEAD_DIM = 64
ROPE_THETA = 10000.0
NORM_EPS = 1e-6
SUBLN_EPS = 1e-5
S5_GROUP = 16
S5_CHUNK = 16
S5_SCAN_STRIDES = (1, 2, 4)
LRU_C = 8.0
CONV_WIDTH = 4
POOL_WINDOWS = (2, 4, 8, 16)
POOL_BUF = max(POOL_WINDOWS) - 1

LANES = 128
SUBLANES = 8
MXU_DIM = 256
VMEM_LIMIT = 56 * 1024 * 1024

NEG_BIG = -1e30
LOG2E = math.log2(math.e)


def _cparams(n_axes, flags=None):
    return pltpu.CompilerParams(dimension_semantics=("arbitrary",) * n_axes,
                                vmem_limit_bytes=VMEM_LIMIT, flags=flags)


def _rms(x, g, eps):
    ms = jnp.mean(x * x, axis=-1, keepdims=True)
    return x * lax.rsqrt(ms + eps) * g


def _lane_tile(x, reps):
    return jnp.concatenate([x] * reps, axis=1)


def _rope(x, cos, sin_signed):
    width = x.shape[1]
    lane = lax.broadcasted_iota(jnp.int32, x.shape, 1)
    half = HEAD_DIM // 2
    rot = jnp.where((lane & half) == 0, pltpu.roll(x, width - half, 1), pltpu.roll(x, half, 1))
    return x * cos + rot * sin_signed


def _inproj_ab_kernel(x_ref, g_ref, w_ref, cos_ref, sin_ref,
                      u_ref, q_ref, k_ref, v_ref, kb_ref, vb_ref, *, s5w, qkw):
    h = _rms(x_ref[...], g_ref[...], NORM_EPS).astype(BF16)
    proj = jnp.dot(h, w_ref[...], preferred_element_type=F32)
    reps = qkw // LANES
    cos = _lane_tile(cos_ref[...], reps)
    sin = _lane_tile(sin_ref[...], reps)
    u_ref[...] = proj[:, :s5w]
    q = _rope(proj[:, s5w:s5w + qkw], cos, sin)
    k = _rope(proj[:, s5w + qkw:s5w + 2 * qkw], cos, sin)
    v = proj[:, s5w + 2 * qkw:]
    q_ref[...] = (q * (HEAD_DIM ** -0.5 * LOG2E)).astype(BF16)
    k_ref[...] = k
    v_ref[...] = v
    kb_ref[...] = k.astype(BF16)
    vb_ref[...] = v.astype(BF16)


def _inproj_ab(x, g, w_bf, cos_tab, sin_tab, tm, s5w, qkw):
    n, d = x.shape
    wn = w_bf.shape[1]
    vw = wn - s5w - 2 * qkw
    period = cos_tab.shape[0] // tm
    row = lambda i: (i, 0)
    const = lambda i: (0, 0)
    tab = lambda i: (i % period, 0)
    return pl.pallas_call(
        functools.partial(_inproj_ab_kernel, s5w=s5w, qkw=qkw),
        out_shape=(jax.ShapeDtypeStruct((n, s5w), F32), jax.ShapeDtypeStruct((n, qkw), BF16),
                   jax.ShapeDtypeStruct((n, qkw), F32), jax.ShapeDtypeStruct((n, vw), F32),
                   jax.ShapeDtypeStruct((n, qkw), BF16), jax.ShapeDtypeStruct((n, vw), BF16)),
        grid=(n // tm,),
        in_specs=[pl.BlockSpec((tm, d), row), pl.BlockSpec((1, d), const),
                  pl.BlockSpec((d, wn), const), pl.BlockSpec((tm, LANES), tab),
                  pl.BlockSpec((tm, LANES), tab)],
        out_specs=(pl.BlockSpec((tm, s5w), row), pl.BlockSpec((tm, qkw), row),
                   pl.BlockSpec((tm, qkw), row), pl.BlockSpec((tm, vw), row),
                   pl.BlockSpec((tm, qkw), row), pl.BlockSpec((tm, vw), row)),
        compiler_params=_cparams(1),
        name="inproj_ab",
    )(x, g, w_bf, cos_tab, sin_tab)


def _diff_lambda(lamp, lam_init):
    s1 = jnp.sum(lamp[0:1] * lamp[1:2], axis=1, keepdims=True)
    s2 = jnp.sum(lamp[2:3] * lamp[3:4], axis=1, keepdims=True)
    return jnp.exp(s1) - jnp.exp(s2) + lam_init


def _subln(o, gain, lam_init):
    ms = jnp.mean(o * o, axis=-1, keepdims=True)
    return o * lax.rsqrt(ms + SUBLN_EPS) * gain * (1.0 - lam_init)


ONES_ROWS = 16

def _attn_prompt_kernel(lamp_ref, q_ref, k_ref, v_ref, gain_ref, o_ref,
                        vt_ref, m_ref, acc_ref, *, tq, tk, lam_init):
    qi = pl.program_id(2)
    seq, vw = v_ref.shape
    per_q = tq // tk

    @pl.when(qi == 0)
    def _():
        for c in range(seq // tk):
            blk = v_ref[c * tk:(c + 1) * tk, :].astype(F32)
            vt_ref[c, :vw, :] = blk.T.astype(BF16)
            vt_ref[c, vw:, :] = jnp.ones((ONES_ROWS, tk), BF16)

    q = q_ref[...]
    lane = lax.broadcasted_iota(jnp.int32, q.shape, 1)
    zero = jnp.zeros_like(q)
    qs = (jnp.where(lane < HEAD_DIM, q, zero), jnp.where(lane >= HEAD_DIM, q, zero))
    m_ref[...] = jnp.full(m_ref.shape, NEG_BIG, F32)
    acc_ref[...] = jnp.zeros(acc_ref.shape, F32)

    def chunk(c, diag):
        off = pl.multiple_of(c * tk, tk)
        kc = k_ref[pl.ds(off, tk), :]
        vc = vt_ref[c]
        def scores(j):
            st = lax.dot_general(kc, qs[j], (((1,), (1,)), ((), ())), preferred_element_type=F32)
            if diag is not None:
                key = lax.broadcasted_iota(jnp.int32, st.shape, 0) + diag * tk
                qry = lax.broadcasted_iota(jnp.int32, st.shape, 1)
                st = jnp.where(key <= qry, st, -jnp.inf)
            return st

        def softmax(j, st):
            m_prev = m_ref[j]
            m_new = jnp.maximum(m_prev, jnp.max(st, axis=0, keepdims=True))
            m_ref[j] = m_new
            return jnp.exp2(m_prev - m_new), jnp.exp2(st - m_new).astype(BF16)

        def accumulate(j, alpha, p):
            pv = jnp.dot(vc, p, preferred_element_type=F32)
            acc_ref[j] = alpha * acc_ref[j] + pv

        alpha0, p0 = softmax(0, scores(0))
        st1 = scores(1)
        accumulate(0, alpha0, p0)
        alpha1, p1 = softmax(1, st1)
        accumulate(1, alpha1, p1)

    def body(c, carry):
        chunk(c, None)
        return carry

    lax.fori_loop(0, qi * per_q, body, 0)
    for d in range(per_q):
        chunk(qi * per_q + d, d)

    lam = _diff_lambda(lamp_ref[...], lam_init)
    a0 = acc_ref[0]
    a1 = acc_ref[1]
    ot = a0[:vw] / a0[vw:vw + 1] - lam * (a1[:vw] / a1[vw:vw + 1])
    o_ref[...] = _subln(ot.T, gain_ref[...], lam_init).astype(o_ref.dtype)


def _attn_prompt(q_bf, k_bf, v_bf, lamp, gain, bsz, seq, tq, tk, lam_init):
    n, qkw = q_bf.shape
    vw_total = v_bf.shape[1]
    heads = qkw // (2 * HEAD_DIM)
    vw = vw_total // heads
    nq = seq // tq
    return pl.pallas_call(
        functools.partial(_attn_prompt_kernel, tq=tq, tk=tk, lam_init=lam_init),
        out_shape=jax.ShapeDtypeStruct((n, vw_total), BF16),
        grid=(bsz, heads, nq),
        in_specs=[pl.BlockSpec((4, HEAD_DIM), lambda b, h, i: (0, 0)),
                  pl.BlockSpec((tq, 2 * HEAD_DIM), lambda b, h, i: (b * nq + i, h)),
                  pl.BlockSpec((seq, 2 * HEAD_DIM), lambda b, h, i: (b, h)),
                  pl.BlockSpec((seq, vw), lambda b, h, i: (b, h)),
                  pl.BlockSpec((1, vw), lambda b, h, i: (0, 0))],
        out_specs=pl.BlockSpec((tq, vw), lambda b, h, i: (b * nq + i, h)),
        scratch_shapes=[pltpu.VMEM((seq // tk, vw + ONES_ROWS, tk), BF16),
                        pltpu.VMEM((2, 1, tq), F32),
                        pltpu.VMEM((2, vw + ONES_ROWS, tq), F32)],
        compiler_params=_cparams(3),
        name="attn_prompt",
    )(lamp, q_bf, k_bf, v_bf, gain)


def _attn_decode_kernel(pt_ref, lamp_ref, q_ref, kself_ref, vself_ref, gain_ref, *rest,
                        pages, lam_init, vw):
    k_refs = rest[:pages]
    v_refs = rest[pages:2 * pages]
    o_ref = rest[2 * pages]
    m_ref, l_ref, acc_ref = rest[2 * pages + 1:]
    j = pl.program_id(1)
    nh2 = m_ref.shape[0]
    width = q_ref.shape[1]

    @pl.when(j == 0)
    def _():
        m_ref[...] = jnp.full(m_ref.shape, NEG_BIG, F32)
        l_ref[...] = jnp.zeros(l_ref.shape, F32)
        acc_ref[...] = jnp.zeros(acc_ref.shape, F32)

    rowh = lax.broadcasted_iota(jnp.int32, (nh2, width), 0)
    laneh = lax.broadcasted_iota(jnp.int32, (nh2, width), 1) // HEAD_DIM
    q8 = jnp.where(rowh == laneh, jnp.broadcast_to(q_ref[...], (nh2, width)), 0.0)
    q8b = q8.astype(BF16)

    s_parts = [jnp.dot(q8b, k_refs[i][...].astype(BF16), preferred_element_type=F32)
               for i in range(pages)]
    s = jnp.concatenate(s_parts, axis=1)
    m_prev = m_ref[...]
    m_new = jnp.maximum(m_prev, jnp.max(s, axis=1, keepdims=True))
    alpha = jnp.exp2(m_prev - m_new)
    p = jnp.exp2(s - _lane_tile(m_new, s.shape[1] // LANES))
    l_ref[...] = alpha * l_ref[...] + jnp.sum(p, axis=1, keepdims=True)
    ps = k_refs[0].shape[1]
    nvh = width // vw
    pb = p.astype(BF16)
    pv = None
    for i in range(pages):
        cols = [jnp.dot(pb[:, i * ps:(i + 1) * ps],
                        v_refs[i][pl.ds(h, ps, stride=nvh), :].astype(BF16),
                        preferred_element_type=F32) for h in range(nvh)]
        part = jnp.concatenate(cols, axis=1)
        pv = part if pv is None else pv + part
    acc_ref[...] = _lane_tile(alpha, width // LANES) * acc_ref[...] + pv
    m_ref[...] = m_new

    @pl.when(j == pl.num_programs(1) - 1)
    def _():
        s_self = jnp.sum(q8 * kself_ref[...], axis=1, keepdims=True)
        m_old = m_ref[...]
        m_fin = jnp.maximum(m_old, s_self)
        a_fin = jnp.exp2(m_old - m_fin)
        p_self = jnp.exp2(s_self - m_fin)
        l_fin = a_fin * l_ref[...] + p_self
        acc = (_lane_tile(a_fin, width // LANES) * acc_ref[...]
               + _lane_tile(p_self, width // LANES) * vself_ref[...])
        o8 = acc / _lane_tile(l_fin, width // LANES)
        lam = _diff_lambda(lamp_ref[...], lam_init)
        vhead = lax.broadcasted_iota(jnp.int32, (nh2, width), 1) // vw
        sign = jnp.where((rowh & 1) == 0, 1.0, -lam)
        coef = jnp.where(vhead == rowh // 2, sign, 0.0)
        o = jnp.sum(o8 * coef, axis=0, keepdims=True)
        gain = gain_ref[...]
        outs = [_subln(o[:, h * vw:(h + 1) * vw], gain, lam_init) for h in range(width // vw)]
        o_ref[...] = jnp.concatenate(outs, axis=1)


def _attn_decode(q, k_self, v_self, cache_kt, cache_v2, page_table, lamp, gain, pages, lam_init):
    nseq, width = q.shape
    npages = page_table.shape[1]
    ps = cache_kt.shape[2]
    vw = gain.shape[1]
    nh2 = width // HEAD_DIM
    pt = page_table.reshape(-1)
    q3, k3, v3 = (a.reshape(nseq, 1, width) for a in (q, k_self, v_self))
    seq_spec = pl.BlockSpec((None, 1, width), lambda b, j, pt: (b, 0, 0))

    def page_spec(i, shape):
        return pl.BlockSpec((None,) + shape,
                            lambda b, j, pt: (pt[b * npages + j * pages + i], 0, 0))

    grid_spec = pltpu.PrefetchScalarGridSpec(
        num_scalar_prefetch=1,
        grid=(nseq, npages // pages),
        in_specs=[pl.BlockSpec((4, HEAD_DIM), lambda b, j, pt: (0, 0)), seq_spec, seq_spec, seq_spec,
                  pl.BlockSpec((1, vw), lambda b, j, pt: (0, 0))]
                 + [page_spec(i, cache_kt.shape[1:]) for i in range(pages)]
                 + [page_spec(i, cache_v2.shape[1:]) for i in range(pages)],
        out_specs=seq_spec,
        scratch_shapes=[pltpu.VMEM((nh2, LANES), F32), pltpu.VMEM((nh2, LANES), F32),
                        pltpu.VMEM((nh2, width), F32)],
    )
    out = pl.pallas_call(
        functools.partial(_attn_decode_kernel, pages=pages, lam_init=lam_init, vw=vw),
        out_shape=jax.ShapeDtypeStruct((nseq, 1, width), F32),
        grid_spec=grid_spec,
        compiler_params=_cparams(2),
        name="attn_decode",
    )(pt, lamp, q3, k3, v3, gain, *([cache_kt] * pages), *([cache_v2] * pages))
    return out.reshape(nseq, width)


def _cmul(a, b):
    return a[0] * b[0] - a[1] * b[1], a[0] * b[1] + a[1] * b[0]


def _s5_discretize(a_re, a_im, log_dt, b_re, b_im):
    a_re, a_im = a_re.astype(F32), a_im.astype(F32)
    step = jnp.exp(log_dt.astype(F32))[:, None]
    llb = (a_re * step, a_im * step)
    lb = _s5_power(llb, 1.0)
    num = (lb[0] - 1.0, lb[1])
    den = a_re * a_re + a_im * a_im
    quo = ((num[0] * a_re + num[1] * a_im) / den, (num[1] * a_re - num[0] * a_im) / den)
    b_bar = _cmul((quo[0][:, :, None], quo[1][:, :, None]), (b_re.astype(F32), b_im.astype(F32)))
    return llb, b_bar


def _s5_power(llb, k):
    mag = jnp.exp(llb[0] * k)
    return mag * jnp.cos(llb[1] * k), mag * jnp.sin(llb[1] * k)


def _s5_prompt_tables(params, seq):
    a_re, a_im, log_dt, b_re, b_im, c_re, c_im, d_skip = params
    g, p = a_re.shape
    ch, t = S5_GROUP, S5_CHUNK
    llb, b_bar = _s5_discretize(a_re, a_im, log_dt, b_re, b_im)
    c = (c_re.astype(F32), c_im.astype(F32))
    lags = jnp.arange(t + 1, dtype=F32)[:, None, None]
    powk = _s5_power((llb[0][None], llb[1][None]), lags)
    cp = _cmul((c[0][None], c[1][None]),
               (powk[0][:, :, None, :], powk[1][:, :, None, :]))
    klag = (jnp.einsum('kgcp,gpd->gkcd', cp[0][:t], b_bar[0])
            - jnp.einsum('kgcp,gpd->gkcd', cp[1][:t], b_bar[1]))
    skip = d_skip.astype(F32).reshape(g, ch)[:, :, None] * jnp.eye(ch, dtype=F32)[None]
    klag = klag.at[:, 0].add(skip)
    idx = jnp.arange(t)
    sel = (idx[None, None, :] - idx[None, :, None] == idx[:, None, None]).astype(F32)
    m_mat = jnp.einsum('kst,gkcd->gsdtc', sel, klag).reshape(g, t * ch, t * ch)
    pw_rev = (powk[0][:t][::-1].transpose(1, 0, 2)[:, :, None, :],
              powk[1][:t][::-1].transpose(1, 0, 2)[:, :, None, :])
    bb_t = (b_bar[0].transpose(0, 2, 1)[:, None], b_bar[1].transpose(0, 2, 1)[:, None])
    e_c = _cmul(pw_rev, bb_t)
    e_re = e_c[0].reshape(g, t * ch, p)
    e_im = e_c[1].reshape(g, t * ch, p)
    e_mat = jnp.concatenate([e_re, e_im, e_im, e_re], axis=-1)
    f_re = cp[0][1:t + 1].transpose(1, 0, 2, 3).reshape(g, t * ch, p)
    f_im = cp[1][1:t + 1].transpose(1, 0, 2, 3).reshape(g, t * ch, p)
    f_mat = jnp.concatenate([f_re, -f_im], axis=-1).transpose(0, 2, 1)
    ks = (jnp.arange(1, SUBLANES + 1, dtype=F32) * t)[None, :, None]
    pw = _s5_power((llb[0][:, None, :], llb[1][:, None, :]), ks)
    a1 = jnp.concatenate([pw[0], pw[0]], axis=-1)
    a2 = jnp.concatenate([-pw[1], pw[1]], axis=-1)
    rows = jnp.arange(SUBLANES)[None, :, None]
    parts = []
    for d in S5_SCAN_STRIDES:
        parts.append(jnp.where(rows >= d, a1[:, d - 1:d, :], 0.0))
        parts.append(jnp.where(rows >= d, a2[:, d - 1:d, :], 0.0))
    tab = jnp.concatenate(parts + [a1, a2], axis=1)
    return m_mat.astype(BF16), e_mat.astype(BF16), f_mat.astype(BF16), tab


def _s5_lane_permutation():
    n = SUBLANES * SUBLANES * S5_GROUP
    idx = jnp.arange(n).reshape(SUBLANES, SUBLANES, S5_GROUP).transpose(1, 0, 2).reshape(n)
    return jax.nn.one_hot(idx, n, dtype=BF16)


def _s5_prompt_kernel(u_ref, perm_ref, e_ref, m_ref, f_ref, tab_ref, y_ref, hl_ref,
                      us_ref, ee_ref, h_ref, ycat_ref):
    t = S5_CHUNK
    ng = SUBLANES
    nsub = u_ref.shape[0] // t
    sw = tab_ref.shape[2]
    half_t = t // 2
    perm = perm_ref[...]
    for half in range(2):
        xs = [u_ref[pl.ds(half * half_t + s, nsub, stride=t), :].astype(BF16) for s in range(half_t)]
        uh = jnp.dot(jnp.concatenate(xs, axis=1), perm, preferred_element_type=F32).astype(BF16)
        for gl in range(ng):
            us_ref[gl, :, half * LANES:(half + 1) * LANES] = uh[:, gl * LANES:(gl + 1) * LANES]
    for gl in range(ng):
        ee_ref[gl] = jnp.dot(us_ref[gl], e_ref[gl], preferred_element_type=F32)
        h_ref[gl, :SUBLANES, :] = jnp.zeros((SUBLANES, sw), F32)

    def step(i, carry):
        row = pl.multiple_of(i * SUBLANES, SUBLANES)
        out = []
        for gl in range(ng):
            h = ee_ref[gl, pl.ds(row, SUBLANES), :sw]
            hs = ee_ref[gl, pl.ds(row, SUBLANES), sw:]
            for di, d in enumerate(S5_SCAN_STRIDES):
                t1 = tab_ref[gl, 2 * di * SUBLANES:(2 * di + 1) * SUBLANES, :]
                t2 = tab_ref[gl, (2 * di + 1) * SUBLANES:(2 * di + 2) * SUBLANES, :]
                sh = pltpu.roll(h, d, 0)
                shs = pltpu.roll(hs, d, 0)
                h, hs = h + t1 * sh + t2 * shs, hs + t1 * shs - t2 * sh
            nd = 2 * len(S5_SCAN_STRIDES)
            q1 = tab_ref[gl, nd * SUBLANES:(nd + 1) * SUBLANES, :]
            q2 = tab_ref[gl, (nd + 1) * SUBLANES:(nd + 2) * SUBLANES, :]
            cb, cbs = carry[2 * gl], carry[2 * gl + 1]
            hf = h + q1 * cb + q2 * cbs
            hfs = hs + q1 * cbs - q2 * cb
            h_ref[gl, pl.ds(SUBLANES + row, SUBLANES), :] = hf
            out.append(jnp.broadcast_to(hf[SUBLANES - 1:SUBLANES, :], hf.shape))
            out.append(jnp.broadcast_to(hfs[SUBLANES - 1:SUBLANES, :], hfs.shape))
        return tuple(out)

    zero = jnp.zeros((SUBLANES, sw), F32)
    carry = lax.fori_loop(0, nsub // SUBLANES, step, (zero,) * (2 * ng))
    for gl in range(ng):
        hl_ref[gl:gl + 1, :] = carry[2 * gl][0:1, :]
        hin = h_ref[gl, pl.ds(SUBLANES - 1, nsub), :].astype(BF16)
        y = (jnp.dot(us_ref[gl], m_ref[gl], preferred_element_type=F32)
             + jnp.dot(hin, f_ref[gl], preferred_element_type=F32))
        for half in range(2):
            ycat_ref[half, :, gl * LANES:(gl + 1) * LANES] = y[:, half * LANES:(half + 1) * LANES]
    for half in range(2):
        yc = ycat_ref[half]
        hi = yc.astype(BF16)
        lo = (yc - hi.astype(F32)).astype(BF16)
        z = (jnp.dot(hi, perm, preferred_element_type=F32)
             + jnp.dot(lo, perm, preferred_element_type=F32))
        for s in range(half_t):
            y_ref[pl.ds(half * half_t + s, nsub, stride=t), :] = z[:, s * LANES:(s + 1) * LANES]


def _s5_prompt(u, tables, bsz, seq):
    m_mat, e_mat, f_mat, tab = tables
    g = m_mat.shape[0]
    p = f_mat.shape[1] // 2
    t, ch = S5_CHUNK, S5_GROUP
    ng = LANES // ch
    assert ng == SUBLANES and t * ch == 2 * LANES
    n, w = u.shape
    nsub = seq // t
    perm = _s5_lane_permutation()
    npm = perm.shape[0]
    ntab = tab.shape[1]
    slab = lambda s, b: (s, 0, 0)
    y, hl = pl.pallas_call(
        _s5_prompt_kernel,
        out_shape=(jax.ShapeDtypeStruct((n, w), F32), jax.ShapeDtypeStruct((bsz, g, 2 * p), F32)),
        grid=(w // LANES, bsz),
        in_specs=[pl.BlockSpec((seq, LANES), lambda s, b: (b, s)),
                  pl.BlockSpec((npm, npm), lambda s, b: (0, 0)),
                  pl.BlockSpec((ng, t * ch, 4 * p), slab),
                  pl.BlockSpec((ng, t * ch, t * ch), slab),
                  pl.BlockSpec((ng, 2 * p, t * ch), slab),
                  pl.BlockSpec((ng, ntab, 2 * p), slab)],
        out_specs=(pl.BlockSpec((seq, LANES), lambda s, b: (b, s)),
                   pl.BlockSpec((None, ng, 2 * p), lambda s, b: (b, s, 0))),
        scratch_shapes=[pltpu.VMEM((ng, nsub, t * ch), BF16),
                        pltpu.VMEM((ng, nsub, 4 * p), F32),
                        pltpu.VMEM((ng, SUBLANES + nsub, 2 * p), F32),
                        pltpu.VMEM((2, nsub, ng * LANES), F32)],
        compiler_params=_cparams(2),
        name="s5_prompt",
    )(u, perm, e_mat, m_mat, f_mat, tab)
    return y, hl[..., :p], hl[..., p:]


def _s5_step_kernel(u_ref, h_ref, hs_ref, bt_ref, ct_ref, a1_ref, a2_ref, d_ref, y_ref, hn_ref):
    hp = lax.Precision.HIGHEST
    u = u_ref[...]
    bu = jnp.einsum('gbc,gcp->gbp', u, bt_ref[...], precision=hp, preferred_element_type=F32)
    hn = a1_ref[...] * h_ref[...] + a2_ref[...] * hs_ref[...] + bu
    hn_ref[...] = hn
    y = jnp.einsum('gbp,gpc->gbc', hn, ct_ref[...], precision=hp, preferred_element_type=F32)
    y_ref[...] = y + d_ref[...] * u


def _s5_step(u, h0_re, h0_im, params):
    a_re, a_im, log_dt, b_re, b_im, c_re, c_im, d_skip = params
    g, p = a_re.shape
    ch = S5_GROUP
    bsz = u.shape[0]
    llb, b_bar = _s5_discretize(a_re, a_im, log_dt, b_re, b_im)
    lam_bar = _s5_power(llb, 1.0)
    bt = jnp.concatenate([b_bar[0].transpose(0, 2, 1), b_bar[1].transpose(0, 2, 1)], axis=-1)
    ct = jnp.concatenate([c_re.astype(F32), -c_im.astype(F32)], axis=-1).transpose(0, 2, 1)
    a1 = jnp.concatenate([lam_bar[0]] * 2, axis=-1)[:, None, :]
    a2 = jnp.concatenate([-lam_bar[1], lam_bar[1]], axis=-1)[:, None, :]
    h = jnp.concatenate([h0_re, h0_im], axis=-1).transpose(1, 0, 2).astype(F32)
    hs = jnp.concatenate([h0_im, h0_re], axis=-1).transpose(1, 0, 2).astype(F32)
    ug = u.reshape(bsz, g, ch).transpose(1, 0, 2)
    dg = d_skip.astype(F32).reshape(g, 1, ch)
    y, hn = pl.pallas_call(
        _s5_step_kernel,
        out_shape=(jax.ShapeDtypeStruct((g, bsz, ch), F32), jax.ShapeDtypeStruct((g, bsz, 2 * p), F32)),
        compiler_params=pltpu.CompilerParams(vmem_limit_bytes=VMEM_LIMIT),
        name="s5_step",
    )(ug, h, hs, bt, ct, a1, a2, dg)
    hn = hn.transpose(1, 0, 2)
    return y.transpose(1, 0, 2).reshape(bsz, g * ch), hn[..., :p], hn[..., p:]


def _resident(a):
    zeros = (0,) * a.ndim
    return pl.BlockSpec(a.shape, lambda *_: zeros, pipeline_mode=pl.Buffered(1))


def _ffn_apply(x, ffn_refs, final_norm):
    g_ref, wg_ref, wu_ref, wd_ref, gf_ref = ffn_refs
    hn = _rms(x, g_ref[...], NORM_EPS).astype(BF16)
    hidden = wg_ref.shape[1]
    th = MXU_DIM if hidden % MXU_DIM == 0 else hidden
    acc = None
    for c in range(hidden // th):
        cols = slice(c * th, (c + 1) * th)
        gate = jnp.dot(hn, wg_ref[:, cols], preferred_element_type=F32)
        up = jnp.dot(hn, wu_ref[:, cols], preferred_element_type=F32)
        act = (jax.nn.silu(gate) * up).astype(BF16)
        part = jnp.dot(act, wd_ref[cols, :], preferred_element_type=F32)
        acc = part if acc is None else acc + part
    out = x + acc
    if final_norm:
        out = _rms(out, gf_ref[...], NORM_EPS)
    return out


def _ffn_specs(ffn_w):
    return [_resident(a) for a in ffn_w]


def _ffn_kernel(x_ref, *rest, final_norm):
    o_ref = rest[-1]
    o_ref[...] = _ffn_apply(x_ref[...], rest[:-1], final_norm)


def _ffn(x, ffn_w, tm, final_norm):
    n, d = x.shape
    return pl.pallas_call(
        functools.partial(_ffn_kernel, final_norm=final_norm),
        out_shape=jax.ShapeDtypeStruct((n, d), F32),
        grid=(n // tm,),
        in_specs=[pl.BlockSpec((tm, d), lambda i: (i, 0))] + _ffn_specs(ffn_w),
        out_specs=pl.BlockSpec((tm, d), lambda i: (i, 0)),
        compiler_params=_cparams(1),
        name="ffn",
    )(x, *ffn_w)


def _post_ab_kernel(x_ref, y_ref, a_ref, wglu_ref, bglu_ref, wo_ref, *rest, s5w, final_norm):
    o_ref = rest[-1]
    z = jax.nn.gelu(y_ref[...])
    gl = jnp.dot(z.astype(BF16), wglu_ref[...], preferred_element_type=F32) + bglu_ref[...]
    s5o = (z * jax.nn.sigmoid(gl)).astype(BF16)
    mix = (jnp.dot(s5o, wo_ref[:s5w, :], preferred_element_type=F32)
           + jnp.dot(a_ref[...].astype(BF16), wo_ref[s5w:, :], preferred_element_type=F32))
    o_ref[...] = _ffn_apply(x_ref[...] + mix, rest[:-1], final_norm)


def _post_ab(x, y, attn, wglu_bf, bglu, wo_bf, ffn_w, tm, final_norm):
    n, d = x.shape
    s5w = y.shape[1]
    aw = attn.shape[1]
    row = lambda i: (i, 0)
    return pl.pallas_call(
        functools.partial(_post_ab_kernel, s5w=s5w, final_norm=final_norm),
        out_shape=jax.ShapeDtypeStruct((n, d), F32),
        grid=(n // tm,),
        in_specs=[pl.BlockSpec((tm, d), row), pl.BlockSpec((tm, s5w), row), pl.BlockSpec((tm, aw), row),
                  _resident(wglu_bf), _resident(bglu), _resident(wo_bf)] + _ffn_specs(ffn_w),
        out_specs=pl.BlockSpec((tm, d), row),
        compiler_params=_cparams(1),
        name="post_ab_ffn",
    )(x, y, attn, wglu_bf, bglu, wo_bf, *ffn_w)


def _lru_gates(xc, wg_ref, ba, bx, lam_param):
    xcb = xc.astype(BF16)
    halves = [jnp.dot(xcb[:, h * MXU_DIM:(h + 1) * MXU_DIM], wg_ref[h], preferred_element_type=F32)
              for h in range(wg_ref.shape[0])]
    r_pre = jnp.concatenate([hv[:, :MXU_DIM] for hv in halves], axis=1)
    i_pre = jnp.concatenate([hv[:, MXU_DIM:] for hv in halves], axis=1)
    r = jax.nn.sigmoid(r_pre + ba)
    i = jax.nn.sigmoid(i_pre + bx)
    log_a = -LRU_C * r * jax.nn.softplus(-lam_param)
    a = jnp.exp(log_a)
    b = jnp.sqrt(-jnp.tanh(log_a) * (a * a + 1.0)) * (i * xc)
    return a, b


def _pool_project(pooled, pw_ref, scale):
    pb = pooled.astype(BF16)
    halves = [jnp.dot(pb[:, h * MXU_DIM:(h + 1) * MXU_DIM], pw_ref[h], preferred_element_type=F32)
              for h in range(pw_ref.shape[0])]
    return jnp.concatenate(halves, axis=1) * scale


def _cd_prompt_kernel(x_ref, g_ref, win_ref, cw_ref, cb_ref, wg_ref, ba_ref, bx_ref, lam_ref,
                      pw_ref, psc_ref, wo_ref, fg_ref, fwg_ref, fwu_ref, fwd_ref, fgf_ref,
                      o_ref, conv_ref, lru_ref, pool_ref,
                      xl_ref, xp_ref, a_ref, b_ref, hs_ref, carry_ref, *, tm, w, final_norm):
    ti = pl.program_id(1)
    hist = POOL_BUF + 1

    @pl.when(ti == 0)
    def _():
        xl_ref[:hist, :] = jnp.zeros((hist, w), F32)
        xp_ref[:hist, :] = jnp.zeros((hist, w), F32)
        carry_ref[...] = jnp.zeros(carry_ref.shape, F32)

    x = x_ref[...]
    hn = _rms(x, g_ref[...], NORM_EPS).astype(BF16)
    proj = jnp.dot(hn, win_ref[...], preferred_element_type=F32)
    gate = proj[:, :w]
    xl_ref[hist:, :] = proj[:, w:2 * w]
    xp_ref[hist:, :] = proj[:, 2 * w:]

    xc = cb_ref[...] + xl_ref[pl.ds(hist - (CONV_WIDTH - 1), tm), :] * cw_ref[0:1, :]
    for j in range(1, CONV_WIDTH):
        xc = xc + xl_ref[pl.ds(hist - (CONV_WIDTH - 1) + j, tm), :] * cw_ref[j:j + 1, :]

    a, b = _lru_gates(xc, wg_ref, ba_ref[...], bx_ref[...], lam_ref[...])
    a_ref[...] = a
    b_ref[...] = b

    def group(gi, carry):
        row = pl.multiple_of(gi * SUBLANES, SUBLANES)
        ag = a_ref[pl.ds(row, SUBLANES), :]
        bg = b_ref[pl.ds(row, SUBLANES), :]
        ridx = lax.broadcasted_iota(jnp.int32, ag.shape, 0)
        for d in (1, 2, 4):
            a_sh = jnp.where(ridx >= d, pltpu.roll(ag, d, 0), 1.0)
            b_sh = jnp.where(ridx >= d, pltpu.roll(bg, d, 0), 0.0)
            bg = ag * b_sh + bg
            ag = ag * a_sh
        h = bg + ag * carry
        hs_ref[pl.ds(row, SUBLANES), :] = h
        return jnp.broadcast_to(h[SUBLANES - 1:SUBLANES, :], h.shape)

    carry = lax.fori_loop(0, tm // SUBLANES, group, jnp.broadcast_to(carry_ref[...], (SUBLANES, w)))
    carry_ref[...] = carry[0:1, :]
    lru_out = jax.nn.gelu(gate) * hs_ref[...]

    pg = w // len(POOL_WINDOWS)
    pos = ti * tm + lax.broadcasted_iota(jnp.int32, (tm, pg), 0)
    pooled = []
    for gidx, win in enumerate(POOL_WINDOWS):
        cols = slice(gidx * pg, (gidx + 1) * pg)
        cur = xp_ref[pl.ds(hist, tm), cols]
        tot = cur
        for kk in range(1, win):
            tot = tot + xp_ref[pl.ds(hist - kk, tm), cols]
        cnt = jnp.minimum(pos + 1, win).astype(F32)
        pooled.append(tot / cnt - cur)
    pool_out = _pool_project(jnp.concatenate(pooled, axis=1), pw_ref, psc_ref[...])

    mix = (jnp.dot(lru_out.astype(BF16), wo_ref[:w, :], preferred_element_type=F32)
           + jnp.dot(pool_out.astype(BF16), wo_ref[w:, :], preferred_element_type=F32))
    o_ref[...] = _ffn_apply(x + mix, (fg_ref, fwg_ref, fwu_ref, fwd_ref, fgf_ref), final_norm)

    conv_ref[...] = xl_ref[pl.ds(hist + tm - (CONV_WIDTH - 1), CONV_WIDTH - 1), :]
    lru_ref[...] = carry_ref[...]
    pool_ref[...] = xp_ref[pl.ds(hist + tm - POOL_BUF, POOL_BUF), :]
    xl_ref[:hist, :] = xl_ref[pl.ds(tm, hist), :]
    xp_ref[:hist, :] = xp_ref[pl.ds(tm, hist), :]


def _cd_prompt(x, g, cdw, ffn_w, bsz, seq, tm, final_norm):
    n, d = x.shape
    w = cdw[1].shape[1]
    nt = seq // tm
    hist = POOL_BUF + 1
    row = lambda b, i: (b * nt + i, 0)
    st = lambda b, i: (b, 0, 0)
    return pl.pallas_call(
        functools.partial(_cd_prompt_kernel, tm=tm, w=w, final_norm=final_norm),
        out_shape=(jax.ShapeDtypeStruct((n, d), F32),
                   jax.ShapeDtypeStruct((bsz, CONV_WIDTH - 1, w), F32),
                   jax.ShapeDtypeStruct((bsz, 1, w), F32),
                   jax.ShapeDtypeStruct((bsz, POOL_BUF, w), F32)),
        grid=(bsz, nt),
        in_specs=[pl.BlockSpec((tm, d), row), _resident(g)] + [_resident(a) for a in cdw]
                 + _ffn_specs(ffn_w),
        out_specs=(pl.BlockSpec((tm, d), row),
                   pl.BlockSpec((None, CONV_WIDTH - 1, w), st),
                   pl.BlockSpec((None, 1, w), st),
                   pl.BlockSpec((None, POOL_BUF, w), st)),
        scratch_shapes=[pltpu.VMEM((hist + tm, w), F32), pltpu.VMEM((hist + tm, w), F32),
                        pltpu.VMEM((tm, w), F32), pltpu.VMEM((tm, w), F32), pltpu.VMEM((tm, w), F32),
                        pltpu.VMEM((1, w), F32)],
        compiler_params=_cparams(2),
        name="cd_prompt_ffn",
    )(x, g, *cdw, *ffn_w)


def _cd_step_kernel(x_ref, g_ref, win_ref, cw_ref, cb_ref, wg_ref, ba_ref, bx_ref, lam_ref,
                    pw_ref, psc_ref, wo_ref, sc_ref, sl_ref, sp_ref,
                    o_ref, conv_ref, lru_ref, pool_ref, *, w):
    x = x_ref[...]
    hn = _rms(x, g_ref[...], NORM_EPS).astype(BF16)
    proj = jnp.dot(hn, win_ref[...], preferred_element_type=F32)
    gate = proj[:, :w]
    xl = proj[:, w:2 * w]
    xp = proj[:, 2 * w:]
    xc = cb_ref[...] + xl * cw_ref[CONV_WIDTH - 1:CONV_WIDTH, :]
    for j in range(CONV_WIDTH - 1):
        xc = xc + sc_ref[j] * cw_ref[j:j + 1, :]
    a, b = _lru_gates(xc, wg_ref, ba_ref[...], bx_ref[...], lam_ref[...])
    h = a * sl_ref[...] + b
    lru_out = jax.nn.gelu(gate) * h

    pg = w // len(POOL_WINDOWS)
    pooled = []
    for gidx, win in enumerate(POOL_WINDOWS):
        cols = slice(gidx * pg, (gidx + 1) * pg)
        cur = xp[:, cols]
        tot = cur
        for kk in range(1, win):
            tot = tot + sp_ref[POOL_BUF - kk][:, cols]
        pooled.append(tot / float(win) - cur)
    pool_out = _pool_project(jnp.concatenate(pooled, axis=1), pw_ref, psc_ref[...])

    mix = (jnp.dot(lru_out.astype(BF16), wo_ref[:w, :], preferred_element_type=F32)
           + jnp.dot(pool_out.astype(BF16), wo_ref[w:, :], preferred_element_type=F32))
    o_ref[...] = x + mix
    for j in range(CONV_WIDTH - 2):
        conv_ref[j] = sc_ref[j + 1]
    conv_ref[CONV_WIDTH - 2] = xl
    lru_ref[...] = h
    for j in range(POOL_BUF - 1):
        pool_ref[j] = sp_ref[j + 1]
    pool_ref[POOL_BUF - 1] = xp


def _cd_step(x, g, cdw, conv_buf, lru_h0, pool_buf):
    n, d = x.shape
    win_bf, cw, cb, wg_bf, ba, bx, lam, pw_bf, psc, wo_bf = cdw
    w = cw.shape[1]
    sc = conv_buf.transpose(1, 0, 2)
    sp = pool_buf.transpose(1, 0, 2)
    o, conv_n, lru_n, pool_n = pl.pallas_call(
        functools.partial(_cd_step_kernel, w=w),
        out_shape=(jax.ShapeDtypeStruct((n, d), F32),
                   jax.ShapeDtypeStruct((CONV_WIDTH - 1, n, w), F32),
                   jax.ShapeDtypeStruct((n, w), F32),
                   jax.ShapeDtypeStruct((POOL_BUF, n, w), F32)),
        compiler_params=pltpu.CompilerParams(vmem_limit_bytes=VMEM_LIMIT),
        name="cd_step",
    )(x, g, win_bf, cw, cb, wg_bf, ba, bx, lam, pw_bf, psc, wo_bf, sc, lru_h0, sp)
    return o, conv_n.transpose(1, 0, 2), lru_n, pool_n.transpose(1, 0, 2)


def _block_diag_tiles(w, tile):
    nb, c, _ = w.shape
    per = tile // c
    out = jnp.zeros((nb // per, tile, tile), w.dtype)
    for i in range(nb):
        t, o = divmod(i, per)
        out = out.at[t, o * c:(o + 1) * c, o * c:(o + 1) * c].set(w[i])
    return out


def _rope_tables(pos, rows):
    half = HEAD_DIM // 2
    inv = ROPE_THETA ** (-jnp.arange(half, dtype=F32) / half)
    ang = pos.astype(F32)[:, None] * inv[None, :]
    cos = jnp.cos(ang)
    sin = jnp.sin(ang)
    reps = LANES // HEAD_DIM
    cos_t = jnp.tile(jnp.concatenate([cos, cos], axis=1), (1, reps))
    sin_t = jnp.tile(jnp.concatenate([-sin, sin], axis=1), (1, reps))
    if cos_t.shape[0] != rows:
        cos_t = jnp.broadcast_to(cos_t, (rows, LANES))
        sin_t = jnp.broadcast_to(sin_t, (rows, LANES))
    return cos_t, sin_t


def _row_tile(n, target):
    t = min(n, target)
    while n % t:
        t //= 2
    return t


def kernel(x_prompt, x_sample, cache_k, cache_v, page_table, state_s5_re, state_s5_im, state_conv, state_lru, state_pool, norm_mix, norm_ffn, norm_final, w_in_ab, w_out_ab, s5_a_re, s5_a_im, s5_log_dt, s5_b_re, s5_b_im, s5_c_re, s5_c_im, s5_d, s5_w_glu, s5_b_glu, diff_lq1, diff_lk1, diff_lq2, diff_lk2, diff_subln, w_in_cd, w_out_cd, conv_w, conv_b, lru_wa, lru_ba, lru_wx, lru_bx, lru_lambda, pool_w, pool_scale, ffn_w_gate, ffn_w_up, ffn_w_down):
    bsz, seq, d = x_prompt.shape
    nseq, dec_seq, _ = x_sample.shape
    assert dec_seq == 1
    depth = norm_mix.shape[0]
    npages = page_table.shape[1]
    page_size = cache_k.shape[2]
    past_len = npages * page_size
    assert past_len >= POOL_BUF
    s5w = s5_d.shape[1]
    qkw = cache_k.shape[3] * cache_k.shape[4]
    vw_total = cache_v.shape[3] * cache_v.shape[4]
    heads = cache_v.shape[3]
    n_p = bsz * seq
    row2 = lambda a: a.reshape(1, -1)

    xp = x_prompt.reshape(n_p, d)
    xs = x_sample.reshape(nseq, d)
    tm_p = _row_tile(seq, 512)
    cos_p, sin_p = _rope_tables(jnp.arange(seq, dtype=jnp.int32), seq)
    cos_s, sin_s = _rope_tables(jnp.full((1,), past_len, jnp.int32), nseq)

    outs_p = {'k': [], 'v': [], 're': [], 'im': [], 'conv': [], 'lru': [], 'pool': []}
    outs_s = {'k': [], 'v': [], 're': [], 'im': [], 'conv': [], 'lru': [], 'pool': []}

    for l in range(depth):
        j = l // 2
        g_mix = row2(norm_mix[l])
        last = l == depth - 1
        ffn_w = (row2(norm_ffn[l]), ffn_w_gate[l].astype(BF16), ffn_w_up[l].astype(BF16),
                 ffn_w_down[l].astype(BF16), row2(norm_final))
        if l % 2 == 0:
            lam_init = 0.8 - 0.6 * math.exp(-0.3 * l)
            w_in = w_in_ab[j].astype(BF16)
            w_out = w_out_ab[j].astype(BF16)
            w_glu = s5_w_glu[j].astype(BF16)
            b_glu = row2(s5_b_glu[j])
            lamp = jnp.stack([diff_lq1[j], diff_lk1[j], diff_lq2[j], diff_lk2[j]]).astype(F32)
            gain = row2(diff_subln[j])
            s5p = (s5_a_re[j], s5_a_im[j], s5_log_dt[j], s5_b_re[j], s5_b_im[j],
                   s5_c_re[j], s5_c_im[j], s5_d[j])
            u, q_bf, k, v, k_bf, v_bf = _inproj_ab(xp, g_mix, w_in, cos_p, sin_p, tm_p, s5w, qkw)
            attn = _attn_prompt(q_bf, k_bf, v_bf, lamp, gain, bsz, seq, _row_tile(seq, 1024), 1024,
                                lam_init)
            y, h_re, h_im = _s5_prompt(u, _s5_prompt_tables(s5p, seq), bsz, seq)
            xp = _post_ab(xp, y, attn, w_glu, b_glu, w_out, ffn_w, tm_p, last)
            outs_p['k'].append(k.reshape(bsz, seq, qkw // HEAD_DIM, HEAD_DIM))
            outs_p['v'].append(v.reshape(bsz, seq, heads, vw_total // heads))
            outs_p['re'].append(h_re)
            outs_p['im'].append(h_im)
            u, q_bf, k, v, _, _ = _inproj_ab(xs, g_mix, w_in, cos_s, sin_s, nseq, s5w, qkw)
            ck = jnp.transpose(cache_k[j], (0, 2, 3, 1)).reshape(-1, qkw, page_size)
            cv = cache_v[j].reshape(-1, page_size * heads, vw_total // heads)
            attn = _attn_decode(q_bf.astype(F32), k, v, ck, cv, page_table, lamp, gain,
                                math.gcd(npages, 16), lam_init)
            y, h_re, h_im = _s5_step(u, state_s5_re[j], state_s5_im[j], s5p)
            xs = _post_ab(xs, y, attn, w_glu, b_glu, w_out, ffn_w, nseq, last)
            outs_s['k'].append(k.reshape(nseq, 1, qkw // HEAD_DIM, HEAD_DIM))
            outs_s['v'].append(v.reshape(nseq, 1, heads, vw_total // heads))
            outs_s['re'].append(h_re)
            outs_s['im'].append(h_im)
        else:
            w = conv_w.shape[2]
            wg = jnp.concatenate([_block_diag_tiles(lru_wa[j], MXU_DIM),
                                  _block_diag_tiles(lru_wx[j], MXU_DIM)], axis=2).astype(BF16)
            cdw = (w_in_cd[j].astype(BF16), conv_w[j], row2(conv_b[j]), wg, row2(lru_ba[j]),
                   row2(lru_bx[j]), row2(lru_lambda[j]),
                   _block_diag_tiles(pool_w[j], MXU_DIM).astype(BF16), row2(pool_scale[j]),
                   w_out_cd[j].astype(BF16))
            xp, c_new, h_new, p_new = _cd_prompt(xp, g_mix, cdw, ffn_w, bsz, seq, tm_p, last)
            outs_p['conv'].append(c_new)
            outs_p['lru'].append(h_new.reshape(bsz, w))
            outs_p['pool'].append(p_new)
            xs, c_new, h_new, p_new = _cd_step(xs, g_mix, cdw, state_conv[j], state_lru[j], state_pool[j])
            outs_s['conv'].append(c_new)
            outs_s['lru'].append(h_new)
            outs_s['pool'].append(p_new)
            xs = _ffn(xs, ffn_w, nseq, last)

    sp = {name: jnp.stack(vals) for name, vals in outs_p.items()}
    ss = {name: jnp.stack(vals) for name, vals in outs_s.items()}
    y_prompt = xp.reshape(bsz, seq, d)
    y_sample = xs.reshape(nseq, 1, d)
    return (y_prompt, y_sample, sp['k'], sp['v'], ss['k'], ss['v'], sp['re'], sp['im'], ss['re'], ss['im'],
            sp['conv'], ss['conv'], sp['lru'], ss['lru'], sp['pool'], ss['pool'])
```

```python
import functools
import math

import jax
import jax.numpy as jnp
from jax import lax
from jax.experimental import pallas as pl
from jax.experimental.pallas import tpu as pltpu

F32 = jnp.float32
BF16 = jnp.bfloat16

HEAD_DIM = 64
ROPE_THETA = 10000.0
NORM_EPS = 1e-6
SUBLN_EPS = 1e-5
S5_GROUP = 16
S5_CHUNK = 16
S5_SCAN_STRIDES = (1, 2, 4)
LRU_C = 8.0
CONV_WIDTH = 4
POOL_WINDOWS = (2, 4, 8, 16)
POOL_BUF = max(POOL_WINDOWS) - 1

LANES = 128
SUBLANES = 8
MXU_DIM = 256
VMEM_LIMIT = 56 * 1024 * 1024

NEG_BIG = -1e30
LOG2E = math.log2(math.e)


def _cparams(n_axes, flags=None):
    return pltpu.CompilerParams(dimension_semantics=("arbitrary",) * n_axes,
                                vmem_limit_bytes=VMEM_LIMIT, flags=flags)


def _rms(x, g, eps):
    ms = jnp.mean(x * x, axis=-1, keepdims=True)
    return x * lax.rsqrt(ms + eps) * g


def _lane_tile(x, reps):
    return jnp.concatenate([x] * reps, axis=1)


def _rope(x, cos, sin_signed):
    width = x.shape[1]
    lane = lax.broadcasted_iota(jnp.int32, x.shape, 1)
    half = HEAD_DIM // 2
    rot = jnp.where((lane & half) == 0, pltpu.roll(x, width - half, 1), pltpu.roll(x, half, 1))
    return x * cos + rot * sin_signed


def _inproj_ab_kernel(x_ref, g_ref, w_ref, cos_ref, sin_ref,
                      u_ref, q_ref, k_ref, v_ref, kb_ref, vb_ref, *, s5w, qkw):
    h = _rms(x_ref[...], g_ref[...], NORM_EPS).astype(BF16)
    proj = jnp.dot(h, w_ref[...], preferred_element_type=F32)
    reps = qkw // LANES
    cos = _lane_tile(cos_ref[...], reps)
    sin = _lane_tile(sin_ref[...], reps)
    u_ref[...] = proj[:, :s5w]
    q = _rope(proj[:, s5w:s5w + qkw], cos, sin)
    k = _rope(proj[:, s5w + qkw:s5w + 2 * qkw], cos, sin)
    v = proj[:, s5w + 2 * qkw:]
    q_ref[...] = (q * (HEAD_DIM ** -0.5 * LOG2E)).astype(BF16)
    k_ref[...] = k
    nvh = v.shape[1] // LANES
    for hh in range(nvh):
        v_ref[pl.ds(hh, v.shape[0], stride=nvh), :] = v[:, hh * LANES:(hh + 1) * LANES]
    kb_ref[...] = k.astype(BF16)
    vb_ref[...] = v.astype(BF16)


def _inproj_ab(x, g, w_bf, cos_tab, sin_tab, tm, s5w, qkw):
    n, d = x.shape
    wn = w_bf.shape[1]
    vw = wn - s5w - 2 * qkw
    nvh = vw // LANES
    period = cos_tab.shape[0] // tm
    row = lambda i: (i, 0)
    const = lambda i: (0, 0)
    tab = lambda i: (i % period, 0)
    return pl.pallas_call(
        functools.partial(_inproj_ab_kernel, s5w=s5w, qkw=qkw),
        out_shape=(jax.ShapeDtypeStruct((n, s5w), F32), jax.ShapeDtypeStruct((n, qkw), BF16),
                   jax.ShapeDtypeStruct((n, qkw), F32), jax.ShapeDtypeStruct((n * nvh, LANES), F32),
                   jax.ShapeDtypeStruct((n, qkw), BF16), jax.ShapeDtypeStruct((n, vw), BF16)),
        grid=(n // tm,),
        in_specs=[pl.BlockSpec((tm, d), row), pl.BlockSpec((1, d), const),
                  pl.BlockSpec((d, wn), const), pl.BlockSpec((tm, LANES), tab),
                  pl.BlockSpec((tm, LANES), tab)],
        out_specs=(pl.BlockSpec((tm, s5w), row), pl.BlockSpec((tm, qkw), row),
                   pl.BlockSpec((tm, qkw), row), pl.BlockSpec((tm * nvh, LANES), row),
                   pl.BlockSpec((tm, qkw), row), pl.BlockSpec((tm, vw), row)),
        compiler_params=_cparams(1),
        name="inproj_ab",
    )(x, g, w_bf, cos_tab, sin_tab)


def _diff_lambda(lamp, lam_init):
    s1 = jnp.sum(lamp[0:1] * lamp[1:2], axis=1, keepdims=True)
    s2 = jnp.sum(lamp[2:3] * lamp[3:4], axis=1, keepdims=True)
    return jnp.exp(s1) - jnp.exp(s2) + lam_init


def _subln(o, gain, lam_init):
    ms = jnp.mean(o * o, axis=-1, keepdims=True)
    return o * lax.rsqrt(ms + SUBLN_EPS) * gain * (1.0 - lam_init)


ONES_ROWS = 16


def _attn_prompt_kernel(lamp_ref, q_ref, k_ref, v_ref, gain_ref, o_ref,
                        vt_ref, m_ref, acc_ref, *, tq, tk, lam_init):
    qi = pl.program_id(2)
    seq, vw = v_ref.shape
    per_q = tq // tk

    @pl.when(qi == 0)
    def _():
        for c in range(seq // tk):
            blk = v_ref[c * tk:(c + 1) * tk, :].astype(F32)
            vt_ref[c, :vw, :] = blk.T.astype(BF16)
            vt_ref[c, vw:, :] = jnp.ones((ONES_ROWS, tk), BF16)

    q = q_ref[...]
    lane = lax.broadcasted_iota(jnp.int32, q.shape, 1)
    zero = jnp.zeros_like(q)
    qs = (jnp.where(lane < HEAD_DIM, q, zero), jnp.where(lane >= HEAD_DIM, q, zero))
    m_ref[...] = jnp.full(m_ref.shape, NEG_BIG, F32)
    acc_ref[...] = jnp.zeros(acc_ref.shape, F32)

    def chunk(c, diag):
        off = pl.multiple_of(c * tk, tk)
        kc = k_ref[pl.ds(off, tk), :]
        vc = vt_ref[c]
        def scores(j):
            st = lax.dot_general(kc, qs[j], (((1,), (1,)), ((), ())), preferred_element_type=F32)
            if diag is not None:
                key = lax.broadcasted_iota(jnp.int32, st.shape, 0) + diag * tk
                qry = lax.broadcasted_iota(jnp.int32, st.shape, 1)
                st = jnp.where(key <= qry, st, -jnp.inf)
            return st

        def softmax(j, st):
            m_prev = m_ref[j]
            m_new = jnp.maximum(m_prev, jnp.max(st, axis=0, keepdims=True))
            m_ref[j] = m_new
            return jnp.exp2(m_prev - m_new), jnp.exp2(st - m_new).astype(BF16)

        def accumulate(j, alpha, p):
            pv = jnp.dot(vc, p, preferred_element_type=F32)
            acc_ref[j] = alpha * acc_ref[j] + pv

        alpha0, p0 = softmax(0, scores(0))
        st1 = scores(1)
        accumulate(0, alpha0, p0)
        alpha1, p1 = softmax(1, st1)
        accumulate(1, alpha1, p1)

    def body(c, carry):
        chunk(c, None)
        return carry

    lax.fori_loop(0, qi * per_q, body, 0)
    for d in range(per_q):
        chunk(qi * per_q + d, d)

    lam = _diff_lambda(lamp_ref[...], lam_init)
    a0 = acc_ref[0]
    a1 = acc_ref[1]
    ot = a0[:vw] / a0[vw:vw + 1] - lam * (a1[:vw] / a1[vw:vw + 1])
    o_ref[...] = _subln(ot.T, gain_ref[...], lam_init).astype(o_ref.dtype)


def _attn_prompt(q_bf, k_bf, v_bf, lamp, gain, bsz, seq, tq, tk, lam_init):
    n, qkw = q_bf.shape
    vw_total = v_bf.shape[1]
    heads = qkw // (2 * HEAD_DIM)
    vw = vw_total // heads
    nq = seq // tq
    return pl.pallas_call(
        functools.partial(_attn_prompt_kernel, tq=tq, tk=tk, lam_init=lam_init),
        out_shape=jax.ShapeDtypeStruct((n, vw_total), BF16),
        grid=(bsz, heads, nq),
        in_specs=[pl.BlockSpec((4, HEAD_DIM), lambda b, h, i: (0, 0)),
                  pl.BlockSpec((tq, 2 * HEAD_DIM), lambda b, h, i: (b * nq + i, h)),
                  pl.BlockSpec((seq, 2 * HEAD_DIM), lambda b, h, i: (b, h)),
                  pl.BlockSpec((seq, vw), lambda b, h, i: (b, h)),
                  pl.BlockSpec((1, vw), lambda b, h, i: (0, 0))],
        out_specs=pl.BlockSpec((tq, vw), lambda b, h, i: (b * nq + i, h)),
        scratch_shapes=[pltpu.VMEM((seq // tk, vw + ONES_ROWS, tk), BF16),
                        pltpu.VMEM((2, 1, tq), F32),
                        pltpu.VMEM((2, vw + ONES_ROWS, tq), F32)],
        compiler_params=_cparams(3),
        name="attn_prompt",
    )(lamp, q_bf, k_bf, v_bf, gain)


def _attn_decode_kernel(pt_ref, lamp_ref, q_ref, kself_ref, vself_ref, gain_ref, *rest,
                        pages, lam_init, vw):
    k_refs = rest[:pages]
    v_refs = rest[pages:2 * pages]
    o_ref = rest[2 * pages]
    m_ref, l_ref, acc_ref = rest[2 * pages + 1:]
    j = pl.program_id(1)
    nh2 = m_ref.shape[0]
    width = q_ref.shape[1]

    @pl.when(j == 0)
    def _():
        m_ref[...] = jnp.full(m_ref.shape, NEG_BIG, F32)
        l_ref[...] = jnp.zeros(l_ref.shape, F32)
        acc_ref[...] = jnp.zeros(acc_ref.shape, F32)

    rowh = lax.broadcasted_iota(jnp.int32, (nh2, width), 0)
    laneh = lax.broadcasted_iota(jnp.int32, (nh2, width), 1) // HEAD_DIM
    q8 = jnp.where(rowh == laneh, jnp.broadcast_to(q_ref[...], (nh2, width)), 0.0)
    q8b = q8.astype(BF16)

    s_parts = [jnp.dot(q8b, k_refs[i][...].astype(BF16), preferred_element_type=F32)
               for i in range(pages)]
    s = jnp.concatenate(s_parts, axis=1)
    m_prev = m_ref[...]
    m_new = jnp.maximum(m_prev, jnp.max(s, axis=1, keepdims=True))
    alpha = jnp.exp2(m_prev - m_new)
    p = jnp.exp2(s - _lane_tile(m_new, s.shape[1] // LANES))
    l_ref[...] = alpha * l_ref[...] + jnp.sum(p, axis=1, keepdims=True)
    ps = k_refs[0].shape[1]
    nvh = width // vw
    pb = p.astype(BF16)
    pv = None
    for i in range(pages):
        cols = [jnp.dot(pb[:, i * ps:(i + 1) * ps],
                        v_refs[i][pl.ds(h, ps, stride=nvh), :].astype(BF16),
                        preferred_element_type=F32) for h in range(nvh)]
        part = jnp.concatenate(cols, axis=1)
        pv = part if pv is None else pv + part
    acc_ref[...] = _lane_tile(alpha, width // LANES) * acc_ref[...] + pv
    m_ref[...] = m_new

    @pl.when(j == pl.num_programs(1) - 1)
    def _():
        s_self = jnp.sum(q8 * kself_ref[...], axis=1, keepdims=True)
        m_old = m_ref[...]
        m_fin = jnp.maximum(m_old, s_self)
        a_fin = jnp.exp2(m_old - m_fin)
        p_self = jnp.exp2(s_self - m_fin)
        l_fin = a_fin * l_ref[...] + p_self
        acc = (_lane_tile(a_fin, width // LANES) * acc_ref[...]
               + _lane_tile(p_self, width // LANES) * vself_ref[...])
        o8 = acc / _lane_tile(l_fin, width // LANES)
        lam = _diff_lambda(lamp_ref[...], lam_init)
        vhead = lax.broadcasted_iota(jnp.int32, (nh2, width), 1) // vw
        sign = jnp.where((rowh & 1) == 0, 1.0, -lam)
        coef = jnp.where(vhead == rowh // 2, sign, 0.0)
        o = jnp.sum(o8 * coef, axis=0, keepdims=True)
        gain = gain_ref[...]
        outs = [_subln(o[:, h * vw:(h + 1) * vw], gain, lam_init) for h in range(width // vw)]
        o_ref[...] = jnp.concatenate(outs, axis=1)


def _attn_decode(q, k_self, v_self, cache_kt, cache_v2, page_table, lamp, gain, pages, lam_init):
    nseq, width = q.shape
    npages = page_table.shape[1]
    ps = cache_kt.shape[2]
    vw = gain.shape[1]
    nh2 = width // HEAD_DIM
    pt = page_table.reshape(-1)
    q3, k3, v3 = (a.reshape(nseq, 1, width) for a in (q, k_self, v_self))
    seq_spec = pl.BlockSpec((None, 1, width), lambda b, j, pt: (b, 0, 0))

    def page_spec(i, shape):
        return pl.BlockSpec((None,) + shape,
                            lambda b, j, pt: (pt[b * npages + j * pages + i], 0, 0))

    grid_spec = pltpu.PrefetchScalarGridSpec(
        num_scalar_prefetch=1,
        grid=(nseq, npages // pages),
        in_specs=[pl.BlockSpec((4, HEAD_DIM), lambda b, j, pt: (0, 0)), seq_spec, seq_spec, seq_spec,
                  pl.BlockSpec((1, vw), lambda b, j, pt: (0, 0))]
                 + [page_spec(i, cache_kt.shape[1:]) for i in range(pages)]
                 + [page_spec(i, cache_v2.shape[1:]) for i in range(pages)],
        out_specs=seq_spec,
        scratch_shapes=[pltpu.VMEM((nh2, LANES), F32), pltpu.VMEM((nh2, LANES), F32),
                        pltpu.VMEM((nh2, width), F32)],
    )
    out = pl.pallas_call(
        functools.partial(_attn_decode_kernel, pages=pages, lam_init=lam_init, vw=vw),
        out_shape=jax.ShapeDtypeStruct((nseq, 1, width), F32),
        grid_spec=grid_spec,
        compiler_params=_cparams(2),
        name="attn_decode",
    )(pt, lamp, q3, k3, v3, gain, *([cache_kt] * pages), *([cache_v2] * pages))
    return out.reshape(nseq, width)


def _cmul(a, b):
    return a[0] * b[0] - a[1] * b[1], a[0] * b[1] + a[1] * b[0]


def _s5_discretize(a_re, a_im, log_dt, b_re, b_im):
    a_re, a_im = a_re.astype(F32), a_im.astype(F32)
    step = jnp.exp(log_dt.astype(F32))[:, None]
    llb = (a_re * step, a_im * step)
    lb = _s5_power(llb, 1.0)
    num = (lb[0] - 1.0, lb[1])
    den = a_re * a_re + a_im * a_im
    quo = ((num[0] * a_re + num[1] * a_im) / den, (num[1] * a_re - num[0] * a_im) / den)
    b_bar = _cmul((quo[0][:, :, None], quo[1][:, :, None]), (b_re.astype(F32), b_im.astype(F32)))
    return llb, b_bar


def _s5_power(llb, k):
    mag = jnp.exp(llb[0] * k)
    return mag * jnp.cos(llb[1] * k), mag * jnp.sin(llb[1] * k)


def _s5_prompt_tables(params, seq):
    a_re, a_im, log_dt, b_re, b_im, c_re, c_im, d_skip = params
    g, p = a_re.shape
    ch, t = S5_GROUP, S5_CHUNK
    llb, b_bar = _s5_discretize(a_re, a_im, log_dt, b_re, b_im)
    c = (c_re.astype(F32), c_im.astype(F32))
    lags = jnp.arange(t + 1, dtype=F32)[:, None, None]
    powk = _s5_power((llb[0][None], llb[1][None]), lags)
    cp = _cmul((c[0][None], c[1][None]),
               (powk[0][:, :, None, :], powk[1][:, :, None, :]))
    klag = (jnp.einsum('kgcp,gpd->gkcd', cp[0][:t], b_bar[0])
            - jnp.einsum('kgcp,gpd->gkcd', cp[1][:t], b_bar[1]))
    skip = d_skip.astype(F32).reshape(g, ch)[:, :, None] * jnp.eye(ch, dtype=F32)[None]
    klag = klag.at[:, 0].add(skip)
    idx = jnp.arange(t)
    sel = (idx[None, None, :] - idx[None, :, None] == idx[:, None, None]).astype(F32)
    m_mat = jnp.einsum('kst,gkcd->gsdtc', sel, klag).reshape(g, t * ch, t * ch)
    pw_rev = (powk[0][:t][::-1].transpose(1, 0, 2)[:, :, None, :],
              powk[1][:t][::-1].transpose(1, 0, 2)[:, :, None, :])
    bb_t = (b_bar[0].transpose(0, 2, 1)[:, None], b_bar[1].transpose(0, 2, 1)[:, None])
    e_c = _cmul(pw_rev, bb_t)
    e_re = e_c[0].reshape(g, t * ch, p)
    e_im = e_c[1].reshape(g, t * ch, p)
    e_mat = jnp.concatenate([e_re, e_im, e_im, e_re], axis=-1)
    f_re = cp[0][1:t + 1].transpose(1, 0, 2, 3).reshape(g, t * ch, p)
    f_im = cp[1][1:t + 1].transpose(1, 0, 2, 3).reshape(g, t * ch, p)
    f_mat = jnp.concatenate([f_re, -f_im], axis=-1).transpose(0, 2, 1)
    ks = (jnp.arange(1, SUBLANES + 1, dtype=F32) * t)[None, :, None]
    pw = _s5_power((llb[0][:, None, :], llb[1][:, None, :]), ks)
    a1 = jnp.concatenate([pw[0], pw[0]], axis=-1)
    a2 = jnp.concatenate([-pw[1], pw[1]], axis=-1)
    rows = jnp.arange(SUBLANES)[None, :, None]
    parts = []
    for d in S5_SCAN_STRIDES:
        parts.append(jnp.where(rows >= d, a1[:, d - 1:d, :], 0.0))
        parts.append(jnp.where(rows >= d, a2[:, d - 1:d, :], 0.0))
    tab = jnp.concatenate(parts + [a1, a2], axis=1)
    return m_mat.astype(BF16), e_mat.astype(BF16), f_mat.astype(BF16), tab


def _s5_lane_permutation():
    n = SUBLANES * SUBLANES * S5_GROUP
    idx = jnp.arange(n).reshape(SUBLANES, SUBLANES, S5_GROUP).transpose(1, 0, 2).reshape(n)
    return jax.nn.one_hot(idx, n, dtype=BF16)


def _s5_prompt_kernel(u_ref, perm_ref, e_ref, m_ref, f_ref, tab_ref, y_ref, hl_ref,
                      us_ref, ee_ref, h_ref, ycat_ref):
    t = S5_CHUNK
    ng = SUBLANES
    nsub = u_ref.shape[0] // t
    sw = tab_ref.shape[2]
    half_t = t // 2
    perm = perm_ref[...]
    for half in range(2):
        xs = [u_ref[pl.ds(half * half_t + s, nsub, stride=t), :].astype(BF16) for s in range(half_t)]
        uh = jnp.dot(jnp.concatenate(xs, axis=1), perm, preferred_element_type=F32).astype(BF16)
        for gl in range(ng):
            us_ref[gl, :, half * LANES:(half + 1) * LANES] = uh[:, gl * LANES:(gl + 1) * LANES]
    for gl in range(ng):
        ee_ref[gl] = jnp.dot(us_ref[gl], e_ref[gl], preferred_element_type=F32)
        h_ref[gl, :SUBLANES, :] = jnp.zeros((SUBLANES, sw), F32)

    def step(i, carry):
        row = pl.multiple_of(i * SUBLANES, SUBLANES)
        out = []
        for gl in range(ng):
            h = ee_ref[gl, pl.ds(row, SUBLANES), :sw]
            hs = ee_ref[gl, pl.ds(row, SUBLANES), sw:]
            for di, d in enumerate(S5_SCAN_STRIDES):
                t1 = tab_ref[gl, 2 * di * SUBLANES:(2 * di + 1) * SUBLANES, :]
                t2 = tab_ref[gl, (2 * di + 1) * SUBLANES:(2 * di + 2) * SUBLANES, :]
                sh = pltpu.roll(h, d, 0)
                shs = pltpu.roll(hs, d, 0)
                h, hs = h + t1 * sh + t2 * shs, hs + t1 * shs - t2 * sh
            nd = 2 * len(S5_SCAN_STRIDES)
            q1 = tab_ref[gl, nd * SUBLANES:(nd + 1) * SUBLANES, :]
            q2 = tab_ref[gl, (nd + 1) * SUBLANES:(nd + 2) * SUBLANES, :]
            cb, cbs = carry[2 * gl], carry[2 * gl + 1]
            hf = h + q1 * cb + q2 * cbs
            hfs = hs + q1 * cbs - q2 * cb
            h_ref[gl, pl.ds(SUBLANES + row, SUBLANES), :] = hf
            out.append(jnp.broadcast_to(hf[SUBLANES - 1:SUBLANES, :], hf.shape))
            out.append(jnp.broadcast_to(hfs[SUBLANES - 1:SUBLANES, :], hfs.shape))
        return tuple(out)

    zero = jnp.zeros((SUBLANES, sw), F32)
    carry = lax.fori_loop(0, nsub // SUBLANES, step, (zero,) * (2 * ng))
    for gl in range(ng):
        hl_ref[gl:gl + 1, :] = carry[2 * gl][0:1, :]
        hin = h_ref[gl, pl.ds(SUBLANES - 1, nsub), :].astype(BF16)
        y = (jnp.dot(us_ref[gl], m_ref[gl], preferred_element_type=F32)
             + jnp.dot(hin, f_ref[gl], preferred_element_type=F32))
        for half in range(2):
            ycat_ref[half, :, gl * LANES:(gl + 1) * LANES] = y[:, half * LANES:(half + 1) * LANES]
    for half in range(2):
        yc = ycat_ref[half]
        hi = yc.astype(BF16)
        lo = (yc - hi.astype(F32)).astype(BF16)
        z = (jnp.dot(hi, perm, preferred_element_type=F32)
             + jnp.dot(lo, perm, preferred_element_type=F32))
        for s in range(half_t):
            y_ref[pl.ds(half * half_t + s, nsub, stride=t), :] = z[:, s * LANES:(s + 1) * LANES]


def _s5_prompt(u, tables, bsz, seq):
    m_mat, e_mat, f_mat, tab = tables
    g = m_mat.shape[0]
    p = f_mat.shape[1] // 2
    t, ch = S5_CHUNK, S5_GROUP
    ng = LANES // ch
    assert ng == SUBLANES and t * ch == 2 * LANES
    n, w = u.shape
    nsub = seq // t
    perm = _s5_lane_permutation()
    npm = perm.shape[0]
    ntab = tab.shape[1]
    slab = lambda s, b: (s, 0, 0)
    y, hl = pl.pallas_call(
        _s5_prompt_kernel,
        out_shape=(jax.ShapeDtypeStruct((n, w), F32), jax.ShapeDtypeStruct((bsz, g, 2 * p), F32)),
        grid=(w // LANES, bsz),
        in_specs=[pl.BlockSpec((seq, LANES), lambda s, b: (b, s)),
                  pl.BlockSpec((npm, npm), lambda s, b: (0, 0)),
                  pl.BlockSpec((ng, t * ch, 4 * p), slab),
                  pl.BlockSpec((ng, t * ch, t * ch), slab),
                  pl.BlockSpec((ng, 2 * p, t * ch), slab),
                  pl.BlockSpec((ng, ntab, 2 * p), slab)],
        out_specs=(pl.BlockSpec((seq, LANES), lambda s, b: (b, s)),
                   pl.BlockSpec((None, ng, 2 * p), lambda s, b: (b, s, 0))),
        scratch_shapes=[pltpu.VMEM((ng, nsub, t * ch), BF16),
                        pltpu.VMEM((ng, nsub, 4 * p), F32),
                        pltpu.VMEM((ng, SUBLANES + nsub, 2 * p), F32),
                        pltpu.VMEM((2, nsub, ng * LANES), F32)],
        compiler_params=_cparams(2),
        name="s5_prompt",
    )(u, perm, e_mat, m_mat, f_mat, tab)
    return y, hl[..., :p], hl[..., p:]


def _s5_step_kernel(u_ref, h_ref, hs_ref, bt_ref, ct_ref, a1_ref, a2_ref, d_ref, y_ref, hn_ref):
    hp = lax.Precision.HIGHEST
    u = u_ref[...]
    bu = jnp.einsum('gbc,gcp->gbp', u, bt_ref[...], precision=hp, preferred_element_type=F32)
    hn = a1_ref[...] * h_ref[...] + a2_ref[...] * hs_ref[...] + bu
    hn_ref[...] = hn
    y = jnp.einsum('gbp,gpc->gbc', hn, ct_ref[...], precision=hp, preferred_element_type=F32)
    y_ref[...] = y + d_ref[...] * u


def _s5_step(u, h0_re, h0_im, params):
    a_re, a_im, log_dt, b_re, b_im, c_re, c_im, d_skip = params
    g, p = a_re.shape
    ch = S5_GROUP
    bsz = u.shape[0]
    llb, b_bar = _s5_discretize(a_re, a_im, log_dt, b_re, b_im)
    lam_bar = _s5_power(llb, 1.0)
    bt = jnp.concatenate([b_bar[0].transpose(0, 2, 1), b_bar[1].transpose(0, 2, 1)], axis=-1)
    ct = jnp.concatenate([c_re.astype(F32), -c_im.astype(F32)], axis=-1).transpose(0, 2, 1)
    a1 = jnp.concatenate([lam_bar[0]] * 2, axis=-1)[:, None, :]
    a2 = jnp.concatenate([-lam_bar[1], lam_bar[1]], axis=-1)[:, None, :]
    h = jnp.concatenate([h0_re, h0_im], axis=-1).transpose(1, 0, 2).astype(F32)
    hs = jnp.concatenate([h0_im, h0_re], axis=-1).transpose(1, 0, 2).astype(F32)
    ug = u.reshape(bsz, g, ch).transpose(1, 0, 2)
    dg = d_skip.astype(F32).reshape(g, 1, ch)
    y, hn = pl.pallas_call(
        _s5_step_kernel,
        out_shape=(jax.ShapeDtypeStruct((g, bsz, ch), F32), jax.ShapeDtypeStruct((g, bsz, 2 * p), F32)),
        compiler_params=pltpu.CompilerParams(vmem_limit_bytes=VMEM_LIMIT),
        name="s5_step",
    )(ug, h, hs, bt, ct, a1, a2, dg)
    hn = hn.transpose(1, 0, 2)
    return y.transpose(1, 0, 2).reshape(bsz, g * ch), hn[..., :p], hn[..., p:]


def _resident(a):
    zeros = (0,) * a.ndim
    return pl.BlockSpec(a.shape, lambda *_: zeros, pipeline_mode=pl.Buffered(1))


def _ffn_apply(x, ffn_refs, final_norm):
    g_ref, wg_ref, wu_ref, wd_ref, gf_ref = ffn_refs
    hn = _rms(x, g_ref[...], NORM_EPS).astype(BF16)
    hidden = wg_ref.shape[1]
    th = MXU_DIM if hidden % MXU_DIM == 0 else hidden
    acc = None
    for c in range(hidden // th):
        cols = slice(c * th, (c + 1) * th)
        gate = jnp.dot(hn, wg_ref[:, cols], preferred_element_type=F32)
        up = jnp.dot(hn, wu_ref[:, cols], preferred_element_type=F32)
        act = (jax.nn.silu(gate) * up).astype(BF16)
        part = jnp.dot(act, wd_ref[cols, :], preferred_element_type=F32)
        acc = part if acc is None else acc + part
    out = x + acc
    if final_norm:
        out = _rms(out, gf_ref[...], NORM_EPS)
    return out


def _ffn_specs(ffn_w):
    return [_resident(a) for a in ffn_w]


def _ffn_kernel(x_ref, *rest, final_norm):
    o_ref = rest[-1]
    o_ref[...] = _ffn_apply(x_ref[...], rest[:-1], final_norm)


def _ffn(x, ffn_w, tm, final_norm):
    n, d = x.shape
    return pl.pallas_call(
        functools.partial(_ffn_kernel, final_norm=final_norm),
        out_shape=jax.ShapeDtypeStruct((n, d), F32),
        grid=(n // tm,),
        in_specs=[pl.BlockSpec((tm, d), lambda i: (i, 0))] + _ffn_specs(ffn_w),
        out_specs=pl.BlockSpec((tm, d), lambda i: (i, 0)),
        compiler_params=_cparams(1),
        name="ffn",
    )(x, *ffn_w)


def _post_ab_kernel(x_ref, y_ref, a_ref, wglu_ref, bglu_ref, wo_ref, *rest, s5w, final_norm):
    o_ref = rest[-1]
    z = jax.nn.gelu(y_ref[...])
    gl = jnp.dot(z.astype(BF16), wglu_ref[...], preferred_element_type=F32) + bglu_ref[...]
    s5o = (z * jax.nn.sigmoid(gl)).astype(BF16)
    mix = (jnp.dot(s5o, wo_ref[:s5w, :], preferred_element_type=F32)
           + jnp.dot(a_ref[...].astype(BF16), wo_ref[s5w:, :], preferred_element_type=F32))
    o_ref[...] = _ffn_apply(x_ref[...] + mix, rest[:-1], final_norm)


def _post_ab(x, y, attn, wglu_bf, bglu, wo_bf, ffn_w, tm, final_norm):
    n, d = x.shape
    s5w = y.shape[1]
    aw = attn.shape[1]
    row = lambda i: (i, 0)
    return pl.pallas_call(
        functools.partial(_post_ab_kernel, s5w=s5w, final_norm=final_norm),
        out_shape=jax.ShapeDtypeStruct((n, d), F32),
        grid=(n // tm,),
        in_specs=[pl.BlockSpec((tm, d), row), pl.BlockSpec((tm, s5w), row), pl.BlockSpec((tm, aw), row),
                  _resident(wglu_bf), _resident(bglu), _resident(wo_bf)] + _ffn_specs(ffn_w),
        out_specs=pl.BlockSpec((tm, d), row),
        compiler_params=_cparams(1),
        name="post_ab_ffn",
    )(x, y, attn, wglu_bf, bglu, wo_bf, *ffn_w)


def _lru_gates(xc, wg_ref, ba, bx, lam_param):
    xcb = xc.astype(BF16)
    halves = [jnp.dot(xcb[:, h * MXU_DIM:(h + 1) * MXU_DIM], wg_ref[h], preferred_element_type=F32)
              for h in range(wg_ref.shape[0])]
    r_pre = jnp.concatenate([hv[:, :MXU_DIM] for hv in halves], axis=1)
    i_pre = jnp.concatenate([hv[:, MXU_DIM:] for hv in halves], axis=1)
    r = jax.nn.sigmoid(r_pre + ba)
    i = jax.nn.sigmoid(i_pre + bx)
    log_a = -LRU_C * r * jax.nn.softplus(-lam_param)
    a = jnp.exp(log_a)
    b = jnp.sqrt(-jnp.tanh(log_a) * (a * a + 1.0)) * (i * xc)
    return a, b


def _pool_project(pooled, pw_ref, scale):
    pb = pooled.astype(BF16)
    halves = [jnp.dot(pb[:, h * MXU_DIM:(h + 1) * MXU_DIM], pw_ref[h], preferred_element_type=F32)
              for h in range(pw_ref.shape[0])]
    return jnp.concatenate(halves, axis=1) * scale


def _cd_prompt_kernel(x_ref, g_ref, win_ref, cw_ref, cb_ref, wg_ref, ba_ref, bx_ref, lam_ref,
                      pw_ref, psc_ref, wo_ref, fg_ref, fwg_ref, fwu_ref, fwd_ref, fgf_ref,
                      o_ref, conv_ref, lru_ref, pool_ref,
                      xl_ref, xp_ref, a_ref, b_ref, hs_ref, carry_ref, *, tm, w, final_norm):
    ti = pl.program_id(1)
    hist = POOL_BUF + 1

    @pl.when(ti == 0)
    def _():
        xl_ref[:hist, :] = jnp.zeros((hist, w), F32)
        xp_ref[:hist, :] = jnp.zeros((hist, w), F32)
        carry_ref[...] = jnp.zeros(carry_ref.shape, F32)

    x = x_ref[...]
    hn = _rms(x, g_ref[...], NORM_EPS).astype(BF16)
    proj = jnp.dot(hn, win_ref[...], preferred_element_type=F32)
    gate = proj[:, :w]
    xl_ref[hist:, :] = proj[:, w:2 * w]
    xp_ref[hist:, :] = proj[:, 2 * w:]

    xc = cb_ref[...] + xl_ref[pl.ds(hist - (CONV_WIDTH - 1), tm), :] * cw_ref[0:1, :]
    for j in range(1, CONV_WIDTH):
        xc = xc + xl_ref[pl.ds(hist - (CONV_WIDTH - 1) + j, tm), :] * cw_ref[j:j + 1, :]

    a, b = _lru_gates(xc, wg_ref, ba_ref[...], bx_ref[...], lam_ref[...])
    a_ref[...] = a
    b_ref[...] = b

    def group(gi, carry):
        row = pl.multiple_of(gi * SUBLANES, SUBLANES)
        ag = a_ref[pl.ds(row, SUBLANES), :]
        bg = b_ref[pl.ds(row, SUBLANES), :]
        ridx = lax.broadcasted_iota(jnp.int32, ag.shape, 0)
        for d in (1, 2, 4):
            a_sh = jnp.where(ridx >= d, pltpu.roll(ag, d, 0), 1.0)
            b_sh = jnp.where(ridx >= d, pltpu.roll(bg, d, 0), 0.0)
            bg = ag * b_sh + bg
            ag = ag * a_sh
        h = bg + ag * carry
        hs_ref[pl.ds(row, SUBLANES), :] = h
        return jnp.broadcast_to(h[SUBLANES - 1:SUBLANES, :], h.shape)

    carry = lax.fori_loop(0, tm // SUBLANES, group, jnp.broadcast_to(carry_ref[...], (SUBLANES, w)))
    carry_ref[...] = carry[0:1, :]
    lru_out = jax.nn.gelu(gate) * hs_ref[...]

    pg = w // len(POOL_WINDOWS)
    pos = ti * tm + lax.broadcasted_iota(jnp.int32, (tm, pg), 0)
    pooled = []
    for gidx, win in enumerate(POOL_WINDOWS):
        cols = slice(gidx * pg, (gidx + 1) * pg)
        cur = xp_ref[pl.ds(hist, tm), cols]
        tot = cur
        for kk in range(1, win):
            tot = tot + xp_ref[pl.ds(hist - kk, tm), cols]
        cnt = jnp.minimum(pos + 1, win).astype(F32)
        pooled.append(tot / cnt - cur)
    pool_out = _pool_project(jnp.concatenate(pooled, axis=1), pw_ref, psc_ref[...])

    mix = (jnp.dot(lru_out.astype(BF16), wo_ref[:w, :], preferred_element_type=F32)
           + jnp.dot(pool_out.astype(BF16), wo_ref[w:, :], preferred_element_type=F32))
    o_ref[...] = _ffn_apply(x + mix, (fg_ref, fwg_ref, fwu_ref, fwd_ref, fgf_ref), final_norm)

    conv_ref[...] = xl_ref[pl.ds(hist + tm - (CONV_WIDTH - 1), CONV_WIDTH - 1), :]
    lru_ref[...] = carry_ref[...]
    pool_ref[...] = xp_ref[pl.ds(hist + tm - POOL_BUF, POOL_BUF), :]
    xl_ref[:hist, :] = xl_ref[pl.ds(tm, hist), :]
    xp_ref[:hist, :] = xp_ref[pl.ds(tm, hist), :]


def _cd_prompt(x, g, cdw, ffn_w, bsz, seq, tm, final_norm):
    n, d = x.shape
    w = cdw[1].shape[1]
    nt = seq // tm
    hist = POOL_BUF + 1
    row = lambda b, i: (b * nt + i, 0)
    st = lambda b, i: (b, 0, 0)
    return pl.pallas_call(
        functools.partial(_cd_prompt_kernel, tm=tm, w=w, final_norm=final_norm),
        out_shape=(jax.ShapeDtypeStruct((n, d), F32),
                   jax.ShapeDtypeStruct((bsz, CONV_WIDTH - 1, w), F32),
                   jax.ShapeDtypeStruct((bsz, 1, w), F32),
                   jax.ShapeDtypeStruct((bsz, POOL_BUF, w), F32)),
        grid=(bsz, nt),
        in_specs=[pl.BlockSpec((tm, d), row), _resident(g)] + [_resident(a) for a in cdw]
                 + _ffn_specs(ffn_w),
        out_specs=(pl.BlockSpec((tm, d), row),
                   pl.BlockSpec((None, CONV_WIDTH - 1, w), st),
                   pl.BlockSpec((None, 1, w), st),
                   pl.BlockSpec((None, POOL_BUF, w), st)),
        scratch_shapes=[pltpu.VMEM((hist + tm, w), F32), pltpu.VMEM((hist + tm, w), F32),
                        pltpu.VMEM((tm, w), F32), pltpu.VMEM((tm, w), F32), pltpu.VMEM((tm, w), F32),
                        pltpu.VMEM((1, w), F32)],
        compiler_params=_cparams(2),
        name="cd_prompt_ffn",
    )(x, g, *cdw, *ffn_w)


def _cd_step_kernel(x_ref, g_ref, win_ref, cw_ref, cb_ref, wg_ref, ba_ref, bx_ref, lam_ref,
                    pw_ref, psc_ref, wo_ref, sc_ref, sl_ref, sp_ref,
                    o_ref, conv_ref, lru_ref, pool_ref, *, w):
    x = x_ref[...]
    hn = _rms(x, g_ref[...], NORM_EPS).astype(BF16)
    proj = jnp.dot(hn, win_ref[...], preferred_element_type=F32)
    gate = proj[:, :w]
    xl = proj[:, w:2 * w]
    xp = proj[:, 2 * w:]
    xc = cb_ref[...] + xl * cw_ref[CONV_WIDTH - 1:CONV_WIDTH, :]
    for j in range(CONV_WIDTH - 1):
        xc = xc + sc_ref[j] * cw_ref[j:j + 1, :]
    a, b = _lru_gates(xc, wg_ref, ba_ref[...], bx_ref[...], lam_ref[...])
    h = a * sl_ref[...] + b
    lru_out = jax.nn.gelu(gate) * h

    pg = w // len(POOL_WINDOWS)
    pooled = []
    for gidx, win in enumerate(POOL_WINDOWS):
        cols = slice(gidx * pg, (gidx + 1) * pg)
        cur = xp[:, cols]
        tot = cur
        for kk in range(1, win):
            tot = tot + sp_ref[POOL_BUF - kk][:, cols]
        pooled.append(tot / float(win) - cur)
    pool_out = _pool_project(jnp.concatenate(pooled, axis=1), pw_ref, psc_ref[...])

    mix = (jnp.dot(lru_out.astype(BF16), wo_ref[:w, :], preferred_element_type=F32)
           + jnp.dot(pool_out.astype(BF16), wo_ref[w:, :], preferred_element_type=F32))
    o_ref[...] = x + mix
    for j in range(CONV_WIDTH - 2):
        conv_ref[j] = sc_ref[j + 1]
    conv_ref[CONV_WIDTH - 2] = xl
    lru_ref[...] = h
    for j in range(POOL_BUF - 1):
        pool_ref[j] = sp_ref[j + 1]
    pool_ref[POOL_BUF - 1] = xp


def _cd_step(x, g, cdw, conv_buf, lru_h0, pool_buf):
    n, d = x.shape
    win_bf, cw, cb, wg_bf, ba, bx, lam, pw_bf, psc, wo_bf = cdw
    w = cw.shape[1]
    sc = conv_buf.transpose(1, 0, 2)
    sp = pool_buf.transpose(1, 0, 2)
    o, conv_n, lru_n, pool_n = pl.pallas_call(
        functools.partial(_cd_step_kernel, w=w),
        out_shape=(jax.ShapeDtypeStruct((n, d), F32),
                   jax.ShapeDtypeStruct((CONV_WIDTH - 1, n, w), F32),
                   jax.ShapeDtypeStruct((n, w), F32),
                   jax.ShapeDtypeStruct((POOL_BUF, n, w), F32)),
        compiler_params=pltpu.CompilerParams(vmem_limit_bytes=VMEM_LIMIT),
        name="cd_step",
    )(x, g, win_bf, cw, cb, wg_bf, ba, bx, lam, pw_bf, psc, wo_bf, sc, lru_h0, sp)
    return o, conv_n.transpose(1, 0, 2), lru_n, pool_n.transpose(1, 0, 2)


def _block_diag_tiles(w, tile):
    nb, c, _ = w.shape
    per = tile // c
    w4 = w.reshape(nb // per, per, c, c)
    placed = jnp.einsum('tpij,pq->tpiqj', w4, jnp.eye(per, dtype=w.dtype))
    return placed.reshape(nb // per, tile, tile)


def _rope_tables(pos, rows):
    half = HEAD_DIM // 2
    inv = ROPE_THETA ** (-jnp.arange(half, dtype=F32) / half)
    ang = pos.astype(F32)[:, None] * inv[None, :]
    cos = jnp.cos(ang)
    sin = jnp.sin(ang)
    reps = LANES // HEAD_DIM
    cos_t = jnp.tile(jnp.concatenate([cos, cos], axis=1), (1, reps))
    sin_t = jnp.tile(jnp.concatenate([-sin, sin], axis=1), (1, reps))
    if cos_t.shape[0] != rows:
        cos_t = jnp.broadcast_to(cos_t, (rows, LANES))
        sin_t = jnp.broadcast_to(sin_t, (rows, LANES))
    return cos_t, sin_t


def _row_tile(n, target):
    t = min(n, target)
    while n % t:
        t //= 2
    return t


def kernel(x_prompt, x_sample, cache_k, cache_v, page_table, state_s5_re, state_s5_im, state_conv, state_lru, state_pool, norm_mix, norm_ffn, norm_final, w_in_ab, w_out_ab, s5_a_re, s5_a_im, s5_log_dt, s5_b_re, s5_b_im, s5_c_re, s5_c_im, s5_d, s5_w_glu, s5_b_glu, diff_lq1, diff_lk1, diff_lq2, diff_lk2, diff_subln, w_in_cd, w_out_cd, conv_w, conv_b, lru_wa, lru_ba, lru_wx, lru_bx, lru_lambda, pool_w, pool_scale, ffn_w_gate, ffn_w_up, ffn_w_down):
    bsz, seq, d = x_prompt.shape
    nseq, dec_seq, _ = x_sample.shape
    assert dec_seq == 1
    depth = norm_mix.shape[0]
    npages = page_table.shape[1]
    page_size = cache_k.shape[2]
    past_len = npages * page_size
    assert past_len >= POOL_BUF
    s5w = s5_d.shape[1]
    qkw = cache_k.shape[3] * cache_k.shape[4]
    vw_total = cache_v.shape[3] * cache_v.shape[4]
    heads = cache_v.shape[3]
    n_p = bsz * seq
    row2 = lambda a: a.reshape(1, -1)

    xp = x_prompt.reshape(n_p, d)
    xs = x_sample.reshape(nseq, d)
    tm_p = _row_tile(seq, 512)
    cos_p, sin_p = _rope_tables(jnp.arange(seq, dtype=jnp.int32), seq)
    cos_s, sin_s = _rope_tables(jnp.full((1,), past_len, jnp.int32), nseq)

    outs_p = {'k': [], 'v': [], 're': [], 'im': [], 'conv': [], 'lru': [], 'pool': []}
    outs_s = {'k': [], 'v': [], 're': [], 'im': [], 'conv': [], 'lru': [], 'pool': []}

    for l in range(depth):
        j = l // 2
        g_mix = row2(norm_mix[l])
        last = l == depth - 1
        ffn_w = (row2(norm_ffn[l]), ffn_w_gate[l].astype(BF16), ffn_w_up[l].astype(BF16),
                 ffn_w_down[l].astype(BF16), row2(norm_final))
        if l % 2 == 0:
            lam_init = 0.8 - 0.6 * math.exp(-0.3 * l)
            w_in = w_in_ab[j].astype(BF16)
            w_out = w_out_ab[j].astype(BF16)
            w_glu = s5_w_glu[j].astype(BF16)
            b_glu = row2(s5_b_glu[j])
            lamp = jnp.stack([diff_lq1[j], diff_lk1[j], diff_lq2[j], diff_lk2[j]]).astype(F32)
            gain = row2(diff_subln[j])
            s5p = (s5_a_re[j], s5_a_im[j], s5_log_dt[j], s5_b_re[j], s5_b_im[j],
                   s5_c_re[j], s5_c_im[j], s5_d[j])
            u, q_bf, k, v, k_bf, v_bf = _inproj_ab(xp, g_mix, w_in, cos_p, sin_p, tm_p, s5w, qkw)
            t_attn = _row_tile(seq, 1024)
            attn = _attn_prompt(q_bf, k_bf, v_bf, lamp, gain, bsz, seq, t_attn, t_attn, lam_init)
            y, h_re, h_im = _s5_prompt(u, _s5_prompt_tables(s5p, seq), bsz, seq)
            xp = _post_ab(xp, y, attn, w_glu, b_glu, w_out, ffn_w, tm_p, last)
            outs_p['k'].append(k.reshape(bsz, seq, qkw // HEAD_DIM, HEAD_DIM))
            outs_p['v'].append(v.reshape(bsz, seq, heads, vw_total // heads))
            outs_p['re'].append(h_re)
            outs_p['im'].append(h_im)
            u, q_bf, k, v, _, _ = _inproj_ab(xs, g_mix, w_in, cos_s, sin_s, nseq, s5w, qkw)
            ck = jnp.transpose(cache_k[j], (0, 2, 3, 1)).reshape(-1, qkw, page_size)
            cv = cache_v[j].reshape(-1, page_size * heads, vw_total // heads)
            attn = _attn_decode(q_bf.astype(F32), k, v.reshape(nseq, vw_total), ck, cv, page_table,
                                lamp, gain,
                                math.gcd(npages, 32), lam_init)
            y, h_re, h_im = _s5_step(u, state_s5_re[j], state_s5_im[j], s5p)
            xs = _post_ab(xs, y, attn, w_glu, b_glu, w_out, ffn_w, nseq, last)
            outs_s['k'].append(k.reshape(nseq, 1, qkw // HEAD_DIM, HEAD_DIM))
            outs_s['v'].append(v.reshape(nseq, 1, heads, vw_total // heads))
            outs_s['re'].append(h_re)
            outs_s['im'].append(h_im)
        else:
            w = conv_w.shape[2]
            wg = jnp.concatenate([_block_diag_tiles(lru_wa[j], MXU_DIM),
                                  _block_diag_tiles(lru_wx[j], MXU_DIM)], axis=2).astype(BF16)
            cdw = (w_in_cd[j].astype(BF16), conv_w[j], row2(conv_b[j]), wg, row2(lru_ba[j]),
                   row2(lru_bx[j]), row2(lru_lambda[j]),
                   _block_diag_tiles(pool_w[j], MXU_DIM).astype(BF16), row2(pool_scale[j]),
                   w_out_cd[j].astype(BF16))
            xp, c_new, h_new, p_new = _cd_prompt(xp, g_mix, cdw, ffn_w, bsz, seq, tm_p, last)
            outs_p['conv'].append(c_new)
            outs_p['lru'].append(h_new.reshape(bsz, w))
            outs_p['pool'].append(p_new)
            xs, c_new, h_new, p_new = _cd_step(xs, g_mix, cdw, state_conv[j], state_lru[j], state_pool[j])
            outs_s['conv'].append(c_new)
            outs_s['lru'].append(h_new)
            outs_s['pool'].append(p_new)
            xs = _ffn(xs, ffn_w, nseq, last)

    sp = {name: jnp.stack(vals) for name, vals in outs_p.items()}
    ss = {name: jnp.stack(vals) for name, vals in outs_s.items()}
    y_prompt = xp.reshape(bsz, seq, d)
    y_sample = xs.reshape(nseq, 1, d)
    return (y_prompt, y_sample, sp['k'], sp['v'], ss['k'], ss['v'], sp['re'], sp['im'], ss['re'], ss['im'],
            sp['conv'], ss['conv'], sp['lru'], ss['lru'], sp['pool'], ss['pool'])
```

```python
import functools
import math

import jax
import jax.numpy as jnp
from jax import lax
from jax.experimental import pallas as pl
from jax.experimental.pallas import tpu as pltpu

F32 = jnp.float32
BF16 = jnp.bfloat16

HEAD_DIM = 64
ROPE_THETA = 10000.0
NORM_EPS = 1e-6
SUBLN_EPS = 1e-5
S5_GROUP = 16
S5_CHUNK = 16
S5_SCAN_STRIDES = (1, 2, 4)
LRU_C = 8.0
CONV_WIDTH = 4
POOL_WINDOWS = (2, 4, 8, 16)
POOL_BUF = max(POOL_WINDOWS) - 1

LANES = 128
SUBLANES = 8
MXU_DIM = 256
VMEM_LIMIT = 56 * 1024 * 1024

NEG_BIG = -1e30
LOG2E = math.log2(math.e)


def _cparams(n_axes, flags=None):
    return pltpu.CompilerParams(dimension_semantics=("arbitrary",) * n_axes,
                                vmem_limit_bytes=VMEM_LIMIT, flags=flags)


def _rms(x, g, eps):
    ms = jnp.mean(x * x, axis=-1, keepdims=True)
    return x * lax.rsqrt(ms + eps) * g


def _lane_tile(x, reps):
    return jnp.concatenate([x] * reps, axis=1)


def _rope(x, cos, sin_signed):
    width = x.shape[1]
    lane = lax.broadcasted_iota(jnp.int32, x.shape, 1)
    half = HEAD_DIM // 2
    rot = jnp.where((lane & half) == 0, pltpu.roll(x, width - half, 1), pltpu.roll(x, half, 1))
    return x * cos + rot * sin_signed


def _inproj_ab_kernel(x_ref, g_ref, w_ref, cos_ref, sin_ref,
                      u_ref, q_ref, k_ref, v_ref, kb_ref, vb_ref, *, s5w, qkw):
    h = _rms(x_ref[...], g_ref[...], NORM_EPS).astype(BF16)
    proj = jnp.dot(h, w_ref[...], preferred_element_type=F32)
    reps = qkw // LANES
    cos = _lane_tile(cos_ref[...], reps)
    sin = _lane_tile(sin_ref[...], reps)
    u_ref[...] = proj[:, :s5w]
    q = _rope(proj[:, s5w:s5w + qkw], cos, sin)
    k = _rope(proj[:, s5w + qkw:s5w + 2 * qkw], cos, sin)
    v = proj[:, s5w + 2 * qkw:]
    q_ref[...] = (q * (HEAD_DIM ** -0.5 * LOG2E)).astype(BF16)
    k_ref[...] = k
    nvh = v.shape[1] // LANES
    for hh in range(nvh):
        v_ref[pl.ds(hh, v.shape[0], stride=nvh), :] = v[:, hh * LANES:(hh + 1) * LANES]
    kb_ref[...] = k.astype(BF16)
    vb_ref[...] = v.astype(BF16)


def _inproj_ab(x, g, w_bf, cos_tab, sin_tab, tm, s5w, qkw):
    n, d = x.shape
    wn = w_bf.shape[1]
    vw = wn - s5w - 2 * qkw
    nvh = vw // LANES
    period = cos_tab.shape[0] // tm
    row = lambda i: (i, 0)
    const = lambda i: (0, 0)
    tab = lambda i: (i % period, 0)
    return pl.pallas_call(
        functools.partial(_inproj_ab_kernel, s5w=s5w, qkw=qkw),
        out_shape=(jax.ShapeDtypeStruct((n, s5w), F32), jax.ShapeDtypeStruct((n, qkw), BF16),
                   jax.ShapeDtypeStruct((n, qkw), F32), jax.ShapeDtypeStruct((n * nvh, LANES), F32),
                   jax.ShapeDtypeStruct((n, qkw), BF16), jax.ShapeDtypeStruct((n, vw), BF16)),
        grid=(n // tm,),
        in_specs=[pl.BlockSpec((tm, d), row), pl.BlockSpec((1, d), const),
                  pl.BlockSpec((d, wn), const), pl.BlockSpec((tm, LANES), tab),
                  pl.BlockSpec((tm, LANES), tab)],
        out_specs=(pl.BlockSpec((tm, s5w), row), pl.BlockSpec((tm, qkw), row),
                   pl.BlockSpec((tm, qkw), row), pl.BlockSpec((tm * nvh, LANES), row),
                   pl.BlockSpec((tm, qkw), row), pl.BlockSpec((tm, vw), row)),
        compiler_params=_cparams(1),
        name="inproj_ab",
    )(x, g, w_bf, cos_tab, sin_tab)


def _diff_lambda(lamp, lam_init):
    s1 = jnp.sum(lamp[0:1] * lamp[1:2], axis=1, keepdims=True)
    s2 = jnp.sum(lamp[2:3] * lamp[3:4], axis=1, keepdims=True)
    return jnp.exp(s1) - jnp.exp(s2) + lam_init


def _subln(o, gain, lam_init):
    ms = jnp.mean(o * o, axis=-1, keepdims=True)
    return o * lax.rsqrt(ms + SUBLN_EPS) * gain * (1.0 - lam_init)


ONES_ROWS = 16


def _attn_prompt_kernel(lamp_ref, q_ref, k_ref, v_ref, gain_ref, o_ref,
                        vt_ref, m_ref, acc_ref, *, tq, tk, lam_init):
    qi = pl.program_id(2)
    seq, vw = v_ref.shape
    per_q = tq // tk

    @pl.when(qi == 0)
    def _():
        for c in range(seq // tk):
            blk = v_ref[c * tk:(c + 1) * tk, :].astype(F32)
            vt_ref[c, :vw, :] = blk.T.astype(BF16)
            vt_ref[c, vw:, :] = jnp.ones((ONES_ROWS, tk), BF16)

    q = q_ref[...]
    lane = lax.broadcasted_iota(jnp.int32, q.shape, 1)
    zero = jnp.zeros_like(q)
    qs = (jnp.where(lane < HEAD_DIM, q, zero), jnp.where(lane >= HEAD_DIM, q, zero))
    m_ref[...] = jnp.full(m_ref.shape, NEG_BIG, F32)
    acc_ref[...] = jnp.zeros(acc_ref.shape, F32)

    def chunk(c, diag):
        off = pl.multiple_of(c * tk, tk)
        kc = k_ref[pl.ds(off, tk), :]
        vc = vt_ref[c]

        def scores(j):
            st = lax.dot_general(kc, qs[j], (((1,), (1,)), ((), ())), preferred_element_type=F32)
            if diag is not None:
                key = lax.broadcasted_iota(jnp.int32, st.shape, 0) + diag * tk
                qry = lax.broadcasted_iota(jnp.int32, st.shape, 1)
                st = jnp.where(key <= qry, st, -jnp.inf)
            return st

        def softmax(j, st):
            m_prev = m_ref[j]
            m_new = jnp.maximum(m_prev, jnp.max(st, axis=0, keepdims=True))
            m_ref[j] = m_new
            return jnp.exp2(m_prev - m_new), jnp.exp2(st - m_new).astype(BF16)

        def accumulate(j, alpha, p):
            pv = jnp.dot(vc, p, preferred_element_type=F32)
            acc_ref[j] = alpha * acc_ref[j] + pv

        alpha0, p0 = softmax(0, scores(0))
        st1 = scores(1)
        accumulate(0, alpha0, p0)
        alpha1, p1 = softmax(1, st1)
        accumulate(1, alpha1, p1)

    def body(c, carry):
        chunk(c, None)
        return carry

    lax.fori_loop(0, qi * per_q, body, 0)
    for d in range(per_q):
        chunk(qi * per_q + d, d)

    lam = _diff_lambda(lamp_ref[...], lam_init)
    a0 = acc_ref[0]
    a1 = acc_ref[1]
    ot = a0[:vw] / a0[vw:vw + 1] - lam * (a1[:vw] / a1[vw:vw + 1])
    o_ref[...] = _subln(ot.T, gain_ref[...], lam_init).astype(o_ref.dtype)


def _attn_prompt(q_bf, k_bf, v_bf, lamp, gain, bsz, seq, tq, tk, lam_init):
    n, qkw = q_bf.shape
    vw_total = v_bf.shape[1]
    heads = qkw // (2 * HEAD_DIM)
    vw = vw_total // heads
    nq = seq // tq
    return pl.pallas_call(
        functools.partial(_attn_prompt_kernel, tq=tq, tk=tk, lam_init=lam_init),
        out_shape=jax.ShapeDtypeStruct((n, vw_total), BF16),
        grid=(bsz, heads, nq),
        in_specs=[pl.BlockSpec((4, HEAD_DIM), lambda b, h, i: (0, 0)),
                  pl.BlockSpec((tq, 2 * HEAD_DIM), lambda b, h, i: (b * nq + i, h)),
                  pl.BlockSpec((seq, 2 * HEAD_DIM), lambda b, h, i: (b, h)),
                  pl.BlockSpec((seq, vw), lambda b, h, i: (b, h)),
                  pl.BlockSpec((1, vw), lambda b, h, i: (0, 0))],
        out_specs=pl.BlockSpec((tq, vw), lambda b, h, i: (b * nq + i, h)),
        scratch_shapes=[pltpu.VMEM((seq // tk, vw + ONES_ROWS, tk), BF16),
                        pltpu.VMEM((2, 1, tq), F32),
                        pltpu.VMEM((2, vw + ONES_ROWS, tq), F32)],
        compiler_params=_cparams(3),
        name="attn_prompt",
    )(lamp, q_bf, k_bf, v_bf, gain)


def _attn_decode_kernel(pt_ref, lamp_ref, q_ref, kself_ref, vself_ref, gain_ref, *rest,
                        pages, lam_init, vw):
    k_refs = rest[:pages]
    v_refs = rest[pages:2 * pages]
    o_ref = rest[2 * pages]
    m_ref, l_ref, acc_ref = rest[2 * pages + 1:]
    j = pl.program_id(1)
    nh2 = m_ref.shape[0]
    width = q_ref.shape[1]

    @pl.when(j == 0)
    def _():
        m_ref[...] = jnp.full(m_ref.shape, NEG_BIG, F32)
        l_ref[...] = jnp.zeros(l_ref.shape, F32)
        acc_ref[...] = jnp.zeros(acc_ref.shape, F32)

    rowh = lax.broadcasted_iota(jnp.int32, (nh2, width), 0)
    laneh = lax.broadcasted_iota(jnp.int32, (nh2, width), 1) // HEAD_DIM
    q8 = jnp.where(rowh == laneh, jnp.broadcast_to(q_ref[...], (nh2, width)), 0.0)
    q8b = q8.astype(BF16)

    s_parts = [jnp.dot(q8b, k_refs[i][...].astype(BF16), preferred_element_type=F32)
               for i in range(pages)]
    s = jnp.concatenate(s_parts, axis=1)
    m_prev = m_ref[...]
    m_new = jnp.maximum(m_prev, jnp.max(s, axis=1, keepdims=True))
    alpha = jnp.exp2(m_prev - m_new)
    p = jnp.exp2(s - _lane_tile(m_new, s.shape[1] // LANES))
    l_ref[...] = alpha * l_ref[...] + jnp.sum(p, axis=1, keepdims=True)
    ps = k_refs[0].shape[1]
    nvh = width // vw
    pb = p.astype(BF16)
    pv = None
    for i in range(pages):
        cols = [jnp.dot(pb[:, i * ps:(i + 1) * ps],
                        v_refs[i][pl.ds(h, ps, stride=nvh), :].astype(BF16),
                        preferred_element_type=F32) for h in range(nvh)]
        part = jnp.concatenate(cols, axis=1)
        pv = part if pv is None else pv + part
    acc_ref[...] = _lane_tile(alpha, width // LANES) * acc_ref[...] + pv
    m_ref[...] = m_new

    @pl.when(j == pl.num_programs(1) - 1)
    def _():
        s_self = jnp.sum(q8 * kself_ref[...], axis=1, keepdims=True)
        m_old = m_ref[...]
        m_fin = jnp.maximum(m_old, s_self)
        a_fin = jnp.exp2(m_old - m_fin)
        p_self = jnp.exp2(s_self - m_fin)
        l_fin = a_fin * l_ref[...] + p_self
        acc = (_lane_tile(a_fin, width // LANES) * acc_ref[...]
               + _lane_tile(p_self, width // LANES) * vself_ref[...])
        o8 = acc / _lane_tile(l_fin, width // LANES)
        lam = _diff_lambda(lamp_ref[...], lam_init)
        vhead = lax.broadcasted_iota(jnp.int32, (nh2, width), 1) // vw
        sign = jnp.where((rowh & 1) == 0, 1.0, -lam)
        coef = jnp.where(vhead == rowh // 2, sign, 0.0)
        o = jnp.sum(o8 * coef, axis=0, keepdims=True)
        gain = gain_ref[...]
        outs = [_subln(o[:, h * vw:(h + 1) * vw], gain, lam_init) for h in range(width // vw)]
        o_ref[...] = jnp.concatenate(outs, axis=1)


def _attn_decode(q, k_self, v_self, cache_kt, cache_v2, page_table, lamp, gain, pages, lam_init):
    nseq, width = q.shape
    npages = page_table.shape[1]
    ps = cache_kt.shape[2]
    vw = gain.shape[1]
    nh2 = width // HEAD_DIM
    pt = page_table.reshape(-1)
    q3, k3, v3 = (a.reshape(nseq, 1, width) for a in (q, k_self, v_self))
    seq_spec = pl.BlockSpec((None, 1, width), lambda b, j, pt: (b, 0, 0))

    def page_spec(i, shape):
        return pl.BlockSpec((None,) + shape,
                            lambda b, j, pt: (pt[b * npages + j * pages + i], 0, 0))

    grid_spec = pltpu.PrefetchScalarGridSpec(
        num_scalar_prefetch=1,
        grid=(nseq, npages // pages),
        in_specs=[pl.BlockSpec((4, HEAD_DIM), lambda b, j, pt: (0, 0)), seq_spec, seq_spec, seq_spec,
                  pl.BlockSpec((1, vw), lambda b, j, pt: (0, 0))]
                 + [page_spec(i, cache_kt.shape[1:]) for i in range(pages)]
                 + [page_spec(i, cache_v2.shape[1:]) for i in range(pages)],
        out_specs=seq_spec,
        scratch_shapes=[pltpu.VMEM((nh2, LANES), F32), pltpu.VMEM((nh2, LANES), F32),
                        pltpu.VMEM((nh2, width), F32)],
    )
    out = pl.pallas_call(
        functools.partial(_attn_decode_kernel, pages=pages, lam_init=lam_init, vw=vw),
        out_shape=jax.ShapeDtypeStruct((nseq, 1, width), F32),
        grid_spec=grid_spec,
        compiler_params=_cparams(2),
        name="attn_decode",
    )(pt, lamp, q3, k3, v3, gain, *([cache_kt] * pages), *([cache_v2] * pages))
    return out.reshape(nseq, width)


def _cmul(a, b):
    return a[0] * b[0] - a[1] * b[1], a[0] * b[1] + a[1] * b[0]


def _s5_discretize(a_re, a_im, log_dt, b_re, b_im):
    a_re, a_im = a_re.astype(F32), a_im.astype(F32)
    step = jnp.exp(log_dt.astype(F32))[:, None]
    llb = (a_re * step, a_im * step)
    lb = _s5_power(llb, 1.0)
    num = (lb[0] - 1.0, lb[1])
    den = a_re * a_re + a_im * a_im
    quo = ((num[0] * a_re + num[1] * a_im) / den, (num[1] * a_re - num[0] * a_im) / den)
    b_bar = _cmul((quo[0][:, :, None], quo[1][:, :, None]), (b_re.astype(F32), b_im.astype(F32)))
    return llb, b_bar


def _s5_power(llb, k):
    mag = jnp.exp(llb[0] * k)
    return mag * jnp.cos(llb[1] * k), mag * jnp.sin(llb[1] * k)


def _s5_prompt_tables(params, seq):
    a_re, a_im, log_dt, b_re, b_im, c_re, c_im, d_skip = params
    g, p = a_re.shape
    ch, t = S5_GROUP, S5_CHUNK
    llb, b_bar = _s5_discretize(a_re, a_im, log_dt, b_re, b_im)
    c = (c_re.astype(F32), c_im.astype(F32))
    lags = jnp.arange(t + 1, dtype=F32)[:, None, None]
    powk = _s5_power((llb[0][None], llb[1][None]), lags)
    cp = _cmul((c[0][None], c[1][None]),
               (powk[0][:, :, None, :], powk[1][:, :, None, :]))
    klag = (jnp.einsum('kgcp,gpd->gdkc', cp[0][:t], b_bar[0])
            - jnp.einsum('kgcp,gpd->gdkc', cp[1][:t], b_bar[1]))
    skip = d_skip.astype(F32).reshape(g, ch)[:, None, :] * jnp.eye(ch, dtype=F32)[None]
    klag = klag.at[:, :, 0, :].add(skip)
    idx = jnp.arange(t)
    place = ((idx[None, None, :] - idx[:, None, None] == idx[None, :, None])[:, :, None, :, None]
             & jnp.eye(ch, dtype=bool)[None, None, :, None, :])
    place = place.reshape(t, t * ch, t * ch).astype(F32)
    m_mat = jnp.einsum('xq,sqr->sxr', klag.reshape(g * ch, t * ch), place)
    m_mat = m_mat.reshape(t, g, ch, t * ch).transpose(1, 0, 2, 3).reshape(g, t * ch, t * ch)
    pw_rev = (powk[0][:t][::-1].transpose(1, 0, 2)[:, :, None, :],
              powk[1][:t][::-1].transpose(1, 0, 2)[:, :, None, :])
    bb_t = (b_bar[0].transpose(0, 2, 1)[:, None], b_bar[1].transpose(0, 2, 1)[:, None])
    e_c = _cmul(pw_rev, bb_t)
    e_re = e_c[0].reshape(g, t * ch, p)
    e_im = e_c[1].reshape(g, t * ch, p)
    e_mat = jnp.concatenate([e_re, e_im, e_im, e_re], axis=-1)
    f_re = cp[0][1:t + 1].transpose(1, 0, 2, 3).reshape(g, t * ch, p)
    f_im = cp[1][1:t + 1].transpose(1, 0, 2, 3).reshape(g, t * ch, p)
    f_mat = jnp.concatenate([f_re, -f_im], axis=-1).transpose(0, 2, 1)
    ks = (jnp.arange(1, SUBLANES + 1, dtype=F32) * t)[None, :, None]
    pw = _s5_power((llb[0][:, None, :], llb[1][:, None, :]), ks)
    a1 = jnp.concatenate([pw[0], pw[0]], axis=-1)
    a2 = jnp.concatenate([-pw[1], pw[1]], axis=-1)
    rows = jnp.arange(SUBLANES)[None, :, None]
    parts = []
    for d in S5_SCAN_STRIDES:
        parts.append(jnp.where(rows >= d, a1[:, d - 1:d, :], 0.0))
        parts.append(jnp.where(rows >= d, a2[:, d - 1:d, :], 0.0))
    tab = jnp.concatenate(parts + [a1, a2], axis=1)
    return m_mat.astype(BF16), e_mat.astype(BF16), f_mat.astype(BF16), tab


def _s5_lane_permutation():
    n = SUBLANES * SUBLANES * S5_GROUP
    idx = jnp.arange(n).reshape(SUBLANES, SUBLANES, S5_GROUP).transpose(1, 0, 2).reshape(n)
    return jax.nn.one_hot(idx, n, dtype=BF16)


def _s5_prompt_kernel(u_ref, perm_ref, e_ref, m_ref, f_ref, tab_ref, y_ref, hl_ref,
                      us_ref, ee_ref, h_ref, ycat_ref):
    t = S5_CHUNK
    ng = SUBLANES
    nsub = u_ref.shape[0] // t
    sw = tab_ref.shape[2]
    half_t = t // 2
    perm = perm_ref[...]
    for half in range(2):
        xs = [u_ref[pl.ds(half * half_t + s, nsub, stride=t), :].astype(BF16) for s in range(half_t)]
        uh = jnp.dot(jnp.concatenate(xs, axis=1), perm, preferred_element_type=F32).astype(BF16)
        for gl in range(ng):
            us_ref[gl, :, half * LANES:(half + 1) * LANES] = uh[:, gl * LANES:(gl + 1) * LANES]
    for gl in range(ng):
        ee_ref[gl] = jnp.dot(us_ref[gl], e_ref[gl], preferred_element_type=F32)
        h_ref[gl, :SUBLANES, :] = jnp.zeros((SUBLANES, sw), F32)

    def step(i, carry):
        row = pl.multiple_of(i * SUBLANES, SUBLANES)
        out = []
        for gl in range(ng):
            h = ee_ref[gl, pl.ds(row, SUBLANES), :sw]
            hs = ee_ref[gl, pl.ds(row, SUBLANES), sw:]
            for di, d in enumerate(S5_SCAN_STRIDES):
                t1 = tab_ref[gl, 2 * di * SUBLANES:(2 * di + 1) * SUBLANES, :]
                t2 = tab_ref[gl, (2 * di + 1) * SUBLANES:(2 * di + 2) * SUBLANES, :]
                sh = pltpu.roll(h, d, 0)
                shs = pltpu.roll(hs, d, 0)
                h, hs = h + t1 * sh + t2 * shs, hs + t1 * shs - t2 * sh
            nd = 2 * len(S5_SCAN_STRIDES)
            q1 = tab_ref[gl, nd * SUBLANES:(nd + 1) * SUBLANES, :]
            q2 = tab_ref[gl, (nd + 1) * SUBLANES:(nd + 2) * SUBLANES, :]
            cb, cbs = carry[2 * gl], carry[2 * gl + 1]
            hf = h + q1 * cb + q2 * cbs
            hfs = hs + q1 * cbs - q2 * cb
            h_ref[gl, pl.ds(SUBLANES + row, SUBLANES), :] = hf
            out.append(jnp.broadcast_to(hf[SUBLANES - 1:SUBLANES, :], hf.shape))
            out.append(jnp.broadcast_to(hfs[SUBLANES - 1:SUBLANES, :], hfs.shape))
        return tuple(out)

    zero = jnp.zeros((SUBLANES, sw), F32)
    carry = lax.fori_loop(0, nsub // SUBLANES, step, (zero,) * (2 * ng))
    for gl in range(ng):
        hl_ref[gl:gl + 1, :] = carry[2 * gl][0:1, :]
        hin = h_ref[gl, pl.ds(SUBLANES - 1, nsub), :].astype(BF16)
        y = (jnp.dot(us_ref[gl], m_ref[gl], preferred_element_type=F32)
             + jnp.dot(hin, f_ref[gl], preferred_element_type=F32))
        for half in range(2):
            ycat_ref[half, :, gl * LANES:(gl + 1) * LANES] = y[:, half * LANES:(half + 1) * LANES]
    for half in range(2):
        yc = ycat_ref[half]
        hi = yc.astype(BF16)
        lo = (yc - hi.astype(F32)).astype(BF16)
        z = (jnp.dot(hi, perm, preferred_element_type=F32)
             + jnp.dot(lo, perm, preferred_element_type=F32))
        for s in range(half_t):
            y_ref[pl.ds(half * half_t + s, nsub, stride=t), :] = z[:, s * LANES:(s + 1) * LANES]


def _s5_prompt(u, tables, bsz, seq):
    m_mat, e_mat, f_mat, tab = tables
    g = m_mat.shape[0]
    p = f_mat.shape[1] // 2
    t, ch = S5_CHUNK, S5_GROUP
    ng = LANES // ch
    assert ng == SUBLANES and t * ch == 2 * LANES
    n, w = u.shape
    nsub = seq // t
    perm = _s5_lane_permutation()
    npm = perm.shape[0]
    ntab = tab.shape[1]
    slab = lambda s, b: (s, 0, 0)
    y, hl = pl.pallas_call(
        _s5_prompt_kernel,
        out_shape=(jax.ShapeDtypeStruct((n, w), F32), jax.ShapeDtypeStruct((bsz, g, 2 * p), F32)),
        grid=(w // LANES, bsz),
        in_specs=[pl.BlockSpec((seq, LANES), lambda s, b: (b, s)),
                  pl.BlockSpec((npm, npm), lambda s, b: (0, 0)),
                  pl.BlockSpec((ng, t * ch, 4 * p), slab),
                  pl.BlockSpec((ng, t * ch, t * ch), slab),
                  pl.BlockSpec((ng, 2 * p, t * ch), slab),
                  pl.BlockSpec((ng, ntab, 2 * p), slab)],
        out_specs=(pl.BlockSpec((seq, LANES), lambda s, b: (b, s)),
                   pl.BlockSpec((None, ng, 2 * p), lambda s, b: (b, s, 0))),
        scratch_shapes=[pltpu.VMEM((ng, nsub, t * ch), BF16),
                        pltpu.VMEM((ng, nsub, 4 * p), F32),
                        pltpu.VMEM((ng, SUBLANES + nsub, 2 * p), F32),
                        pltpu.VMEM((2, nsub, ng * LANES), F32)],
        compiler_params=_cparams(2),
        name="s5_prompt",
    )(u, perm, e_mat, m_mat, f_mat, tab)
    return y, hl[..., :p], hl[..., p:]


def _s5_step_kernel(u_ref, h_ref, hs_ref, bt_ref, ct_ref, a1_ref, a2_ref, d_ref, y_ref, hn_ref):
    hp = lax.Precision.HIGHEST
    u = u_ref[...]
    bu = jnp.einsum('gbc,gcp->gbp', u, bt_ref[...], precision=hp, preferred_element_type=F32)
    hn = a1_ref[...] * h_ref[...] + a2_ref[...] * hs_ref[...] + bu
    hn_ref[...] = hn
    y = jnp.einsum('gbp,gpc->gbc', hn, ct_ref[...], precision=hp, preferred_element_type=F32)
    y_ref[...] = y + d_ref[...] * u


def _s5_step(u, h0_re, h0_im, params):
    a_re, a_im, log_dt, b_re, b_im, c_re, c_im, d_skip = params
    g, p = a_re.shape
    ch = S5_GROUP
    bsz = u.shape[0]
    llb, b_bar = _s5_discretize(a_re, a_im, log_dt, b_re, b_im)
    lam_bar = _s5_power(llb, 1.0)
    bt = jnp.concatenate([b_bar[0].transpose(0, 2, 1), b_bar[1].transpose(0, 2, 1)], axis=-1)
    ct = jnp.concatenate([c_re.astype(F32), -c_im.astype(F32)], axis=-1).transpose(0, 2, 1)
    a1 = jnp.concatenate([lam_bar[0]] * 2, axis=-1)[:, None, :]
    a2 = jnp.concatenate([-lam_bar[1], lam_bar[1]], axis=-1)[:, None, :]
    h = jnp.concatenate([h0_re, h0_im], axis=-1).transpose(1, 0, 2).astype(F32)
    hs = jnp.concatenate([h0_im, h0_re], axis=-1).transpose(1, 0, 2).astype(F32)
    ug = u.reshape(bsz, g, ch).transpose(1, 0, 2)
    dg = d_skip.astype(F32).reshape(g, 1, ch)
    y, hn = pl.pallas_call(
        _s5_step_kernel,
        out_shape=(jax.ShapeDtypeStruct((g, bsz, ch), F32), jax.ShapeDtypeStruct((g, bsz, 2 * p), F32)),
        compiler_params=pltpu.CompilerParams(vmem_limit_bytes=VMEM_LIMIT),
        name="s5_step",
    )(ug, h, hs, bt, ct, a1, a2, dg)
    hn = hn.transpose(1, 0, 2)
    return y.transpose(1, 0, 2).reshape(bsz, g * ch), hn[..., :p], hn[..., p:]


def _resident(a):
    zeros = (0,) * a.ndim
    return pl.BlockSpec(a.shape, lambda *_: zeros, pipeline_mode=pl.Buffered(1))


def _ffn_apply(x, ffn_refs, final_norm):
    g_ref, wg_ref, wu_ref, wd_ref, gf_ref = ffn_refs
    hn = _rms(x, g_ref[...], NORM_EPS).astype(BF16)
    hidden = wg_ref.shape[1]
    th = MXU_DIM if hidden % MXU_DIM == 0 else hidden
    acc = None
    for c in range(hidden // th):
        cols = slice(c * th, (c + 1) * th)
        gate = jnp.dot(hn, wg_ref[:, cols], preferred_element_type=F32)
        up = jnp.dot(hn, wu_ref[:, cols], preferred_element_type=F32)
        act = (jax.nn.silu(gate) * up).astype(BF16)
        part = jnp.dot(act, wd_ref[cols, :], preferred_element_type=F32)
        acc = part if acc is None else acc + part
    out = x + acc
    if final_norm:
        out = _rms(out, gf_ref[...], NORM_EPS)
    return out


def _ffn_specs(ffn_w):
    return [_resident(a) for a in ffn_w]


def _ffn_kernel(x_ref, *rest, final_norm):
    o_ref = rest[-1]
    o_ref[...] = _ffn_apply(x_ref[...], rest[:-1], final_norm)


def _ffn(x, ffn_w, tm, final_norm):
    n, d = x.shape
    return pl.pallas_call(
        functools.partial(_ffn_kernel, final_norm=final_norm),
        out_shape=jax.ShapeDtypeStruct((n, d), F32),
        grid=(n // tm,),
        in_specs=[pl.BlockSpec((tm, d), lambda i: (i, 0))] + _ffn_specs(ffn_w),
        out_specs=pl.BlockSpec((tm, d), lambda i: (i, 0)),
        compiler_params=_cparams(1),
        name="ffn",
    )(x, *ffn_w)


def _post_ab_kernel(x_ref, y_ref, a_ref, wglu_ref, bglu_ref, wo_ref, *rest, s5w, final_norm):
    o_ref = rest[-1]
    z = jax.nn.gelu(y_ref[...])
    gl = jnp.dot(z.astype(BF16), wglu_ref[...], preferred_element_type=F32) + bglu_ref[...]
    s5o = (z * jax.nn.sigmoid(gl)).astype(BF16)
    mix = (jnp.dot(s5o, wo_ref[:s5w, :], preferred_element_type=F32)
           + jnp.dot(a_ref[...].astype(BF16), wo_ref[s5w:, :], preferred_element_type=F32))
    o_ref[...] = _ffn_apply(x_ref[...] + mix, rest[:-1], final_norm)


def _post_ab(x, y, attn, wglu_bf, bglu, wo_bf, ffn_w, tm, final_norm):
    n, d = x.shape
    s5w = y.shape[1]
    aw = attn.shape[1]
    row = lambda i: (i, 0)
    return pl.pallas_call(
        functools.partial(_post_ab_kernel, s5w=s5w, final_norm=final_norm),
        out_shape=jax.ShapeDtypeStruct((n, d), F32),
        grid=(n // tm,),
        in_specs=[pl.BlockSpec((tm, d), row), pl.BlockSpec((tm, s5w), row), pl.BlockSpec((tm, aw), row),
                  _resident(wglu_bf), _resident(bglu), _resident(wo_bf)] + _ffn_specs(ffn_w),
        out_specs=pl.BlockSpec((tm, d), row),
        compiler_params=_cparams(1),
        name="post_ab_ffn",
    )(x, y, attn, wglu_bf, bglu, wo_bf, *ffn_w)


def _lru_gates(xc, wg_ref, ba, bx, lam_param):
    xcb = xc.astype(BF16)
    halves = [jnp.dot(xcb[:, h * MXU_DIM:(h + 1) * MXU_DIM], wg_ref[h], preferred_element_type=F32)
              for h in range(wg_ref.shape[0])]
    r_pre = jnp.concatenate([hv[:, :MXU_DIM] for hv in halves], axis=1)
    i_pre = jnp.concatenate([hv[:, MXU_DIM:] for hv in halves], axis=1)
    r = jax.nn.sigmoid(r_pre + ba)
    i = jax.nn.sigmoid(i_pre + bx)
    log_a = -LRU_C * r * jax.nn.softplus(-lam_param)
    a = jnp.exp(log_a)
    b = jnp.sqrt(-jnp.tanh(log_a) * (a * a + 1.0)) * (i * xc)
    return a, b


def _pool_project(pooled, pw_ref, scale):
    pb = pooled.astype(BF16)
    halves = [jnp.dot(pb[:, h * MXU_DIM:(h + 1) * MXU_DIM], pw_ref[h], preferred_element_type=F32)
              for h in range(pw_ref.shape[0])]
    return jnp.concatenate(halves, axis=1) * scale


def _cd_prompt_kernel(x_ref, g_ref, win_ref, cw_ref, cb_ref, wg_ref, ba_ref, bx_ref, lam_ref,
                      pw_ref, psc_ref, wo_ref, fg_ref, fwg_ref, fwu_ref, fwd_ref, fgf_ref,
                      o_ref, conv_ref, lru_ref, pool_ref,
                      xl_ref, xp_ref, s2_ref, s4_ref, a_ref, b_ref, hs_ref, carry_ref,
                      *, tm, w, final_norm):
    ti = pl.program_id(1)
    hist = POOL_BUF + 1
    hp = hist + SUBLANES

    @pl.when(ti == 0)
    def _():
        xl_ref[:hist, :] = jnp.zeros((hist, w), F32)
        xp_ref[:hp, :] = jnp.zeros((hp, w), F32)
        s2_ref[:SUBLANES, :] = jnp.zeros((SUBLANES, w), F32)
        s4_ref[:SUBLANES, :] = jnp.zeros((SUBLANES, w), F32)
        carry_ref[...] = jnp.zeros(carry_ref.shape, F32)

    x = x_ref[...]
    hn = _rms(x, g_ref[...], NORM_EPS).astype(BF16)
    proj = jnp.dot(hn, win_ref[...], preferred_element_type=F32)
    gate = proj[:, :w]
    xl_ref[hist:, :] = proj[:, w:2 * w]
    xp_ref[hp:, :] = proj[:, 2 * w:]

    xc = cb_ref[...] + xl_ref[pl.ds(hist - (CONV_WIDTH - 1), tm), :] * cw_ref[0:1, :]
    for j in range(1, CONV_WIDTH):
        xc = xc + xl_ref[pl.ds(hist - (CONV_WIDTH - 1) + j, tm), :] * cw_ref[j:j + 1, :]

    a, b = _lru_gates(xc, wg_ref, ba_ref[...], bx_ref[...], lam_ref[...])
    a_ref[...] = a
    b_ref[...] = b

    def group(gi, carry):
        row = pl.multiple_of(gi * SUBLANES, SUBLANES)
        ag = a_ref[pl.ds(row, SUBLANES), :]
        bg = b_ref[pl.ds(row, SUBLANES), :]
        ridx = lax.broadcasted_iota(jnp.int32, ag.shape, 0)
        for d in (1, 2, 4):
            a_sh = jnp.where(ridx >= d, pltpu.roll(ag, d, 0), 1.0)
            b_sh = jnp.where(ridx >= d, pltpu.roll(bg, d, 0), 0.0)
            bg = ag * b_sh + bg
            ag = ag * a_sh
        h = bg + ag * carry
        hs_ref[pl.ds(row, SUBLANES), :] = h
        return jnp.broadcast_to(h[SUBLANES - 1:SUBLANES, :], h.shape)

    carry = lax.fori_loop(0, tm // SUBLANES, group, jnp.broadcast_to(carry_ref[...], (SUBLANES, w)))
    carry_ref[...] = carry[0:1, :]
    lru_out = jax.nn.gelu(gate) * hs_ref[...]

    assert POOL_WINDOWS == (2, 4, 8, 16)
    pg = w // len(POOL_WINDOWS)
    span = tm + hist
    s2_ref[pl.ds(SUBLANES, span), :] = (xp_ref[pl.ds(SUBLANES, span), :]
                                        + xp_ref[pl.ds(SUBLANES - 1, span), :])
    s4_ref[pl.ds(SUBLANES, span), pg:] = (s2_ref[pl.ds(SUBLANES, span), pg:]
                                          + s2_ref[pl.ds(SUBLANES - 2, span), pg:])
    s2_ref[pl.ds(SUBLANES, span), 2 * pg:] = (s4_ref[pl.ds(SUBLANES, span), 2 * pg:]
                                              + s4_ref[pl.ds(SUBLANES - 4, span), 2 * pg:])
    sums = (s2_ref[pl.ds(hp, tm), :pg],
            s4_ref[pl.ds(hp, tm), pg:2 * pg],
            s2_ref[pl.ds(hp, tm), 2 * pg:3 * pg],
            s2_ref[pl.ds(hp, tm), 3 * pg:] + s2_ref[pl.ds(hp - SUBLANES, tm), 3 * pg:])
    pos = ti * tm + lax.broadcasted_iota(jnp.int32, (tm, pg), 0)
    pooled = []
    for gidx, win in enumerate(POOL_WINDOWS):
        cur = xp_ref[pl.ds(hp, tm), gidx * pg:(gidx + 1) * pg]
        cnt = jnp.minimum(pos + 1, win).astype(F32)
        pooled.append(sums[gidx] / cnt - cur)
    pool_out = _pool_project(jnp.concatenate(pooled, axis=1), pw_ref, psc_ref[...])

    mix = (jnp.dot(lru_out.astype(BF16), wo_ref[:w, :], preferred_element_type=F32)
           + jnp.dot(pool_out.astype(BF16), wo_ref[w:, :], preferred_element_type=F32))
    o_ref[...] = _ffn_apply(x + mix, (fg_ref, fwg_ref, fwu_ref, fwd_ref, fgf_ref), final_norm)

    conv_ref[...] = xl_ref[pl.ds(hist + tm - (CONV_WIDTH - 1), CONV_WIDTH - 1), :]
    lru_ref[...] = carry_ref[...]
    pool_ref[...] = xp_ref[pl.ds(hp + tm - POOL_BUF, POOL_BUF), :]
    xl_ref[:hist, :] = xl_ref[pl.ds(tm, hist), :]
    xp_ref[pl.ds(SUBLANES, hist), :] = xp_ref[pl.ds(SUBLANES + tm, hist), :]


def _cd_prompt(x, g, cdw, ffn_w, bsz, seq, tm, final_norm):
    n, d = x.shape
    w = cdw[1].shape[1]
    nt = seq // tm
    hist = POOL_BUF + 1
    row = lambda b, i: (b * nt + i, 0)
    st = lambda b, i: (b, 0, 0)
    return pl.pallas_call(
        functools.partial(_cd_prompt_kernel, tm=tm, w=w, final_norm=final_norm),
        out_shape=(jax.ShapeDtypeStruct((n, d), F32),
                   jax.ShapeDtypeStruct((bsz, CONV_WIDTH - 1, w), F32),
                   jax.ShapeDtypeStruct((bsz, 1, w), F32),
                   jax.ShapeDtypeStruct((bsz, POOL_BUF, w), F32)),
        grid=(bsz, nt),
        in_specs=[pl.BlockSpec((tm, d), row), _resident(g)] + [_resident(a) for a in cdw]
                 + _ffn_specs(ffn_w),
        out_specs=(pl.BlockSpec((tm, d), row),
                   pl.BlockSpec((None, CONV_WIDTH - 1, w), st),
                   pl.BlockSpec((None, 1, w), st),
                   pl.BlockSpec((None, POOL_BUF, w), st)),
        scratch_shapes=[pltpu.VMEM((hist + tm, w), F32)]
                       + [pltpu.VMEM((SUBLANES + hist + tm, w), F32)] * 3
                       + [pltpu.VMEM((tm, w), F32)] * 3 + [pltpu.VMEM((1, w), F32)],
        compiler_params=_cparams(2),
        name="cd_prompt_ffn",
    )(x, g, *cdw, *ffn_w)


def _cd_step_kernel(x_ref, g_ref, win_ref, cw_ref, cb_ref, wg_ref, ba_ref, bx_ref, lam_ref,
                    pw_ref, psc_ref, wo_ref, sc_ref, sl_ref, sp_ref,
                    o_ref, conv_ref, lru_ref, pool_ref, *, w):
    x = x_ref[...]
    hn = _rms(x, g_ref[...], NORM_EPS).astype(BF16)
    proj = jnp.dot(hn, win_ref[...], preferred_element_type=F32)
    gate = proj[:, :w]
    xl = proj[:, w:2 * w]
    xp = proj[:, 2 * w:]
    xc = cb_ref[...] + xl * cw_ref[CONV_WIDTH - 1:CONV_WIDTH, :]
    for j in range(CONV_WIDTH - 1):
        xc = xc + sc_ref[j] * cw_ref[j:j + 1, :]
    a, b = _lru_gates(xc, wg_ref, ba_ref[...], bx_ref[...], lam_ref[...])
    h = a * sl_ref[...] + b
    lru_out = jax.nn.gelu(gate) * h

    pg = w // len(POOL_WINDOWS)
    pooled = []
    for gidx, win in enumerate(POOL_WINDOWS):
        cols = slice(gidx * pg, (gidx + 1) * pg)
        cur = xp[:, cols]
        tot = cur
        for kk in range(1, win):
            tot = tot + sp_ref[POOL_BUF - kk][:, cols]
        pooled.append(tot / float(win) - cur)
    pool_out = _pool_project(jnp.concatenate(pooled, axis=1), pw_ref, psc_ref[...])

    mix = (jnp.dot(lru_out.astype(BF16), wo_ref[:w, :], preferred_element_type=F32)
           + jnp.dot(pool_out.astype(BF16), wo_ref[w:, :], preferred_element_type=F32))
    o_ref[...] = x + mix
    for j in range(CONV_WIDTH - 2):
        conv_ref[j] = sc_ref[j + 1]
    conv_ref[CONV_WIDTH - 2] = xl
    lru_ref[...] = h
    for j in range(POOL_BUF - 1):
        pool_ref[j] = sp_ref[j + 1]
    pool_ref[POOL_BUF - 1] = xp


def _cd_step(x, g, cdw, conv_buf, lru_h0, pool_buf):
    n, d = x.shape
    win_bf, cw, cb, wg_bf, ba, bx, lam, pw_bf, psc, wo_bf = cdw
    w = cw.shape[1]
    sc = conv_buf.transpose(1, 0, 2)
    sp = pool_buf.transpose(1, 0, 2)
    o, conv_n, lru_n, pool_n = pl.pallas_call(
        functools.partial(_cd_step_kernel, w=w),
        out_shape=(jax.ShapeDtypeStruct((n, d), F32),
                   jax.ShapeDtypeStruct((CONV_WIDTH - 1, n, w), F32),
                   jax.ShapeDtypeStruct((n, w), F32),
                   jax.ShapeDtypeStruct((POOL_BUF, n, w), F32)),
        compiler_params=pltpu.CompilerParams(vmem_limit_bytes=VMEM_LIMIT),
        name="cd_step",
    )(x, g, win_bf, cw, cb, wg_bf, ba, bx, lam, pw_bf, psc, wo_bf, sc, lru_h0, sp)
    return o, conv_n.transpose(1, 0, 2), lru_n, pool_n.transpose(1, 0, 2)


def _block_diag_tiles(w, tile):
    nb, c, _ = w.shape
    per = tile // c
    w4 = w.reshape(nb // per, per, c, c)
    placed = jnp.einsum('tpij,pq->tpiqj', w4, jnp.eye(per, dtype=w.dtype))
    return placed.reshape(nb // per, tile, tile)


def _rope_tables(pos, rows):
    half = HEAD_DIM // 2
    inv = ROPE_THETA ** (-jnp.arange(half, dtype=F32) / half)
    ang = pos.astype(F32)[:, None] * inv[None, :]
    cos = jnp.cos(ang)
    sin = jnp.sin(ang)
    reps = LANES // HEAD_DIM
    cos_t = jnp.tile(jnp.concatenate([cos, cos], axis=1), (1, reps))
    sin_t = jnp.tile(jnp.concatenate([-sin, sin], axis=1), (1, reps))
    if cos_t.shape[0] != rows:
        cos_t = jnp.broadcast_to(cos_t, (rows, LANES))
        sin_t = jnp.broadcast_to(sin_t, (rows, LANES))
    return cos_t, sin_t


def _row_tile(n, target):
    t = min(n, target)
    while n % t:
        t //= 2
    return t


def kernel(x_prompt, x_sample, cache_k, cache_v, page_table, state_s5_re, state_s5_im, state_conv, state_lru, state_pool, norm_mix, norm_ffn, norm_final, w_in_ab, w_out_ab, s5_a_re, s5_a_im, s5_log_dt, s5_b_re, s5_b_im, s5_c_re, s5_c_im, s5_d, s5_w_glu, s5_b_glu, diff_lq1, diff_lk1, diff_lq2, diff_lk2, diff_subln, w_in_cd, w_out_cd, conv_w, conv_b, lru_wa, lru_ba, lru_wx, lru_bx, lru_lambda, pool_w, pool_scale, ffn_w_gate, ffn_w_up, ffn_w_down):
    bsz, seq, d = x_prompt.shape
    nseq, dec_seq, _ = x_sample.shape
    assert dec_seq == 1
    depth = norm_mix.shape[0]
    npages = page_table.shape[1]
    page_size = cache_k.shape[2]
    past_len = npages * page_size
    assert past_len >= POOL_BUF
    s5w = s5_d.shape[1]
    qkw = cache_k.shape[3] * cache_k.shape[4]
    vw_total = cache_v.shape[3] * cache_v.shape[4]
    heads = cache_v.shape[3]
    n_p = bsz * seq
    row2 = lambda a: a.reshape(1, -1)

    xp = x_prompt.reshape(n_p, d)
    xs = x_sample.reshape(nseq, d)
    tm_p = _row_tile(seq, 512)
    cos_p, sin_p = _rope_tables(jnp.arange(seq, dtype=jnp.int32), seq)
    cos_s, sin_s = _rope_tables(jnp.full((1,), past_len, jnp.int32), nseq)

    outs_p = {'k': [], 'v': [], 're': [], 'im': [], 'conv': [], 'lru': [], 'pool': []}
    outs_s = {'k': [], 'v': [], 're': [], 'im': [], 'conv': [], 'lru': [], 'pool': []}

    for l in range(depth):
        j = l // 2
        g_mix = row2(norm_mix[l])
        last = l == depth - 1
        ffn_w = (row2(norm_ffn[l]), ffn_w_gate[l].astype(BF16), ffn_w_up[l].astype(BF16),
                 ffn_w_down[l].astype(BF16), row2(norm_final))
        if l % 2 == 0:
            lam_init = 0.8 - 0.6 * math.exp(-0.3 * l)
            w_in = w_in_ab[j].astype(BF16)
            w_out = w_out_ab[j].astype(BF16)
            w_glu = s5_w_glu[j].astype(BF16)
            b_glu = row2(s5_b_glu[j])
            lamp = jnp.stack([diff_lq1[j], diff_lk1[j], diff_lq2[j], diff_lk2[j]]).astype(F32)
            gain = row2(diff_subln[j])
            s5p = (s5_a_re[j], s5_a_im[j], s5_log_dt[j], s5_b_re[j], s5_b_im[j],
                   s5_c_re[j], s5_c_im[j], s5_d[j])
            u, q_bf, k, v, k_bf, v_bf = _inproj_ab(xp, g_mix, w_in, cos_p, sin_p, tm_p, s5w, qkw)
            t_attn = _row_tile(seq, 1024)
            attn = _attn_prompt(q_bf, k_bf, v_bf, lamp, gain, bsz, seq, t_attn, t_attn, lam_init)
            y, h_re, h_im = _s5_prompt(u, _s5_prompt_tables(s5p, seq), bsz, seq)
            xp = _post_ab(xp, y, attn, w_glu, b_glu, w_out, ffn_w, tm_p, last)
            outs_p['k'].append(k.reshape(bsz, seq, qkw // HEAD_DIM, HEAD_DIM))
            outs_p['v'].append(v.reshape(bsz, seq, heads, vw_total // heads))
            outs_p['re'].append(h_re)
            outs_p['im'].append(h_im)
            u, q_bf, k, v, _, _ = _inproj_ab(xs, g_mix, w_in, cos_s, sin_s, nseq, s5w, qkw)
            ck = jnp.transpose(cache_k[j], (0, 2, 3, 1)).reshape(-1, qkw, page_size)
            cv = cache_v[j].reshape(-1, page_size * heads, vw_total // heads)
            attn = _attn_decode(q_bf.astype(F32), k, v.reshape(nseq, vw_total), ck, cv, page_table,
                                lamp, gain,
                                math.gcd(npages, 32), lam_init)
            y, h_re, h_im = _s5_step(u, state_s5_re[j], state_s5_im[j], s5p)
            xs = _post_ab(xs, y, attn, w_glu, b_glu, w_out, ffn_w, nseq, last)
            outs_s['k'].append(k.reshape(nseq, 1, qkw // HEAD_DIM, HEAD_DIM))
            outs_s['v'].append(v.reshape(nseq, 1, heads, vw_total // heads))
            outs_s['re'].append(h_re)
            outs_s['im'].append(h_im)
        else:
            w = conv_w.shape[2]
            wg = jnp.concatenate([_block_diag_tiles(lru_wa[j], MXU_DIM),
                                  _block_diag_tiles(lru_wx[j], MXU_DIM)], axis=2).astype(BF16)
            cdw = (w_in_cd[j].astype(BF16), conv_w[j], row2(conv_b[j]), wg, row2(lru_ba[j]),
                   row2(lru_bx[j]), row2(lru_lambda[j]),
                   _block_diag_tiles(pool_w[j], MXU_DIM).astype(BF16), row2(pool_scale[j]),
                   w_out_cd[j].astype(BF16))
            xp, c_new, h_new, p_new = _cd_prompt(xp, g_mix, cdw, ffn_w, bsz, seq, tm_p, last)
            outs_p['conv'].append(c_new)
            outs_p['lru'].append(h_new.reshape(bsz, w))
            outs_p['pool'].append(p_new)
            xs, c_new, h_new, p_new = _cd_step(xs, g_mix, cdw, state_conv[j], state_lru[j], state_pool[j])
            outs_s['conv'].append(c_new)
            outs_s['lru'].append(h_new)
            outs_s['pool'].append(p_new)
            xs = _ffn(xs, ffn_w, nseq, last)

    sp = {name: jnp.stack(vals) for name, vals in outs_p.items()}
    ss = {name: jnp.stack(vals) for name, vals in outs_s.items()}
    y_prompt = xp.reshape(bsz, seq, d)
    y_sample = xs.reshape(nseq, 1, d)
    return (y_prompt, y_sample, sp['k'], sp['v'], ss['k'], ss['v'], sp['re'], sp['im'], ss['re'], ss['im'],
            sp['conv'], ss['conv'], sp['lru'], ss['lru'], sp['pool'], ss['pool'])
```

```python
import functools
import math

import jax
import jax.numpy as jnp
from jax import lax
from jax.experimental import pallas as pl
from jax.experimental.pallas import tpu as pltpu

F32 = jnp.float32
BF16 = jnp.bfloat16

HEAD_DIM = 64
ROPE_THETA = 10000.0
NORM_EPS = 1e-6
SUBLN_EPS = 1e-5
S5_GROUP = 16
S5_CHUNK = 16
S5_SCAN_STRIDES = (1, 2, 4)
LRU_C = 8.0
CONV_WIDTH = 4
POOL_WINDOWS = (2, 4, 8, 16)
POOL_BUF = max(POOL_WINDOWS) - 1

LANES = 128
SUBLANES = 8
MXU_DIM = 256
VMEM_LIMIT = 56 * 1024 * 1024

NEG_BIG = -1e30
LOG2E = math.log2(math.e)


def _cparams(n_axes, flags=None):
    return pltpu.CompilerParams(dimension_semantics=("arbitrary",) * n_axes,
                                vmem_limit_bytes=VMEM_LIMIT, flags=flags)


def _rms(x, g, eps):
    ms = jnp.mean(x * x, axis=-1, keepdims=True)
    return x * lax.rsqrt(ms + eps) * g


def _lane_tile(x, reps):
    return jnp.concatenate([x] * reps, axis=1)


def _rope(x, cos, sin_signed):
    width = x.shape[1]
    lane = lax.broadcasted_iota(jnp.int32, x.shape, 1)
    half = HEAD_DIM // 2
    rot = jnp.where((lane & half) == 0, pltpu.roll(x, width - half, 1), pltpu.roll(x, half, 1))
    return x * cos + rot * sin_signed


def _inproj_ab_kernel(x_ref, g_ref, w_ref, cos_ref, sin_ref,
                      u_ref, q_ref, k_ref, v_ref, kb_ref, vb_ref, *, s5w, qkw):
    h = _rms(x_ref[...], g_ref[...], NORM_EPS).astype(BF16)
    proj = jnp.dot(h, w_ref[...], preferred_element_type=F32)
    reps = qkw // LANES
    cos = _lane_tile(cos_ref[...], reps)
    sin = _lane_tile(sin_ref[...], reps)
    u_ref[...] = proj[:, :s5w]
    q = _rope(proj[:, s5w:s5w + qkw], cos, sin)
    k = _rope(proj[:, s5w + qkw:s5w + 2 * qkw], cos, sin)
    v = proj[:, s5w + 2 * qkw:]
    q_ref[...] = (q * (HEAD_DIM ** -0.5 * LOG2E)).astype(BF16)
    k_ref[...] = k
    nvh = v.shape[1] // LANES
    for hh in range(nvh):
        v_ref[pl.ds(hh, v.shape[0], stride=nvh), :] = v[:, hh * LANES:(hh + 1) * LANES]
    kb_ref[...] = k.astype(BF16)
    vb_ref[...] = v.astype(BF16)


def _inproj_ab(x, g, w_bf, cos_tab, sin_tab, tm, s5w, qkw):
    n, d = x.shape
    wn = w_bf.shape[1]
    vw = wn - s5w - 2 * qkw
    nvh = vw // LANES
    period = cos_tab.shape[0] // tm
    row = lambda i: (i, 0)
    const = lambda i: (0, 0)
    tab = lambda i: (i % period, 0)
    return pl.pallas_call(
        functools.partial(_inproj_ab_kernel, s5w=s5w, qkw=qkw),
        out_shape=(jax.ShapeDtypeStruct((n, s5w), F32), jax.ShapeDtypeStruct((n, qkw), BF16),
                   jax.ShapeDtypeStruct((n, qkw), F32), jax.ShapeDtypeStruct((n * nvh, LANES), F32),
                   jax.ShapeDtypeStruct((n, qkw), BF16), jax.ShapeDtypeStruct((n, vw), BF16)),
        grid=(n // tm,),
        in_specs=[pl.BlockSpec((tm, d), row), pl.BlockSpec((1, d), const),
                  pl.BlockSpec((d, wn), const), pl.BlockSpec((tm, LANES), tab),
                  pl.BlockSpec((tm, LANES), tab)],
        out_specs=(pl.BlockSpec((tm, s5w), row), pl.BlockSpec((tm, qkw), row),
                   pl.BlockSpec((tm, qkw), row), pl.BlockSpec((tm * nvh, LANES), row),
                   pl.BlockSpec((tm, qkw), row), pl.BlockSpec((tm, vw), row)),
        compiler_params=_cparams(1),
        name="inproj_ab",
    )(x, g, w_bf, cos_tab, sin_tab)


def _diff_lambda(lamp, lam_init):
    s1 = jnp.sum(lamp[0:1] * lamp[1:2], axis=1, keepdims=True)
    s2 = jnp.sum(lamp[2:3] * lamp[3:4], axis=1, keepdims=True)
    return jnp.exp(s1) - jnp.exp(s2) + lam_init


def _subln(o, gain, lam_init):
    ms = jnp.mean(o * o, axis=-1, keepdims=True)
    return o * lax.rsqrt(ms + SUBLN_EPS) * gain * (1.0 - lam_init)


ONES_ROWS = 16


def _attn_prompt_kernel(lamp_ref, q_ref, k_ref, v_ref, gain_ref, o_ref,
                        vt_ref, m_ref, acc_ref, *, tq, tk, lam_init):
    qi = pl.program_id(2)
    seq, vw = v_ref.shape
    per_q = tq // tk

    @pl.when(qi == 0)
    def _():
        for c in range(seq // tk):
            blk = v_ref[c * tk:(c + 1) * tk, :].astype(F32)
            vt_ref[c, :vw, :] = blk.T.astype(BF16)
            vt_ref[c, vw:, :] = jnp.ones((ONES_ROWS, tk), BF16)

    q = q_ref[...]
    lane = lax.broadcasted_iota(jnp.int32, q.shape, 1)
    zero = jnp.zeros_like(q)
    qs = (jnp.where(lane < HEAD_DIM, q, zero), jnp.where(lane >= HEAD_DIM, q, zero))
    m_ref[...] = jnp.full(m_ref.shape, NEG_BIG, F32)
    acc_ref[...] = jnp.zeros(acc_ref.shape, F32)

    def chunk(c, diag):
        off = pl.multiple_of(c * tk, tk)
        kc = k_ref[pl.ds(off, tk), :]
        vc = vt_ref[c]

        def scores(j):
            st = lax.dot_general(kc, qs[j], (((1,), (1,)), ((), ())), preferred_element_type=F32)
            if diag is not None:
                key = lax.broadcasted_iota(jnp.int32, st.shape, 0) + diag * tk
                qry = lax.broadcasted_iota(jnp.int32, st.shape, 1)
                st = jnp.where(key <= qry, st, -jnp.inf)
            return st

        def softmax(j, st):
            m_prev = m_ref[j]
            m_new = jnp.maximum(m_prev, jnp.max(st, axis=0, keepdims=True))
            m_ref[j] = m_new
            return jnp.exp2(m_prev - m_new), jnp.exp2(st - m_new).astype(BF16)

        def accumulate(j, alpha, p):
            pv = jnp.dot(vc, p, preferred_element_type=F32)
            acc_ref[j] = alpha * acc_ref[j] + pv

        alpha0, p0 = softmax(0, scores(0))
        st1 = scores(1)
        accumulate(0, alpha0, p0)
        alpha1, p1 = softmax(1, st1)
        accumulate(1, alpha1, p1)

    def body(c, carry):
        chunk(c, None)
        return carry

    def diagonal_chunk(c):
        h = tk // 2
        off = pl.multiple_of(c * tk, tk)
        k_lo = k_ref[pl.ds(off, h), :]
        k_hi = k_ref[pl.ds(off + h, h), :]
        vc = vt_ref[c]

        def masked_scores(kpart, qpart):
            st = lax.dot_general(kpart, qpart, (((1,), (1,)), ((), ())), preferred_element_type=F32)
            key = lax.broadcasted_iota(jnp.int32, st.shape, 0)
            qry = lax.broadcasted_iota(jnp.int32, st.shape, 1)
            return jnp.where(key <= qry, st, -jnp.inf)

        for j in range(2):
            st_lo = masked_scores(k_lo, qs[j])
            st_hi = masked_scores(k_hi, qs[j][h:, :])
            m_prev = m_ref[j]
            m_lo = jnp.maximum(m_prev, jnp.max(st_lo, axis=0, keepdims=True))
            m_new = jnp.concatenate(
                [m_lo[:, :h], jnp.maximum(m_lo[:, h:], jnp.max(st_hi, axis=0, keepdims=True))], axis=1)
            m_ref[j] = m_new
            alpha = jnp.exp2(m_prev - m_new)
            p_lo = jnp.exp2(st_lo - m_new).astype(BF16)
            p_hi = jnp.exp2(st_hi - m_new[:, h:]).astype(BF16)
            pv = jnp.dot(vc[:, :h], p_lo, preferred_element_type=F32)
            pv_hi = jnp.dot(vc[:, h:], p_hi, preferred_element_type=F32)
            pv = jnp.concatenate([pv[:, :h], pv[:, h:] + pv_hi], axis=1)
            acc_ref[j] = alpha * acc_ref[j] + pv

    lax.fori_loop(0, qi * per_q, body, 0)
    if per_q == 1:
        diagonal_chunk(qi)
    else:
        for d in range(per_q):
            chunk(qi * per_q + d, d)

    lam = _diff_lambda(lamp_ref[...], lam_init)
    a0 = acc_ref[0]
    a1 = acc_ref[1]
    ot = a0[:vw] / a0[vw:vw + 1] - lam * (a1[:vw] / a1[vw:vw + 1])
    o_ref[...] = _subln(ot.T, gain_ref[...], lam_init).astype(o_ref.dtype)


def _attn_prompt(q_bf, k_bf, v_bf, lamp, gain, bsz, seq, tq, tk, lam_init):
    n, qkw = q_bf.shape
    vw_total = v_bf.shape[1]
    heads = qkw // (2 * HEAD_DIM)
    vw = vw_total // heads
    nq = seq // tq
    return pl.pallas_call(
        functools.partial(_attn_prompt_kernel, tq=tq, tk=tk, lam_init=lam_init),
        out_shape=jax.ShapeDtypeStruct((n, vw_total), BF16),
        grid=(bsz, heads, nq),
        in_specs=[pl.BlockSpec((4, HEAD_DIM), lambda b, h, i: (0, 0)),
                  pl.BlockSpec((tq, 2 * HEAD_DIM), lambda b, h, i: (b * nq + i, h)),
                  pl.BlockSpec((seq, 2 * HEAD_DIM), lambda b, h, i: (b, h)),
                  pl.BlockSpec((seq, vw), lambda b, h, i: (b, h)),
                  pl.BlockSpec((1, vw), lambda b, h, i: (0, 0))],
        out_specs=pl.BlockSpec((tq, vw), lambda b, h, i: (b * nq + i, h)),
        scratch_shapes=[pltpu.VMEM((seq // tk, vw + ONES_ROWS, tk), BF16),
                        pltpu.VMEM((2, 1, tq), F32),
                        pltpu.VMEM((2, vw + ONES_ROWS, tq), F32)],
        compiler_params=_cparams(3),
        name="attn_prompt",
    )(lamp, q_bf, k_bf, v_bf, gain)


def _attn_decode_kernel(pt_ref, lamp_ref, q_ref, kself_ref, vself_ref, gain_ref, *rest,
                        pages, lam_init, vw):
    k_refs = rest[:pages]
    v_refs = rest[pages:2 * pages]
    o_ref = rest[2 * pages]
    m_ref, l_ref, acc_ref = rest[2 * pages + 1:]
    j = pl.program_id(1)
    nh2 = m_ref.shape[0]
    width = q_ref.shape[1]

    @pl.when(j == 0)
    def _():
        m_ref[...] = jnp.full(m_ref.shape, NEG_BIG, F32)
        l_ref[...] = jnp.zeros(l_ref.shape, F32)
        acc_ref[...] = jnp.zeros(acc_ref.shape, F32)

    rowh = lax.broadcasted_iota(jnp.int32, (nh2, width), 0)
    laneh = lax.broadcasted_iota(jnp.int32, (nh2, width), 1) // HEAD_DIM
    q8 = jnp.where(rowh == laneh, jnp.broadcast_to(q_ref[...], (nh2, width)), 0.0)
    q8b = q8.astype(BF16)

    s_parts = [jnp.dot(q8b, k_refs[i][...].astype(BF16), preferred_element_type=F32)
               for i in range(pages)]
    s = jnp.concatenate(s_parts, axis=1)
    m_prev = m_ref[...]
    m_new = jnp.maximum(m_prev, jnp.max(s, axis=1, keepdims=True))
    alpha = jnp.exp2(m_prev - m_new)
    p = jnp.exp2(s - _lane_tile(m_new, s.shape[1] // LANES))
    l_ref[...] = alpha * l_ref[...] + jnp.sum(p, axis=1, keepdims=True)
    ps = k_refs[0].shape[1]
    nvh = width // vw
    pb = p.astype(BF16)
    pv = None
    for i in range(pages):
        cols = [jnp.dot(pb[:, i * ps:(i + 1) * ps],
                        v_refs[i][pl.ds(h, ps, stride=nvh), :].astype(BF16),
                        preferred_element_type=F32) for h in range(nvh)]
        part = jnp.concatenate(cols, axis=1)
        pv = part if pv is None else pv + part
    acc_ref[...] = _lane_tile(alpha, width // LANES) * acc_ref[...] + pv
    m_ref[...] = m_new

    @pl.when(j == pl.num_programs(1) - 1)
    def _():
        s_self = jnp.sum(q8 * kself_ref[...], axis=1, keepdims=True)
        m_old = m_ref[...]
        m_fin = jnp.maximum(m_old, s_self)
        a_fin = jnp.exp2(m_old - m_fin)
        p_self = jnp.exp2(s_self - m_fin)
        l_fin = a_fin * l_ref[...] + p_self
        acc = (_lane_tile(a_fin, width // LANES) * acc_ref[...]
               + _lane_tile(p_self, width // LANES) * vself_ref[...])
        o8 = acc / _lane_tile(l_fin, width // LANES)
        lam = _diff_lambda(lamp_ref[...], lam_init)
        vhead = lax.broadcasted_iota(jnp.int32, (nh2, width), 1) // vw
        sign = jnp.where((rowh & 1) == 0, 1.0, -lam)
        coef = jnp.where(vhead == rowh // 2, sign, 0.0)
        o = jnp.sum(o8 * coef, axis=0, keepdims=True)
        gain = gain_ref[...]
        outs = [_subln(o[:, h * vw:(h + 1) * vw], gain, lam_init) for h in range(width // vw)]
        o_ref[...] = jnp.concatenate(outs, axis=1)


def _attn_decode(q, k_self, v_self, cache_kt, cache_v2, page_table, lamp, gain, pages, lam_init):
    nseq, width = q.shape
    npages = page_table.shape[1]
    ps = cache_kt.shape[2]
    vw = gain.shape[1]
    nh2 = width // HEAD_DIM
    pt = page_table.reshape(-1)
    q3, k3, v3 = (a.reshape(nseq, 1, width) for a in (q, k_self, v_self))
    seq_spec = pl.BlockSpec((None, 1, width), lambda b, j, pt: (b, 0, 0))

    def page_spec(i, shape):
        return pl.BlockSpec((None,) + shape,
                            lambda b, j, pt: (pt[b * npages + j * pages + i], 0, 0))

    grid_spec = pltpu.PrefetchScalarGridSpec(
        num_scalar_prefetch=1,
        grid=(nseq, npages // pages),
        in_specs=[pl.BlockSpec((4, HEAD_DIM), lambda b, j, pt: (0, 0)), seq_spec, seq_spec, seq_spec,
                  pl.BlockSpec((1, vw), lambda b, j, pt: (0, 0))]
                 + [page_spec(i, cache_kt.shape[1:]) for i in range(pages)]
                 + [page_spec(i, cache_v2.shape[1:]) for i in range(pages)],
        out_specs=seq_spec,
        scratch_shapes=[pltpu.VMEM((nh2, LANES), F32), pltpu.VMEM((nh2, LANES), F32),
                        pltpu.VMEM((nh2, width), F32)],
    )
    out = pl.pallas_call(
        functools.partial(_attn_decode_kernel, pages=pages, lam_init=lam_init, vw=vw),
        out_shape=jax.ShapeDtypeStruct((nseq, 1, width), F32),
        grid_spec=grid_spec,
        compiler_params=_cparams(2),
        name="attn_decode",
    )(pt, lamp, q3, k3, v3, gain, *([cache_kt] * pages), *([cache_v2] * pages))
    return out.reshape(nseq, width)


def _cmul(a, b):
    return a[0] * b[0] - a[1] * b[1], a[0] * b[1] + a[1] * b[0]


def _s5_discretize(a_re, a_im, log_dt, b_re, b_im):
    a_re, a_im = a_re.astype(F32), a_im.astype(F32)
    step = jnp.exp(log_dt.astype(F32))[:, None]
    llb = (a_re * step, a_im * step)
    lb = _s5_power(llb, 1.0)
    num = (lb[0] - 1.0, lb[1])
    den = a_re * a_re + a_im * a_im
    quo = ((num[0] * a_re + num[1] * a_im) / den, (num[1] * a_re - num[0] * a_im) / den)
    b_bar = _cmul((quo[0][:, :, None], quo[1][:, :, None]), (b_re.astype(F32), b_im.astype(F32)))
    return llb, b_bar


def _s5_power(llb, k):
    mag = jnp.exp(llb[0] * k)
    return mag * jnp.cos(llb[1] * k), mag * jnp.sin(llb[1] * k)


def _s5_prompt_tables(params, seq):
    a_re, a_im, log_dt, b_re, b_im, c_re, c_im, d_skip = params
    g, p = a_re.shape
    ch, t = S5_GROUP, S5_CHUNK
    llb, b_bar = _s5_discretize(a_re, a_im, log_dt, b_re, b_im)
    c = (c_re.astype(F32), c_im.astype(F32))
    lags = jnp.arange(t + 1, dtype=F32)[:, None, None]
    powk = _s5_power((llb[0][None], llb[1][None]), lags)
    cp = _cmul((c[0][None], c[1][None]),
               (powk[0][:, :, None, :], powk[1][:, :, None, :]))
    klag = (jnp.einsum('kgcp,gpd->gdkc', cp[0][:t], b_bar[0])
            - jnp.einsum('kgcp,gpd->gdkc', cp[1][:t], b_bar[1]))
    skip = d_skip.astype(F32).reshape(g, ch)[:, None, :] * jnp.eye(ch, dtype=F32)[None]
    klag = klag.at[:, :, 0, :].add(skip)
    idx = jnp.arange(t)
    place = ((idx[None, None, :] - idx[:, None, None] == idx[None, :, None])[:, :, None, :, None]
             & jnp.eye(ch, dtype=bool)[None, None, :, None, :])
    place = place.reshape(t, t * ch, t * ch).astype(F32)
    m_mat = jnp.einsum('xq,sqr->sxr', klag.reshape(g * ch, t * ch), place)
    m_mat = m_mat.reshape(t, g, ch, t * ch).transpose(1, 0, 2, 3).reshape(g, t * ch, t * ch)
    pw_rev = (powk[0][:t][::-1].transpose(1, 0, 2)[:, :, None, :],
              powk[1][:t][::-1].transpose(1, 0, 2)[:, :, None, :])
    bb_t = (b_bar[0].transpose(0, 2, 1)[:, None], b_bar[1].transpose(0, 2, 1)[:, None])
    e_c = _cmul(pw_rev, bb_t)
    e_re = e_c[0].reshape(g, t * ch, p)
    e_im = e_c[1].reshape(g, t * ch, p)
    e_mat = jnp.concatenate([e_re, e_im, e_im, e_re], axis=-1)
    f_re = cp[0][1:t + 1].transpose(1, 0, 2, 3).reshape(g, t * ch, p)
    f_im = cp[1][1:t + 1].transpose(1, 0, 2, 3).reshape(g, t * ch, p)
    f_mat = jnp.concatenate([f_re, -f_im], axis=-1).transpose(0, 2, 1)
    ks = (jnp.arange(1, SUBLANES + 1, dtype=F32) * t)[None, :, None]
    pw = _s5_power((llb[0][:, None, :], llb[1][:, None, :]), ks)
    a1 = jnp.concatenate([pw[0], pw[0]], axis=-1)
    a2 = jnp.concatenate([-pw[1], pw[1]], axis=-1)
    rows = jnp.arange(SUBLANES)[None, :, None]
    parts = []
    for d in S5_SCAN_STRIDES:
        parts.append(jnp.where(rows >= d, a1[:, d - 1:d, :], 0.0))
        parts.append(jnp.where(rows >= d, a2[:, d - 1:d, :], 0.0))
    tab = jnp.concatenate(parts + [a1, a2], axis=1)
    return m_mat.astype(BF16), e_mat.astype(BF16), f_mat.astype(BF16), tab


def _s5_lane_permutation():
    n = SUBLANES * SUBLANES * S5_GROUP
    idx = jnp.arange(n).reshape(SUBLANES, SUBLANES, S5_GROUP).transpose(1, 0, 2).reshape(n)
    return jax.nn.one_hot(idx, n, dtype=BF16)


def _s5_prompt_kernel(u_ref, perm_ref, e_ref, m_ref, f_ref, tab_ref, y_ref, hl_ref,
                      us_ref, ee_ref, h_ref, ycat_ref):
    t = S5_CHUNK
    ng = SUBLANES
    nsub = u_ref.shape[0] // t
    sw = tab_ref.shape[2]
    half_t = t // 2
    perm = perm_ref[...]
    for half in range(2):
        xs = [u_ref[pl.ds(half * half_t + s, nsub, stride=t), :].astype(BF16) for s in range(half_t)]
        uh = jnp.dot(jnp.concatenate(xs, axis=1), perm, preferred_element_type=F32).astype(BF16)
        for gl in range(ng):
            us_ref[gl, :, half * LANES:(half + 1) * LANES] = uh[:, gl * LANES:(gl + 1) * LANES]
    for gl in range(ng):
        ee_ref[gl] = jnp.dot(us_ref[gl], e_ref[gl], preferred_element_type=F32)
        h_ref[gl, :SUBLANES, :] = jnp.zeros((SUBLANES, sw), F32)

    def step(i, carry):
        row = pl.multiple_of(i * SUBLANES, SUBLANES)
        out = []
        for gl in range(ng):
            h = ee_ref[gl, pl.ds(row, SUBLANES), :sw]
            hs = ee_ref[gl, pl.ds(row, SUBLANES), sw:]
            for di, d in enumerate(S5_SCAN_STRIDES):
                t1 = tab_ref[gl, 2 * di * SUBLANES:(2 * di + 1) * SUBLANES, :]
                t2 = tab_ref[gl, (2 * di + 1) * SUBLANES:(2 * di + 2) * SUBLANES, :]
                sh = pltpu.roll(h, d, 0)
                shs = pltpu.roll(hs, d, 0)
                h, hs = h + t1 * sh + t2 * shs, hs + t1 * shs - t2 * sh
            nd = 2 * len(S5_SCAN_STRIDES)
            q1 = tab_ref[gl, nd * SUBLANES:(nd + 1) * SUBLANES, :]
            q2 = tab_ref[gl, (nd + 1) * SUBLANES:(nd + 2) * SUBLANES, :]
            cb, cbs = carry[2 * gl], carry[2 * gl + 1]
            hf = h + q1 * cb + q2 * cbs
            hfs = hs + q1 * cbs - q2 * cb
            h_ref[gl, pl.ds(SUBLANES + row, SUBLANES), :] = hf
            out.append(jnp.broadcast_to(hf[SUBLANES - 1:SUBLANES, :], hf.shape))
            out.append(jnp.broadcast_to(hfs[SUBLANES - 1:SUBLANES, :], hfs.shape))
        return tuple(out)

    zero = jnp.zeros((SUBLANES, sw), F32)
    carry = lax.fori_loop(0, nsub // SUBLANES, step, (zero,) * (2 * ng))
    for gl in range(ng):
        hl_ref[gl:gl + 1, :] = carry[2 * gl][0:1, :]
        hin = h_ref[gl, pl.ds(SUBLANES - 1, nsub), :].astype(BF16)
        y = (jnp.dot(us_ref[gl], m_ref[gl], preferred_element_type=F32)
             + jnp.dot(hin, f_ref[gl], preferred_element_type=F32))
        for half in range(2):
            ycat_ref[half, :, gl * LANES:(gl + 1) * LANES] = y[:, half * LANES:(half + 1) * LANES]
    for half in range(2):
        yc = ycat_ref[half]
        hi = yc.astype(BF16)
        lo = (yc - hi.astype(F32)).astype(BF16)
        z = (jnp.dot(hi, perm, preferred_element_type=F32)
             + jnp.dot(lo, perm, preferred_element_type=F32))
        for s in range(half_t):
            y_ref[pl.ds(half * half_t + s, nsub, stride=t), :] = z[:, s * LANES:(s + 1) * LANES]


def _s5_prompt(u, tables, bsz, seq):
    m_mat, e_mat, f_mat, tab = tables
    g = m_mat.shape[0]
    p = f_mat.shape[1] // 2
    t, ch = S5_CHUNK, S5_GROUP
    ng = LANES // ch
    assert ng == SUBLANES and t * ch == 2 * LANES
    n, w = u.shape
    nsub = seq // t
    perm = _s5_lane_permutation()
    npm = perm.shape[0]
    ntab = tab.shape[1]
    slab = lambda s, b: (s, 0, 0)
    y, hl = pl.pallas_call(
        _s5_prompt_kernel,
        out_shape=(jax.ShapeDtypeStruct((n, w), F32), jax.ShapeDtypeStruct((bsz, g, 2 * p), F32)),
        grid=(w // LANES, bsz),
        in_specs=[pl.BlockSpec((seq, LANES), lambda s, b: (b, s)),
                  pl.BlockSpec((npm, npm), lambda s, b: (0, 0)),
                  pl.BlockSpec((ng, t * ch, 4 * p), slab),
                  pl.BlockSpec((ng, t * ch, t * ch), slab),
                  pl.BlockSpec((ng, 2 * p, t * ch), slab),
                  pl.BlockSpec((ng, ntab, 2 * p), slab)],
        out_specs=(pl.BlockSpec((seq, LANES), lambda s, b: (b, s)),
                   pl.BlockSpec((None, ng, 2 * p), lambda s, b: (b, s, 0))),
        scratch_shapes=[pltpu.VMEM((ng, nsub, t * ch), BF16),
                        pltpu.VMEM((ng, nsub, 4 * p), F32),
                        pltpu.VMEM((ng, SUBLANES + nsub, 2 * p), F32),
                        pltpu.VMEM((2, nsub, ng * LANES), F32)],
        compiler_params=_cparams(2),
        name="s5_prompt",
    )(u, perm, e_mat, m_mat, f_mat, tab)
    return y, hl[..., :p], hl[..., p:]


def _s5_step_kernel(u_ref, h_ref, hs_ref, bt_ref, ct_ref, a1_ref, a2_ref, d_ref, y_ref, hn_ref):
    hp = lax.Precision.HIGHEST
    u = u_ref[...]
    bu = jnp.einsum('gbc,gcp->gbp', u, bt_ref[...], precision=hp, preferred_element_type=F32)
    hn = a1_ref[...] * h_ref[...] + a2_ref[...] * hs_ref[...] + bu
    hn_ref[...] = hn
    y = jnp.einsum('gbp,gpc->gbc', hn, ct_ref[...], precision=hp, preferred_element_type=F32)
    y_ref[...] = y + d_ref[...] * u


def _s5_step(u, h0_re, h0_im, params):
    a_re, a_im, log_dt, b_re, b_im, c_re, c_im, d_skip = params
    g, p = a_re.shape
    ch = S5_GROUP
    bsz = u.shape[0]
    llb, b_bar = _s5_discretize(a_re, a_im, log_dt, b_re, b_im)
    lam_bar = _s5_power(llb, 1.0)
    bt = jnp.concatenate([b_bar[0].transpose(0, 2, 1), b_bar[1].transpose(0, 2, 1)], axis=-1)
    ct = jnp.concatenate([c_re.astype(F32), -c_im.astype(F32)], axis=-1).transpose(0, 2, 1)
    a1 = jnp.concatenate([lam_bar[0]] * 2, axis=-1)[:, None, :]
    a2 = jnp.concatenate([-lam_bar[1], lam_bar[1]], axis=-1)[:, None, :]
    h = jnp.concatenate([h0_re, h0_im], axis=-1).transpose(1, 0, 2).astype(F32)
    hs = jnp.concatenate([h0_im, h0_re], axis=-1).transpose(1, 0, 2).astype(F32)
    ug = u.reshape(bsz, g, ch).transpose(1, 0, 2)
    dg = d_skip.astype(F32).reshape(g, 1, ch)
    y, hn = pl.pallas_call(
        _s5_step_kernel,
        out_shape=(jax.ShapeDtypeStruct((g, bsz, ch), F32), jax.ShapeDtypeStruct((g, bsz, 2 * p), F32)),
        compiler_params=pltpu.CompilerParams(vmem_limit_bytes=VMEM_LIMIT),
        name="s5_step",
    )(ug, h, hs, bt, ct, a1, a2, dg)
    hn = hn.transpose(1, 0, 2)
    return y.transpose(1, 0, 2).reshape(bsz, g * ch), hn[..., :p], hn[..., p:]


def _resident(a):
    zeros = (0,) * a.ndim
    return pl.BlockSpec(a.shape, lambda *_: zeros, pipeline_mode=pl.Buffered(1))


def _ffn_apply(x, ffn_refs, final_norm):
    g_ref, wg_ref, wu_ref, wd_ref, gf_ref = ffn_refs
    hn = _rms(x, g_ref[...], NORM_EPS).astype(BF16)
    hidden = wg_ref.shape[1]
    th = MXU_DIM if hidden % MXU_DIM == 0 else hidden
    acc = None
    for c in range(hidden // th):
        cols = slice(c * th, (c + 1) * th)
        gate = jnp.dot(hn, wg_ref[:, cols], preferred_element_type=F32)
        up = jnp.dot(hn, wu_ref[:, cols], preferred_element_type=F32)
        act = (jax.nn.silu(gate) * up).astype(BF16)
        part = jnp.dot(act, wd_ref[cols, :], preferred_element_type=F32)
        acc = part if acc is None else acc + part
    out = x + acc
    if final_norm:
        out = _rms(out, gf_ref[...], NORM_EPS)
    return out


def _ffn_specs(ffn_w):
    g, wg, wu, wd, gf, layer = ffn_w

    def one_layer(a):
        return pl.BlockSpec((None,) + a.shape[1:], lambda *_: (layer, 0, 0),
                            pipeline_mode=pl.Buffered(1))

    return [_resident(g), one_layer(wg), one_layer(wu), one_layer(wd), _resident(gf)]


def _ffn_kernel(x_ref, *rest, final_norm):
    o_ref = rest[-1]
    o_ref[...] = _ffn_apply(x_ref[...], rest[:-1], final_norm)


def _ffn(x, ffn_w, tm, final_norm):
    n, d = x.shape
    return pl.pallas_call(
        functools.partial(_ffn_kernel, final_norm=final_norm),
        out_shape=jax.ShapeDtypeStruct((n, d), F32),
        grid=(n // tm,),
        in_specs=[pl.BlockSpec((tm, d), lambda i: (i, 0))] + _ffn_specs(ffn_w),
        out_specs=pl.BlockSpec((tm, d), lambda i: (i, 0)),
        compiler_params=_cparams(1),
        name="ffn",
    )(x, *ffn_w[:5])


def _post_ab_kernel(x_ref, y_ref, a_ref, wglu_ref, bglu_ref, wo_ref, *rest, s5w, final_norm):
    o_ref = rest[-1]
    z = jax.nn.gelu(y_ref[...])
    gl = jnp.dot(z.astype(BF16), wglu_ref[...], preferred_element_type=F32) + bglu_ref[...]
    s5o = (z * jax.nn.sigmoid(gl)).astype(BF16)
    mix = (jnp.dot(s5o, wo_ref[:s5w, :], preferred_element_type=F32)
           + jnp.dot(a_ref[...].astype(BF16), wo_ref[s5w:, :], preferred_element_type=F32))
    o_ref[...] = _ffn_apply(x_ref[...] + mix, rest[:-1], final_norm)


def _post_ab(x, y, attn, wglu_bf, bglu, wo_bf, ffn_w, tm, final_norm):
    n, d = x.shape
    s5w = y.shape[1]
    aw = attn.shape[1]
    row = lambda i: (i, 0)
    return pl.pallas_call(
        functools.partial(_post_ab_kernel, s5w=s5w, final_norm=final_norm),
        out_shape=jax.ShapeDtypeStruct((n, d), F32),
        grid=(n // tm,),
        in_specs=[pl.BlockSpec((tm, d), row), pl.BlockSpec((tm, s5w), row), pl.BlockSpec((tm, aw), row),
                  _resident(wglu_bf), _resident(bglu), _resident(wo_bf)] + _ffn_specs(ffn_w),
        out_specs=pl.BlockSpec((tm, d), row),
        compiler_params=_cparams(1),
        name="post_ab_ffn",
    )(x, y, attn, wglu_bf, bglu, wo_bf, *ffn_w[:5])


def _lru_gates(xc, wg_ref, ba, bx, lam_param):
    xcb = xc.astype(BF16)
    halves = [jnp.dot(xcb[:, h * MXU_DIM:(h + 1) * MXU_DIM], wg_ref[h], preferred_element_type=F32)
              for h in range(wg_ref.shape[0])]
    r_pre = jnp.concatenate([hv[:, :MXU_DIM] for hv in halves], axis=1)
    i_pre = jnp.concatenate([hv[:, MXU_DIM:] for hv in halves], axis=1)
    r = jax.nn.sigmoid(r_pre + ba)
    i = jax.nn.sigmoid(i_pre + bx)
    log_a = -LRU_C * r * jax.nn.softplus(-lam_param)
    a = jnp.exp(log_a)
    b = jnp.sqrt(-jnp.tanh(log_a) * (a * a + 1.0)) * (i * xc)
    return a, b


def _pool_project(pooled, pw_ref, scale):
    pb = pooled.astype(BF16)
    halves = [jnp.dot(pb[:, h * MXU_DIM:(h + 1) * MXU_DIM], pw_ref[h], preferred_element_type=F32)
              for h in range(pw_ref.shape[0])]
    return jnp.concatenate(halves, axis=1) * scale


def _cd_prompt_kernel(x_ref, g_ref, win_ref, cw_ref, cb_ref, wg_ref, ba_ref, bx_ref, lam_ref,
                      pw_ref, psc_ref, wo_ref, fg_ref, fwg_ref, fwu_ref, fwd_ref, fgf_ref,
                      o_ref, conv_ref, lru_ref, pool_ref,
                      xl_ref, xp_ref, s2_ref, s4_ref, a_ref, b_ref, hs_ref, carry_ref,
                      *, tm, w, final_norm):
    ti = pl.program_id(1)
    hist = POOL_BUF + 1
    hp = hist + SUBLANES

    @pl.when(ti == 0)
    def _():
        xl_ref[:hist, :] = jnp.zeros((hist, w), F32)
        xp_ref[:hp, :] = jnp.zeros((hp, w), F32)
        s2_ref[:SUBLANES, :] = jnp.zeros((SUBLANES, w), F32)
        s4_ref[:SUBLANES, :] = jnp.zeros((SUBLANES, w), F32)
        carry_ref[...] = jnp.zeros(carry_ref.shape, F32)

    x = x_ref[...]
    hn = _rms(x, g_ref[...], NORM_EPS).astype(BF16)
    proj = jnp.dot(hn, win_ref[...], preferred_element_type=F32)
    gate = proj[:, :w]
    xl_ref[hist:, :] = proj[:, w:2 * w]
    xp_ref[hp:, :] = proj[:, 2 * w:]

    xc = cb_ref[...] + xl_ref[pl.ds(hist - (CONV_WIDTH - 1), tm), :] * cw_ref[0:1, :]
    for j in range(1, CONV_WIDTH):
        xc = xc + xl_ref[pl.ds(hist - (CONV_WIDTH - 1) + j, tm), :] * cw_ref[j:j + 1, :]

    a, b = _lru_gates(xc, wg_ref, ba_ref[...], bx_ref[...], lam_ref[...])
    a_ref[...] = a
    b_ref[...] = b

    def group(gi, carry):
        row = pl.multiple_of(gi * SUBLANES, SUBLANES)
        ag = a_ref[pl.ds(row, SUBLANES), :]
        bg = b_ref[pl.ds(row, SUBLANES), :]
        ridx = lax.broadcasted_iota(jnp.int32, ag.shape, 0)
        for d in (1, 2, 4):
            a_sh = jnp.where(ridx >= d, pltpu.roll(ag, d, 0), 1.0)
            b_sh = jnp.where(ridx >= d, pltpu.roll(bg, d, 0), 0.0)
            bg = ag * b_sh + bg
            ag = ag * a_sh
        h = bg + ag * carry
        hs_ref[pl.ds(row, SUBLANES), :] = h
        return jnp.broadcast_to(h[SUBLANES - 1:SUBLANES, :], h.shape)

    carry = lax.fori_loop(0, tm // SUBLANES, group, jnp.broadcast_to(carry_ref[...], (SUBLANES, w)))
    carry_ref[...] = carry[0:1, :]
    lru_out = jax.nn.gelu(gate) * hs_ref[...]

    assert POOL_WINDOWS == (2, 4, 8, 16)
    pg = w // len(POOL_WINDOWS)
    span = tm + hist
    s2_ref[pl.ds(SUBLANES, span), :] = (xp_ref[pl.ds(SUBLANES, span), :]
                                        + xp_ref[pl.ds(SUBLANES - 1, span), :])
    s4_ref[pl.ds(SUBLANES, span), pg:] = (s2_ref[pl.ds(SUBLANES, span), pg:]
                                          + s2_ref[pl.ds(SUBLANES - 2, span), pg:])
    s2_ref[pl.ds(SUBLANES, span), 2 * pg:] = (s4_ref[pl.ds(SUBLANES, span), 2 * pg:]
                                              + s4_ref[pl.ds(SUBLANES - 4, span), 2 * pg:])
    sums = (s2_ref[pl.ds(hp, tm), :pg],
            s4_ref[pl.ds(hp, tm), pg:2 * pg],
            s2_ref[pl.ds(hp, tm), 2 * pg:3 * pg],
            s2_ref[pl.ds(hp, tm), 3 * pg:] + s2_ref[pl.ds(hp - SUBLANES, tm), 3 * pg:])
    pos = ti * tm + lax.broadcasted_iota(jnp.int32, (tm, pg), 0)
    pooled = []
    for gidx, win in enumerate(POOL_WINDOWS):
        cur = xp_ref[pl.ds(hp, tm), gidx * pg:(gidx + 1) * pg]
        cnt = jnp.minimum(pos + 1, win).astype(F32)
        pooled.append(sums[gidx] / cnt - cur)
    pool_out = _pool_project(jnp.concatenate(pooled, axis=1), pw_ref, psc_ref[...])

    mix = (jnp.dot(lru_out.astype(BF16), wo_ref[:w, :], preferred_element_type=F32)
           + jnp.dot(pool_out.astype(BF16), wo_ref[w:, :], preferred_element_type=F32))
    o_ref[...] = _ffn_apply(x + mix, (fg_ref, fwg_ref, fwu_ref, fwd_ref, fgf_ref), final_norm)

    conv_ref[...] = xl_ref[pl.ds(hist + tm - (CONV_WIDTH - 1), CONV_WIDTH - 1), :]
    lru_ref[...] = carry_ref[...]
    pool_ref[...] = xp_ref[pl.ds(hp + tm - POOL_BUF, POOL_BUF), :]
    xl_ref[:hist, :] = xl_ref[pl.ds(tm, hist), :]
    xp_ref[pl.ds(SUBLANES, hist), :] = xp_ref[pl.ds(SUBLANES + tm, hist), :]


def _cd_prompt(x, g, cdw, ffn_w, bsz, seq, tm, final_norm):
    n, d = x.shape
    w = cdw[1].shape[1]
    nt = seq // tm
    hist = POOL_BUF + 1
    row = lambda b, i: (b * nt + i, 0)
    st = lambda b, i: (b, 0, 0)
    return pl.pallas_call(
        functools.partial(_cd_prompt_kernel, tm=tm, w=w, final_norm=final_norm),
        out_shape=(jax.ShapeDtypeStruct((n, d), F32),
                   jax.ShapeDtypeStruct((bsz, CONV_WIDTH - 1, w), F32),
                   jax.ShapeDtypeStruct((bsz, 1, w), F32),
                   jax.ShapeDtypeStruct((bsz, POOL_BUF, w), F32)),
        grid=(bsz, nt),
        in_specs=[pl.BlockSpec((tm, d), row), _resident(g)] + [_resident(a) for a in cdw]
                 + _ffn_specs(ffn_w),
        out_specs=(pl.BlockSpec((tm, d), row),
                   pl.BlockSpec((None, CONV_WIDTH - 1, w), st),
                   pl.BlockSpec((None, 1, w), st),
                   pl.BlockSpec((None, POOL_BUF, w), st)),
        scratch_shapes=[pltpu.VMEM((hist + tm, w), F32)]
                       + [pltpu.VMEM((SUBLANES + hist + tm, w), F32)] * 3
                       + [pltpu.VMEM((tm, w), F32)] * 3 + [pltpu.VMEM((1, w), F32)],
        compiler_params=_cparams(2),
        name="cd_prompt_ffn",
    )(x, g, *cdw, *ffn_w[:5])


def _cd_step_kernel(x_ref, g_ref, win_ref, cw_ref, cb_ref, wg_ref, ba_ref, bx_ref, lam_ref,
                    pw_ref, psc_ref, wo_ref, sc_ref, sl_ref, sp_ref,
                    o_ref, conv_ref, lru_ref, pool_ref, *, w):
    x = x_ref[...]
    hn = _rms(x, g_ref[...], NORM_EPS).astype(BF16)
    proj = jnp.dot(hn, win_ref[...], preferred_element_type=F32)
    gate = proj[:, :w]
    xl = proj[:, w:2 * w]
    xp = proj[:, 2 * w:]
    xc = cb_ref[...] + xl * cw_ref[CONV_WIDTH - 1:CONV_WIDTH, :]
    for j in range(CONV_WIDTH - 1):
        xc = xc + sc_ref[j] * cw_ref[j:j + 1, :]
    a, b = _lru_gates(xc, wg_ref, ba_ref[...], bx_ref[...], lam_ref[...])
    h = a * sl_ref[...] + b
    lru_out = jax.nn.gelu(gate) * h

    pg = w // len(POOL_WINDOWS)
    pooled = []
    for gidx, win in enumerate(POOL_WINDOWS):
        cols = slice(gidx * pg, (gidx + 1) * pg)
        cur = xp[:, cols]
        tot = cur
        for kk in range(1, win):
            tot = tot + sp_ref[POOL_BUF - kk][:, cols]
        pooled.append(tot / float(win) - cur)
    pool_out = _pool_project(jnp.concatenate(pooled, axis=1), pw_ref, psc_ref[...])

    mix = (jnp.dot(lru_out.astype(BF16), wo_ref[:w, :], preferred_element_type=F32)
           + jnp.dot(pool_out.astype(BF16), wo_ref[w:, :], preferred_element_type=F32))
    o_ref[...] = x + mix
    for j in range(CONV_WIDTH - 2):
        conv_ref[j] = sc_ref[j + 1]
    conv_ref[CONV_WIDTH - 2] = xl
    lru_ref[...] = h
    for j in range(POOL_BUF - 1):
        pool_ref[j] = sp_ref[j + 1]
    pool_ref[POOL_BUF - 1] = xp


def _cd_step(x, g, cdw, conv_buf, lru_h0, pool_buf):
    n, d = x.shape
    win_bf, cw, cb, wg_bf, ba, bx, lam, pw_bf, psc, wo_bf = cdw
    w = cw.shape[1]
    sc = conv_buf.transpose(1, 0, 2)
    sp = pool_buf.transpose(1, 0, 2)
    o, conv_n, lru_n, pool_n = pl.pallas_call(
        functools.partial(_cd_step_kernel, w=w),
        out_shape=(jax.ShapeDtypeStruct((n, d), F32),
                   jax.ShapeDtypeStruct((CONV_WIDTH - 1, n, w), F32),
                   jax.ShapeDtypeStruct((n, w), F32),
                   jax.ShapeDtypeStruct((POOL_BUF, n, w), F32)),
        compiler_params=pltpu.CompilerParams(vmem_limit_bytes=VMEM_LIMIT),
        name="cd_step",
    )(x, g, win_bf, cw, cb, wg_bf, ba, bx, lam, pw_bf, psc, wo_bf, sc, lru_h0, sp)
    return o, conv_n.transpose(1, 0, 2), lru_n, pool_n.transpose(1, 0, 2)


def _block_diag_tiles(w, tile):
    nb, c, _ = w.shape
    per = tile // c
    w4 = w.reshape(nb // per, per, c, c)
    placed = jnp.einsum('tpij,pq->tpiqj', w4, jnp.eye(per, dtype=w.dtype))
    return placed.reshape(nb // per, tile, tile)


def _rope_tables(pos, rows):
    half = HEAD_DIM // 2
    inv = ROPE_THETA ** (-jnp.arange(half, dtype=F32) / half)
    ang = pos.astype(F32)[:, None] * inv[None, :]
    cos = jnp.cos(ang)
    sin = jnp.sin(ang)
    reps = LANES // HEAD_DIM
    cos_t = jnp.tile(jnp.concatenate([cos, cos], axis=1), (1, reps))
    sin_t = jnp.tile(jnp.concatenate([-sin, sin], axis=1), (1, reps))
    if cos_t.shape[0] != rows:
        cos_t = jnp.broadcast_to(cos_t, (rows, LANES))
        sin_t = jnp.broadcast_to(sin_t, (rows, LANES))
    return cos_t, sin_t


def _row_tile(n, target):
    t = min(n, target)
    while n % t:
        t //= 2
    return t


def kernel(x_prompt, x_sample, cache_k, cache_v, page_table, state_s5_re, state_s5_im, state_conv, state_lru, state_pool, norm_mix, norm_ffn, norm_final, w_in_ab, w_out_ab, s5_a_re, s5_a_im, s5_log_dt, s5_b_re, s5_b_im, s5_c_re, s5_c_im, s5_d, s5_w_glu, s5_b_glu, diff_lq1, diff_lk1, diff_lq2, diff_lk2, diff_subln, w_in_cd, w_out_cd, conv_w, conv_b, lru_wa, lru_ba, lru_wx, lru_bx, lru_lambda, pool_w, pool_scale, ffn_w_gate, ffn_w_up, ffn_w_down):
    bsz, seq, d = x_prompt.shape
    nseq, dec_seq, _ = x_sample.shape
    assert dec_seq == 1
    depth = norm_mix.shape[0]
    npages = page_table.shape[1]
    page_size = cache_k.shape[2]
    past_len = npages * page_size
    assert past_len >= POOL_BUF
    s5w = s5_d.shape[1]
    qkw = cache_k.shape[3] * cache_k.shape[4]
    vw_total = cache_v.shape[3] * cache_v.shape[4]
    heads = cache_v.shape[3]
    n_p = bsz * seq
    row2 = lambda a: a.reshape(1, -1)

    xp = x_prompt.reshape(n_p, d)
    xs = x_sample.reshape(nseq, d)
    tm_p = _row_tile(seq, 512)
    cos_p, sin_p = _rope_tables(jnp.arange(seq, dtype=jnp.int32), seq)
    cos_s, sin_s = _rope_tables(jnp.full((1,), past_len, jnp.int32), nseq)

    outs_p = {'k': [], 'v': [], 're': [], 'im': [], 'conv': [], 'lru': [], 'pool': []}
    outs_s = {'k': [], 'v': [], 're': [], 'im': [], 'conv': [], 'lru': [], 'pool': []}

    ffn_gate_bf = ffn_w_gate.astype(BF16)
    ffn_up_bf = ffn_w_up.astype(BF16)
    ffn_down_bf = ffn_w_down.astype(BF16)

    for l in range(depth):
        j = l // 2
        g_mix = row2(norm_mix[l])
        last = l == depth - 1
        ffn_w = (row2(norm_ffn[l]), ffn_gate_bf, ffn_up_bf, ffn_down_bf, row2(norm_final), l)
        if l % 2 == 0:
            lam_init = 0.8 - 0.6 * math.exp(-0.3 * l)
            w_in = w_in_ab[j].astype(BF16)
            w_out = w_out_ab[j].astype(BF16)
            w_glu = s5_w_glu[j].astype(BF16)
            b_glu = row2(s5_b_glu[j])
            lamp = jnp.stack([diff_lq1[j], diff_lk1[j], diff_lq2[j], diff_lk2[j]]).astype(F32)
            gain = row2(diff_subln[j])
            s5p = (s5_a_re[j], s5_a_im[j], s5_log_dt[j], s5_b_re[j], s5_b_im[j],
                   s5_c_re[j], s5_c_im[j], s5_d[j])
            u, q_bf, k, v, k_bf, v_bf = _inproj_ab(xp, g_mix, w_in, cos_p, sin_p, tm_p, s5w, qkw)
            t_attn = _row_tile(seq, 1024)
            attn = _attn_prompt(q_bf, k_bf, v_bf, lamp, gain, bsz, seq, t_attn, t_attn, lam_init)
            y, h_re, h_im = _s5_prompt(u, _s5_prompt_tables(s5p, seq), bsz, seq)
            xp = _post_ab(xp, y, attn, w_glu, b_glu, w_out, ffn_w, tm_p, last)
            outs_p['k'].append(k.reshape(bsz, seq, qkw // HEAD_DIM, HEAD_DIM))
            outs_p['v'].append(v.reshape(bsz, seq, heads, vw_total // heads))
            outs_p['re'].append(h_re)
            outs_p['im'].append(h_im)
            u, q_bf, k, v, _, _ = _inproj_ab(xs, g_mix, w_in, cos_s, sin_s, nseq, s5w, qkw)
            ck = jnp.transpose(cache_k[j], (0, 2, 3, 1)).reshape(-1, qkw, page_size)
            cv = cache_v[j].reshape(-1, page_size * heads, vw_total // heads)
            attn = _attn_decode(q_bf.astype(F32), k, v.reshape(nseq, vw_total), ck, cv, page_table,
                                lamp, gain,
                                math.gcd(npages, 32), lam_init)
            y, h_re, h_im = _s5_step(u, state_s5_re[j], state_s5_im[j], s5p)
            xs = _post_ab(xs, y, attn, w_glu, b_glu, w_out, ffn_w, nseq, last)
            outs_s['k'].append(k.reshape(nseq, 1, qkw // HEAD_DIM, HEAD_DIM))
            outs_s['v'].append(v.reshape(nseq, 1, heads, vw_total // heads))
            outs_s['re'].append(h_re)
            outs_s['im'].append(h_im)
        else:
            w = conv_w.shape[2]
            wg = jnp.concatenate([_block_diag_tiles(lru_wa[j], MXU_DIM),
                                  _block_diag_tiles(lru_wx[j], MXU_DIM)], axis=2).astype(BF16)
            cdw = (w_in_cd[j].astype(BF16), conv_w[j], row2(conv_b[j]), wg, row2(lru_ba[j]),
                   row2(lru_bx[j]), row2(lru_lambda[j]),
                   _block_diag_tiles(pool_w[j], MXU_DIM).astype(BF16), row2(pool_scale[j]),
                   w_out_cd[j].astype(BF16))
            xp, c_new, h_new, p_new = _cd_prompt(xp, g_mix, cdw, ffn_w, bsz, seq, tm_p, last)
            outs_p['conv'].append(c_new)
            outs_p['lru'].append(h_new.reshape(bsz, w))
            outs_p['pool'].append(p_new)
            xs, c_new, h_new, p_new = _cd_step(xs, g_mix, cdw, state_conv[j], state_lru[j], state_pool[j])
            outs_s['conv'].append(c_new)
            outs_s['lru'].append(h_new)
            outs_s['pool'].append(p_new)
            xs = _ffn(xs, ffn_w, nseq, last)

    sp = {name: jnp.stack(vals) for name, vals in outs_p.items()}
    ss = {name: jnp.stack(vals) for name, vals in outs_s.items()}
    y_prompt = xp.reshape(bsz, seq, d)
    y_sample = xs.reshape(nseq, 1, d)
    return (y_prompt, y_sample, sp['k'], sp['v'], ss['k'], ss['v'], sp['re'], sp['im'], ss['re'], ss['im'],
            sp['conv'], ss['conv'], sp['lru'], ss['lru'], sp['pool'], ss['pool'])
```

```python
import functools
import math

import jax
import jax.numpy as jnp
from jax import lax
from jax.experimental import pallas as pl
from jax.experimental.pallas import tpu as pltpu

F32 = jnp.float32
BF16 = jnp.bfloat16

HEAD_DIM = 64
ROPE_THETA = 10000.0
NORM_EPS = 1e-6
SUBLN_EPS = 1e-5
S5_GROUP = 16
S5_CHUNK = 16
S5_SCAN_STRIDES = (1, 2, 4)
LRU_C = 8.0
CONV_WIDTH = 4
POOL_WINDOWS = (2, 4, 8, 16)
POOL_BUF = max(POOL_WINDOWS) - 1

LANES = 128
SUBLANES = 8
MXU_DIM = 256
VMEM_LIMIT = 56 * 1024 * 1024

NEG_BIG = -1e30
LOG2E = math.log2(math.e)


def _cparams(n_axes, flags=None):
    return pltpu.CompilerParams(dimension_semantics=("arbitrary",) * n_axes,
                                vmem_limit_bytes=VMEM_LIMIT, flags=flags)


def _rms(x, g, eps):
    ms = jnp.mean(x * x, axis=-1, keepdims=True)
    return x * lax.rsqrt(ms + eps) * g


def _lane_tile(x, reps):
    return jnp.concatenate([x] * reps, axis=1)


def _rope(x, cos, sin_signed):
    width = x.shape[1]
    lane = lax.broadcasted_iota(jnp.int32, x.shape, 1)
    half = HEAD_DIM // 2
    rot = jnp.where((lane & half) == 0, pltpu.roll(x, width - half, 1), pltpu.roll(x, half, 1))
    return x * cos + rot * sin_signed


def _inproj_ab_kernel(x_ref, g_ref, w_ref, cos_ref, sin_ref,
                      u_ref, q_ref, k_ref, v_ref, kb_ref, vb_ref, *, s5w, qkw):
    h = _rms(x_ref[...], g_ref[...], NORM_EPS).astype(BF16)
    proj = jnp.dot(h, w_ref[...], preferred_element_type=F32)
    reps = qkw // LANES
    cos = _lane_tile(cos_ref[...], reps)
    sin = _lane_tile(sin_ref[...], reps)
    u_ref[...] = proj[:, :s5w]
    q = _rope(proj[:, s5w:s5w + qkw], cos, sin)
    k = _rope(proj[:, s5w + qkw:s5w + 2 * qkw], cos, sin)
    v = proj[:, s5w + 2 * qkw:]
    q_ref[...] = (q * (HEAD_DIM ** -0.5 * LOG2E)).astype(BF16)
    k_ref[...] = k
    nvh = v.shape[1] // LANES
    for hh in range(nvh):
        v_ref[pl.ds(hh, v.shape[0], stride=nvh), :] = v[:, hh * LANES:(hh + 1) * LANES]
    kb_ref[...] = k.astype(BF16)
    vb_ref[...] = v.astype(BF16)


def _inproj_ab(x, g, w_bf, cos_tab, sin_tab, tm, s5w, qkw):
    n, d = x.shape
    wn = w_bf.shape[1]
    vw = wn - s5w - 2 * qkw
    nvh = vw // LANES
    period = cos_tab.shape[0] // tm
    row = lambda i: (i, 0)
    const = lambda i: (0, 0)
    tab = lambda i: (i % period, 0)
    return pl.pallas_call(
        functools.partial(_inproj_ab_kernel, s5w=s5w, qkw=qkw),
        out_shape=(jax.ShapeDtypeStruct((n, s5w), F32), jax.ShapeDtypeStruct((n, qkw), BF16),
                   jax.ShapeDtypeStruct((n, qkw), F32), jax.ShapeDtypeStruct((n * nvh, LANES), F32),
                   jax.ShapeDtypeStruct((n, qkw), BF16), jax.ShapeDtypeStruct((n, vw), BF16)),
        grid=(n // tm,),
        in_specs=[pl.BlockSpec((tm, d), row), pl.BlockSpec((1, d), const),
                  pl.BlockSpec((d, wn), const), pl.BlockSpec((tm, LANES), tab),
                  pl.BlockSpec((tm, LANES), tab)],
        out_specs=(pl.BlockSpec((tm, s5w), row), pl.BlockSpec((tm, qkw), row),
                   pl.BlockSpec((tm, qkw), row), pl.BlockSpec((tm * nvh, LANES), row),
                   pl.BlockSpec((tm, qkw), row), pl.BlockSpec((tm, vw), row)),
        compiler_params=_cparams(1),
        name="inproj_ab",
    )(x, g, w_bf, cos_tab, sin_tab)


def _diff_lambda(lamp, lam_init):
    s1 = jnp.sum(lamp[0:1] * lamp[1:2], axis=1, keepdims=True)
    s2 = jnp.sum(lamp[2:3] * lamp[3:4], axis=1, keepdims=True)
    return jnp.exp(s1) - jnp.exp(s2) + lam_init


def _subln(o, gain, lam_init):
    ms = jnp.mean(o * o, axis=-1, keepdims=True)
    return o * lax.rsqrt(ms + SUBLN_EPS) * gain * (1.0 - lam_init)


ONES_ROWS = 16


def _attn_prompt_kernel(lamp_ref, q_ref, k_ref, v_ref, gain_ref, kf_ref, o_ref, kt_ref,
                        vt_ref, m_ref, acc_ref, *, tq, tk, lam_init):
    qi = pl.program_id(2)
    seq, vw = v_ref.shape
    per_q = tq // tk

    @pl.when(qi == 0)
    def _():
        for c in range(seq // tk):
            blk = v_ref[c * tk:(c + 1) * tk, :].astype(F32)
            vt_ref[c, :vw, :] = blk.T.astype(BF16)
            vt_ref[c, vw:, :] = jnp.ones((ONES_ROWS, tk), BF16)
            kt_ref[:, c * tk:(c + 1) * tk] = kf_ref[c * tk:(c + 1) * tk, :].T

    q = q_ref[...]
    lane = lax.broadcasted_iota(jnp.int32, q.shape, 1)
    zero = jnp.zeros_like(q)
    qs = (jnp.where(lane < HEAD_DIM, q, zero), jnp.where(lane >= HEAD_DIM, q, zero))
    m_ref[...] = jnp.full(m_ref.shape, NEG_BIG, F32)
    acc_ref[...] = jnp.zeros(acc_ref.shape, F32)

    def chunk(c, diag):
        off = pl.multiple_of(c * tk, tk)
        kc = k_ref[pl.ds(off, tk), :]
        vc = vt_ref[c]

        def scores(j):
            st = lax.dot_general(kc, qs[j], (((1,), (1,)), ((), ())), preferred_element_type=F32)
            if diag is not None:
                key = lax.broadcasted_iota(jnp.int32, st.shape, 0) + diag * tk
                qry = lax.broadcasted_iota(jnp.int32, st.shape, 1)
                st = jnp.where(key <= qry, st, -jnp.inf)
            return st

        def softmax(j, st):
            m_prev = m_ref[j]
            m_new = jnp.maximum(m_prev, jnp.max(st, axis=0, keepdims=True))
            m_ref[j] = m_new
            return jnp.exp2(m_prev - m_new), jnp.exp2(st - m_new).astype(BF16)

        def accumulate(j, alpha, p):
            pv = jnp.dot(vc, p, preferred_element_type=F32)
            acc_ref[j] = alpha * acc_ref[j] + pv

        alpha0, p0 = softmax(0, scores(0))
        st1 = scores(1)
        accumulate(0, alpha0, p0)
        alpha1, p1 = softmax(1, st1)
        accumulate(1, alpha1, p1)

    def body(c, carry):
        chunk(c, None)
        return carry

    def diagonal_chunk(c):
        h = tk // 2
        off = pl.multiple_of(c * tk, tk)
        k_lo = k_ref[pl.ds(off, h), :]
        k_hi = k_ref[pl.ds(off + h, h), :]
        vc = vt_ref[c]

        def masked_scores(kpart, qpart):
            st = lax.dot_general(kpart, qpart, (((1,), (1,)), ((), ())), preferred_element_type=F32)
            key = lax.broadcasted_iota(jnp.int32, st.shape, 0)
            qry = lax.broadcasted_iota(jnp.int32, st.shape, 1)
            return jnp.where(key <= qry, st, -jnp.inf)

        for j in range(2):
            st_lo = masked_scores(k_lo, qs[j])
            st_hi = masked_scores(k_hi, qs[j][h:, :])
            m_prev = m_ref[j]
            m_lo = jnp.maximum(m_prev, jnp.max(st_lo, axis=0, keepdims=True))
            m_new = jnp.concatenate(
                [m_lo[:, :h], jnp.maximum(m_lo[:, h:], jnp.max(st_hi, axis=0, keepdims=True))], axis=1)
            m_ref[j] = m_new
            alpha = jnp.exp2(m_prev - m_new)
            p_lo = jnp.exp2(st_lo - m_new).astype(BF16)
            p_hi = jnp.exp2(st_hi - m_new[:, h:]).astype(BF16)
            pv = jnp.dot(vc[:, :h], p_lo, preferred_element_type=F32)
            pv_hi = jnp.dot(vc[:, h:], p_hi, preferred_element_type=F32)
            pv = jnp.concatenate([pv[:, :h], pv[:, h:] + pv_hi], axis=1)
            acc_ref[j] = alpha * acc_ref[j] + pv

    lax.fori_loop(0, qi * per_q, body, 0)
    if per_q == 1:
        diagonal_chunk(qi)
    else:
        for d in range(per_q):
            chunk(qi * per_q + d, d)

    lam = _diff_lambda(lamp_ref[...], lam_init)
    a0 = acc_ref[0]
    a1 = acc_ref[1]
    ot = a0[:vw] / a0[vw:vw + 1] - lam * (a1[:vw] / a1[vw:vw + 1])
    o_ref[...] = _subln(ot.T, gain_ref[...], lam_init).astype(o_ref.dtype)


def _attn_prompt(q_bf, k_bf, v_bf, k_f32, lamp, gain, bsz, seq, tq, tk, lam_init):
    n, qkw = q_bf.shape
    vw_total = v_bf.shape[1]
    heads = qkw // (2 * HEAD_DIM)
    vw = vw_total // heads
    nq = seq // tq
    pair = lambda b, h, i: (b, h)
    return pl.pallas_call(
        functools.partial(_attn_prompt_kernel, tq=tq, tk=tk, lam_init=lam_init),
        out_shape=(jax.ShapeDtypeStruct((n, vw_total), BF16),
                   jax.ShapeDtypeStruct((bsz * qkw, seq), F32)),
        grid=(bsz, heads, nq),
        in_specs=[pl.BlockSpec((4, HEAD_DIM), lambda b, h, i: (0, 0)),
                  pl.BlockSpec((tq, 2 * HEAD_DIM), lambda b, h, i: (b * nq + i, h)),
                  pl.BlockSpec((seq, 2 * HEAD_DIM), pair),
                  pl.BlockSpec((seq, vw), pair),
                  pl.BlockSpec((1, vw), lambda b, h, i: (0, 0)),
                  pl.BlockSpec((seq, 2 * HEAD_DIM), pair)],
        out_specs=(pl.BlockSpec((tq, vw), lambda b, h, i: (b * nq + i, h)),
                   pl.BlockSpec((2 * HEAD_DIM, seq), lambda b, h, i: (b * heads + h, 0))),
        scratch_shapes=[pltpu.VMEM((seq // tk, vw + ONES_ROWS, tk), BF16),
                        pltpu.VMEM((2, 1, tq), F32),
                        pltpu.VMEM((2, vw + ONES_ROWS, tq), F32)],
        compiler_params=_cparams(3),
        name="attn_prompt",
    )(lamp, q_bf, k_bf, v_bf, gain, k_f32)


def _attn_decode_kernel(pt_ref, lamp_ref, q_ref, kself_ref, vself_ref, gain_ref, *rest,
                        pages, lam_init, vw):
    k_refs = rest[:pages]
    v_refs = rest[pages:2 * pages]
    o_ref = rest[2 * pages]
    m_ref, l_ref, acc_ref = rest[2 * pages + 1:]
    j = pl.program_id(1)
    nh2 = m_ref.shape[0]
    width = q_ref.shape[1]

    @pl.when(j == 0)
    def _():
        m_ref[...] = jnp.full(m_ref.shape, NEG_BIG, F32)
        l_ref[...] = jnp.zeros(l_ref.shape, F32)
        acc_ref[...] = jnp.zeros(acc_ref.shape, F32)

    rowh = lax.broadcasted_iota(jnp.int32, (nh2, width), 0)
    laneh = lax.broadcasted_iota(jnp.int32, (nh2, width), 1) // HEAD_DIM
    q8 = jnp.where(rowh == laneh, jnp.broadcast_to(q_ref[...], (nh2, width)), 0.0)
    q8b = q8.astype(BF16)

    s_parts = [jnp.dot(q8b, k_refs[i][...].astype(BF16), preferred_element_type=F32)
               for i in range(pages)]
    s = jnp.concatenate(s_parts, axis=1)
    m_prev = m_ref[...]
    m_new = jnp.maximum(m_prev, jnp.max(s, axis=1, keepdims=True))
    alpha = jnp.exp2(m_prev - m_new)
    p = jnp.exp2(s - _lane_tile(m_new, s.shape[1] // LANES))
    l_ref[...] = alpha * l_ref[...] + jnp.sum(p, axis=1, keepdims=True)
    ps = k_refs[0].shape[1]
    nvh = width // vw
    pb = p.astype(BF16)
    pv = None
    for i in range(pages):
        cols = [jnp.dot(pb[:, i * ps:(i + 1) * ps],
                        v_refs[i][pl.ds(h, ps, stride=nvh), :].astype(BF16),
                        preferred_element_type=F32) for h in range(nvh)]
        part = jnp.concatenate(cols, axis=1)
        pv = part if pv is None else pv + part
    acc_ref[...] = _lane_tile(alpha, width // LANES) * acc_ref[...] + pv
    m_ref[...] = m_new

    @pl.when(j == pl.num_programs(1) - 1)
    def _():
        s_self = jnp.sum(q8 * kself_ref[...], axis=1, keepdims=True)
        m_old = m_ref[...]
        m_fin = jnp.maximum(m_old, s_self)
        a_fin = jnp.exp2(m_old - m_fin)
        p_self = jnp.exp2(s_self - m_fin)
        l_fin = a_fin * l_ref[...] + p_self
        acc = (_lane_tile(a_fin, width // LANES) * acc_ref[...]
               + _lane_tile(p_self, width // LANES) * vself_ref[...])
        o8 = acc / _lane_tile(l_fin, width // LANES)
        lam = _diff_lambda(lamp_ref[...], lam_init)
        vhead = lax.broadcasted_iota(jnp.int32, (nh2, width), 1) // vw
        sign = jnp.where((rowh & 1) == 0, 1.0, -lam)
        coef = jnp.where(vhead == rowh // 2, sign, 0.0)
        o = jnp.sum(o8 * coef, axis=0, keepdims=True)
        gain = gain_ref[...]
        outs = [_subln(o[:, h * vw:(h + 1) * vw], gain, lam_init) for h in range(width // vw)]
        o_ref[...] = jnp.concatenate(outs, axis=1)


def _attn_decode(q, k_self, v_self, cache_kt, cache_v2, page_table, lamp, gain, pages, lam_init):
    nseq, width = q.shape
    npages = page_table.shape[1]
    ps = cache_kt.shape[2]
    vw = gain.shape[1]
    nh2 = width // HEAD_DIM
    pt = page_table.reshape(-1)
    q3, k3, v3 = (a.reshape(nseq, 1, width) for a in (q, k_self, v_self))
    seq_spec = pl.BlockSpec((None, 1, width), lambda b, j, pt: (b, 0, 0))

    def page_spec(i, shape):
        return pl.BlockSpec((None,) + shape,
                            lambda b, j, pt: (pt[b * npages + j * pages + i], 0, 0))

    grid_spec = pltpu.PrefetchScalarGridSpec(
        num_scalar_prefetch=1,
        grid=(nseq, npages // pages),
        in_specs=[pl.BlockSpec((4, HEAD_DIM), lambda b, j, pt: (0, 0)), seq_spec, seq_spec, seq_spec,
                  pl.BlockSpec((1, vw), lambda b, j, pt: (0, 0))]
                 + [page_spec(i, cache_kt.shape[1:]) for i in range(pages)]
                 + [page_spec(i, cache_v2.shape[1:]) for i in range(pages)],
        out_specs=seq_spec,
        scratch_shapes=[pltpu.VMEM((nh2, LANES), F32), pltpu.VMEM((nh2, LANES), F32),
                        pltpu.VMEM((nh2, width), F32)],
    )
    out = pl.pallas_call(
        functools.partial(_attn_decode_kernel, pages=pages, lam_init=lam_init, vw=vw),
        out_shape=jax.ShapeDtypeStruct((nseq, 1, width), F32),
        grid_spec=grid_spec,
        compiler_params=_cparams(2),
        name="attn_decode",
    )(pt, lamp, q3, k3, v3, gain, *([cache_kt] * pages), *([cache_v2] * pages))
    return out.reshape(nseq, width)


def _cmul(a, b):
    return a[0] * b[0] - a[1] * b[1], a[0] * b[1] + a[1] * b[0]


def _s5_discretize(a_re, a_im, log_dt, b_re, b_im):
    a_re, a_im = a_re.astype(F32), a_im.astype(F32)
    step = jnp.exp(log_dt.astype(F32))[:, None]
    llb = (a_re * step, a_im * step)
    lb = _s5_power(llb, 1.0)
    num = (lb[0] - 1.0, lb[1])
    den = a_re * a_re + a_im * a_im
    quo = ((num[0] * a_re + num[1] * a_im) / den, (num[1] * a_re - num[0] * a_im) / den)
    b_bar = _cmul((quo[0][:, :, None], quo[1][:, :, None]), (b_re.astype(F32), b_im.astype(F32)))
    return llb, b_bar


def _s5_power(llb, k):
    mag = jnp.exp(llb[0] * k)
    return mag * jnp.cos(llb[1] * k), mag * jnp.sin(llb[1] * k)


def _s5_prompt_tables(params, seq):
    a_re, a_im, log_dt, b_re, b_im, c_re, c_im, d_skip = params
    g, p = a_re.shape
    ch, t = S5_GROUP, S5_CHUNK
    llb, b_bar = _s5_discretize(a_re, a_im, log_dt, b_re, b_im)
    c = (c_re.astype(F32), c_im.astype(F32))
    lags = jnp.arange(t + 1, dtype=F32)[:, None, None]
    powk = _s5_power((llb[0][None], llb[1][None]), lags)
    cp = _cmul((c[0][None], c[1][None]),
               (powk[0][:, :, None, :], powk[1][:, :, None, :]))
    klag = (jnp.einsum('kgcp,gpd->gdkc', cp[0][:t], b_bar[0])
            - jnp.einsum('kgcp,gpd->gdkc', cp[1][:t], b_bar[1]))
    skip = d_skip.astype(F32).reshape(g, ch)[:, None, :] * jnp.eye(ch, dtype=F32)[None]
    klag = klag.at[:, :, 0, :].add(skip)
    idx = jnp.arange(t)
    place = ((idx[None, None, :] - idx[:, None, None] == idx[None, :, None])[:, :, None, :, None]
             & jnp.eye(ch, dtype=bool)[None, None, :, None, :])
    place = place.reshape(t, t * ch, t * ch).astype(F32)
    m_mat = jnp.einsum('xq,sqr->sxr', klag.reshape(g * ch, t * ch), place)
    m_mat = m_mat.reshape(t, g, ch, t * ch).transpose(1, 0, 2, 3).reshape(g, t * ch, t * ch)
    pw_rev = (powk[0][:t][::-1].transpose(1, 0, 2)[:, :, None, :],
              powk[1][:t][::-1].transpose(1, 0, 2)[:, :, None, :])
    bb_t = (b_bar[0].transpose(0, 2, 1)[:, None], b_bar[1].transpose(0, 2, 1)[:, None])
    e_c = _cmul(pw_rev, bb_t)
    e_re = e_c[0].reshape(g, t * ch, p)
    e_im = e_c[1].reshape(g, t * ch, p)
    e_mat = jnp.concatenate([e_re, e_im, e_im, e_re], axis=-1)
    f_re = cp[0][1:t + 1].transpose(1, 0, 2, 3).reshape(g, t * ch, p)
    f_im = cp[1][1:t + 1].transpose(1, 0, 2, 3).reshape(g, t * ch, p)
    f_mat = jnp.concatenate([f_re, -f_im], axis=-1).transpose(0, 2, 1)
    ks = (jnp.arange(1, SUBLANES + 1, dtype=F32) * t)[None, :, None]
    pw = _s5_power((llb[0][:, None, :], llb[1][:, None, :]), ks)
    a1 = jnp.concatenate([pw[0], pw[0]], axis=-1)
    a2 = jnp.concatenate([-pw[1], pw[1]], axis=-1)
    rows = jnp.arange(SUBLANES)[None, :, None]
    parts = []
    for d in S5_SCAN_STRIDES:
        parts.append(jnp.where(rows >= d, a1[:, d - 1:d, :], 0.0))
        parts.append(jnp.where(rows >= d, a2[:, d - 1:d, :], 0.0))
    tab = jnp.concatenate(parts + [a1, a2], axis=1)
    return m_mat.astype(BF16), e_mat.astype(BF16), f_mat.astype(BF16), tab


def _s5_lane_permutation():
    n = SUBLANES * SUBLANES * S5_GROUP
    idx = jnp.arange(n).reshape(SUBLANES, SUBLANES, S5_GROUP).transpose(1, 0, 2).reshape(n)
    return jax.nn.one_hot(idx, n, dtype=BF16)


def _s5_prompt_kernel(u_ref, perm_ref, e_ref, m_ref, f_ref, tab_ref, y_ref, hl_ref,
                      us_ref, ee_ref, h_ref, ycat_ref):
    t = S5_CHUNK
    ng = SUBLANES
    nsub = u_ref.shape[0] // t
    sw = tab_ref.shape[2]
    half_t = t // 2
    perm = perm_ref[...]
    for half in range(2):
        xs = [u_ref[pl.ds(half * half_t + s, nsub, stride=t), :].astype(BF16) for s in range(half_t)]
        uh = jnp.dot(jnp.concatenate(xs, axis=1), perm, preferred_element_type=F32).astype(BF16)
        for gl in range(ng):
            us_ref[gl, :, half * LANES:(half + 1) * LANES] = uh[:, gl * LANES:(gl + 1) * LANES]
    for gl in range(ng):
        ee_ref[gl] = jnp.dot(us_ref[gl], e_ref[gl], preferred_element_type=F32)
        h_ref[gl, :SUBLANES, :] = jnp.zeros((SUBLANES, sw), F32)

    def step(i, carry):
        row = pl.multiple_of(i * SUBLANES, SUBLANES)
        out = []
        for gl in range(ng):
            h = ee_ref[gl, pl.ds(row, SUBLANES), :sw]
            hs = ee_ref[gl, pl.ds(row, SUBLANES), sw:]
            for di, d in enumerate(S5_SCAN_STRIDES):
                t1 = tab_ref[gl, 2 * di * SUBLANES:(2 * di + 1) * SUBLANES, :]
                t2 = tab_ref[gl, (2 * di + 1) * SUBLANES:(2 * di + 2) * SUBLANES, :]
                sh = pltpu.roll(h, d, 0)
                shs = pltpu.roll(hs, d, 0)
                h, hs = h + t1 * sh + t2 * shs, hs + t1 * shs - t2 * sh
            nd = 2 * len(S5_SCAN_STRIDES)
            q1 = tab_ref[gl, nd * SUBLANES:(nd + 1) * SUBLANES, :]
            q2 = tab_ref[gl, (nd + 1) * SUBLANES:(nd + 2) * SUBLANES, :]
            cb, cbs = carry[2 * gl], carry[2 * gl + 1]
            hf = h + q1 * cb + q2 * cbs
            hfs = hs + q1 * cbs - q2 * cb
            h_ref[gl, pl.ds(SUBLANES + row, SUBLANES), :] = hf
            out.append(jnp.broadcast_to(hf[SUBLANES - 1:SUBLANES, :], hf.shape))
            out.append(jnp.broadcast_to(hfs[SUBLANES - 1:SUBLANES, :], hfs.shape))
        return tuple(out)

    zero = jnp.zeros((SUBLANES, sw), F32)
    carry = lax.fori_loop(0, nsub // SUBLANES, step, (zero,) * (2 * ng))
    for gl in range(ng):
        hl_ref[gl:gl + 1, :] = carry[2 * gl][0:1, :]
        hin = h_ref[gl, pl.ds(SUBLANES - 1, nsub), :].astype(BF16)
        y = (jnp.dot(us_ref[gl], m_ref[gl], preferred_element_type=F32)
             + jnp.dot(hin, f_ref[gl], preferred_element_type=F32))
        for half in range(2):
            ycat_ref[half, :, gl * LANES:(gl + 1) * LANES] = y[:, half * LANES:(half + 1) * LANES]
    for half in range(2):
        yc = ycat_ref[half]
        hi = yc.astype(BF16)
        lo = (yc - hi.astype(F32)).astype(BF16)
        z = (jnp.dot(hi, perm, preferred_element_type=F32)
             + jnp.dot(lo, perm, preferred_element_type=F32))
        for s in range(half_t):
            y_ref[pl.ds(half * half_t + s, nsub, stride=t), :] = z[:, s * LANES:(s + 1) * LANES]


def _s5_prompt(u, tables, bsz, seq):
    m_mat, e_mat, f_mat, tab = tables
    g = m_mat.shape[0]
    p = f_mat.shape[1] // 2
    t, ch = S5_CHUNK, S5_GROUP
    ng = LANES // ch
    assert ng == SUBLANES and t * ch == 2 * LANES
    n, w = u.shape
    nsub = seq // t
    perm = _s5_lane_permutation()
    npm = perm.shape[0]
    ntab = tab.shape[1]
    slab = lambda s, b: (s, 0, 0)
    y, hl = pl.pallas_call(
        _s5_prompt_kernel,
        out_shape=(jax.ShapeDtypeStruct((n, w), F32), jax.ShapeDtypeStruct((bsz, g, 2 * p), F32)),
        grid=(w // LANES, bsz),
        in_specs=[pl.BlockSpec((seq, LANES), lambda s, b: (b, s)),
                  pl.BlockSpec((npm, npm), lambda s, b: (0, 0)),
                  pl.BlockSpec((ng, t * ch, 4 * p), slab),
                  pl.BlockSpec((ng, t * ch, t * ch), slab),
                  pl.BlockSpec((ng, 2 * p, t * ch), slab),
                  pl.BlockSpec((ng, ntab, 2 * p), slab)],
        out_specs=(pl.BlockSpec((seq, LANES), lambda s, b: (b, s)),
                   pl.BlockSpec((None, ng, 2 * p), lambda s, b: (b, s, 0))),
        scratch_shapes=[pltpu.VMEM((ng, nsub, t * ch), BF16),
                        pltpu.VMEM((ng, nsub, 4 * p), F32),
                        pltpu.VMEM((ng, SUBLANES + nsub, 2 * p), F32),
                        pltpu.VMEM((2, nsub, ng * LANES), F32)],
        compiler_params=_cparams(2),
        name="s5_prompt",
    )(u, perm, e_mat, m_mat, f_mat, tab)
    return y, hl[..., :p], hl[..., p:]


def _s5_step_kernel(u_ref, h_ref, hs_ref, bt_ref, ct_ref, a1_ref, a2_ref, d_ref, y_ref, hn_ref):
    hp = lax.Precision.HIGHEST
    u = u_ref[...]
    bu = jnp.einsum('gbc,gcp->gbp', u, bt_ref[...], precision=hp, preferred_element_type=F32)
    hn = a1_ref[...] * h_ref[...] + a2_ref[...] * hs_ref[...] + bu
    hn_ref[...] = hn
    y = jnp.einsum('gbp,gpc->gbc', hn, ct_ref[...], precision=hp, preferred_element_type=F32)
    y_ref[...] = y + d_ref[...] * u


def _s5_step(u, h0_re, h0_im, params):
    a_re, a_im, log_dt, b_re, b_im, c_re, c_im, d_skip = params
    g, p = a_re.shape
    ch = S5_GROUP
    bsz = u.shape[0]
    llb, b_bar = _s5_discretize(a_re, a_im, log_dt, b_re, b_im)
    lam_bar = _s5_power(llb, 1.0)
    bt = jnp.concatenate([b_bar[0].transpose(0, 2, 1), b_bar[1].transpose(0, 2, 1)], axis=-1)
    ct = jnp.concatenate([c_re.astype(F32), -c_im.astype(F32)], axis=-1).transpose(0, 2, 1)
    a1 = jnp.concatenate([lam_bar[0]] * 2, axis=-1)[:, None, :]
    a2 = jnp.concatenate([-lam_bar[1], lam_bar[1]], axis=-1)[:, None, :]
    h = jnp.concatenate([h0_re, h0_im], axis=-1).transpose(1, 0, 2).astype(F32)
    hs = jnp.concatenate([h0_im, h0_re], axis=-1).transpose(1, 0, 2).astype(F32)
    ug = u.reshape(bsz, g, ch).transpose(1, 0, 2)
    dg = d_skip.astype(F32).reshape(g, 1, ch)
    y, hn = pl.pallas_call(
        _s5_step_kernel,
        out_shape=(jax.ShapeDtypeStruct((g, bsz, ch), F32), jax.ShapeDtypeStruct((g, bsz, 2 * p), F32)),
        compiler_params=pltpu.CompilerParams(vmem_limit_bytes=VMEM_LIMIT),
        name="s5_step",
    )(ug, h, hs, bt, ct, a1, a2, dg)
    hn = hn.transpose(1, 0, 2)
    return y.transpose(1, 0, 2).reshape(bsz, g * ch), hn[..., :p], hn[..., p:]


def _resident(a):
    zeros = (0,) * a.ndim
    return pl.BlockSpec(a.shape, lambda *_: zeros, pipeline_mode=pl.Buffered(1))


def _ffn_apply(x, ffn_refs, final_norm):
    g_ref, wg_ref, wu_ref, wd_ref, gf_ref = ffn_refs
    hn = _rms(x, g_ref[...], NORM_EPS).astype(BF16)
    hidden = wg_ref.shape[1]
    th = MXU_DIM if hidden % MXU_DIM == 0 else hidden
    acc = None
    for c in range(hidden // th):
        cols = slice(c * th, (c + 1) * th)
        gate = jnp.dot(hn, wg_ref[:, cols], preferred_element_type=F32)
        up = jnp.dot(hn, wu_ref[:, cols], preferred_element_type=F32)
        act = (jax.nn.silu(gate) * up).astype(BF16)
        part = jnp.dot(act, wd_ref[cols, :], preferred_element_type=F32)
        acc = part if acc is None else acc + part
    out = x + acc
    if final_norm:
        out = _rms(out, gf_ref[...], NORM_EPS)
    return out


def _ffn_specs(ffn_w):
    g, wg, wu, wd, gf, layer = ffn_w

    def one_layer(a):
        return pl.BlockSpec((None,) + a.shape[1:], lambda *_: (layer, 0, 0),
                            pipeline_mode=pl.Buffered(1))

    return [_resident(g), one_layer(wg), one_layer(wu), one_layer(wd), _resident(gf)]


def _ffn_kernel(x_ref, *rest, final_norm):
    o_ref = rest[-1]
    o_ref[...] = _ffn_apply(x_ref[...], rest[:-1], final_norm)


def _ffn(x, ffn_w, tm, final_norm):
    n, d = x.shape
    return pl.pallas_call(
        functools.partial(_ffn_kernel, final_norm=final_norm),
        out_shape=jax.ShapeDtypeStruct((n, d), F32),
        grid=(n // tm,),
        in_specs=[pl.BlockSpec((tm, d), lambda i: (i, 0))] + _ffn_specs(ffn_w),
        out_specs=pl.BlockSpec((tm, d), lambda i: (i, 0)),
        compiler_params=_cparams(1),
        name="ffn",
    )(x, *ffn_w[:5])


def _post_ab_kernel(x_ref, y_ref, a_ref, wglu_ref, bglu_ref, wo_ref, *rest, s5w, final_norm):
    o_ref = rest[-1]
    z = jax.nn.gelu(y_ref[...])
    gl = jnp.dot(z.astype(BF16), wglu_ref[...], preferred_element_type=F32) + bglu_ref[...]
    s5o = (z * jax.nn.sigmoid(gl)).astype(BF16)
    mix = (jnp.dot(s5o, wo_ref[:s5w, :], preferred_element_type=F32)
           + jnp.dot(a_ref[...].astype(BF16), wo_ref[s5w:, :], preferred_element_type=F32))
    o_ref[...] = _ffn_apply(x_ref[...] + mix, rest[:-1], final_norm)


def _post_ab(x, y, attn, wglu_bf, bglu, wo_bf, ffn_w, tm, final_norm):
    n, d = x.shape
    s5w = y.shape[1]
    aw = attn.shape[1]
    row = lambda i: (i, 0)
    return pl.pallas_call(
        functools.partial(_post_ab_kernel, s5w=s5w, final_norm=final_norm),
        out_shape=jax.ShapeDtypeStruct((n, d), F32),
        grid=(n // tm,),
        in_specs=[pl.BlockSpec((tm, d), row), pl.BlockSpec((tm, s5w), row), pl.BlockSpec((tm, aw), row),
                  _resident(wglu_bf), _resident(bglu), _resident(wo_bf)] + _ffn_specs(ffn_w),
        out_specs=pl.BlockSpec((tm, d), row),
        compiler_params=_cparams(1),
        name="post_ab_ffn",
    )(x, y, attn, wglu_bf, bglu, wo_bf, *ffn_w[:5])


def _lru_gates(xc, wg_ref, ba, bx, lam_param):
    xcb = xc.astype(BF16)
    halves = [jnp.dot(xcb[:, h * MXU_DIM:(h + 1) * MXU_DIM], wg_ref[h], preferred_element_type=F32)
              for h in range(wg_ref.shape[0])]
    r_pre = jnp.concatenate([hv[:, :MXU_DIM] for hv in halves], axis=1)
    i_pre = jnp.concatenate([hv[:, MXU_DIM:] for hv in halves], axis=1)
    r = jax.nn.sigmoid(r_pre + ba)
    i = jax.nn.sigmoid(i_pre + bx)
    log_a = -LRU_C * r * jax.nn.softplus(-lam_param)
    a = jnp.exp(log_a)
    b = jnp.sqrt(-jnp.tanh(log_a) * (a * a + 1.0)) * (i * xc)
    return a, b


def _pool_project(pooled, pw_ref, scale):
    pb = pooled.astype(BF16)
    halves = [jnp.dot(pb[:, h * MXU_DIM:(h + 1) * MXU_DIM], pw_ref[h], preferred_element_type=F32)
              for h in range(pw_ref.shape[0])]
    return jnp.concatenate(halves, axis=1) * scale


def _cd_prompt_kernel(x_ref, g_ref, win_ref, cw_ref, cb_ref, wg_ref, ba_ref, bx_ref, lam_ref,
                      pw_ref, psc_ref, wo_ref, fg_ref, fwg_ref, fwu_ref, fwd_ref, fgf_ref,
                      o_ref, conv_ref, lru_ref, pool_ref,
                      xl_ref, xp_ref, s2_ref, s4_ref, a_ref, b_ref, hs_ref, carry_ref,
                      *, tm, w, final_norm):
    ti = pl.program_id(1)
    hist = POOL_BUF + 1
    hp = hist + SUBLANES

    @pl.when(ti == 0)
    def _():
        xl_ref[:hist, :] = jnp.zeros((hist, w), F32)
        xp_ref[:hp, :] = jnp.zeros((hp, w), F32)
        s2_ref[:SUBLANES, :] = jnp.zeros((SUBLANES, w), F32)
        s4_ref[:SUBLANES, :] = jnp.zeros((SUBLANES, w), F32)
        carry_ref[...] = jnp.zeros(carry_ref.shape, F32)

    x = x_ref[...]
    hn = _rms(x, g_ref[...], NORM_EPS).astype(BF16)
    proj = jnp.dot(hn, win_ref[...], preferred_element_type=F32)
    gate = proj[:, :w]
    xl_ref[hist:, :] = proj[:, w:2 * w]
    xp_ref[hp:, :] = proj[:, 2 * w:]

    xc = cb_ref[...] + xl_ref[pl.ds(hist - (CONV_WIDTH - 1), tm), :] * cw_ref[0:1, :]
    for j in range(1, CONV_WIDTH):
        xc = xc + xl_ref[pl.ds(hist - (CONV_WIDTH - 1) + j, tm), :] * cw_ref[j:j + 1, :]

    a, b = _lru_gates(xc, wg_ref, ba_ref[...], bx_ref[...], lam_ref[...])
    a_ref[...] = a
    b_ref[...] = b

    def group(gi, carry):
        row = pl.multiple_of(gi * SUBLANES, SUBLANES)
        ag = a_ref[pl.ds(row, SUBLANES), :]
        bg = b_ref[pl.ds(row, SUBLANES), :]
        ridx = lax.broadcasted_iota(jnp.int32, ag.shape, 0)
        for d in (1, 2, 4):
            a_sh = jnp.where(ridx >= d, pltpu.roll(ag, d, 0), 1.0)
            b_sh = jnp.where(ridx >= d, pltpu.roll(bg, d, 0), 0.0)
            bg = ag * b_sh + bg
            ag = ag * a_sh
        h = bg + ag * carry
        hs_ref[pl.ds(row, SUBLANES), :] = h
        return jnp.broadcast_to(h[SUBLANES - 1:SUBLANES, :], h.shape)

    carry = lax.fori_loop(0, tm // SUBLANES, group, jnp.broadcast_to(carry_ref[...], (SUBLANES, w)))
    carry_ref[...] = carry[0:1, :]
    lru_out = jax.nn.gelu(gate) * hs_ref[...]

    assert POOL_WINDOWS == (2, 4, 8, 16)
    pg = w // len(POOL_WINDOWS)
    span = tm + hist
    s2_ref[pl.ds(SUBLANES, span), :] = (xp_ref[pl.ds(SUBLANES, span), :]
                                        + xp_ref[pl.ds(SUBLANES - 1, span), :])
    s4_ref[pl.ds(SUBLANES, span), pg:] = (s2_ref[pl.ds(SUBLANES, span), pg:]
                                          + s2_ref[pl.ds(SUBLANES - 2, span), pg:])
    s2_ref[pl.ds(SUBLANES, span), 2 * pg:] = (s4_ref[pl.ds(SUBLANES, span), 2 * pg:]
                                              + s4_ref[pl.ds(SUBLANES - 4, span), 2 * pg:])
    sums = (s2_ref[pl.ds(hp, tm), :pg],
            s4_ref[pl.ds(hp, tm), pg:2 * pg],
            s2_ref[pl.ds(hp, tm), 2 * pg:3 * pg],
            s2_ref[pl.ds(hp, tm), 3 * pg:] + s2_ref[pl.ds(hp - SUBLANES, tm), 3 * pg:])
    pos = ti * tm + lax.broadcasted_iota(jnp.int32, (tm, pg), 0)
    pooled = []
    for gidx, win in enumerate(POOL_WINDOWS):
        cur = xp_ref[pl.ds(hp, tm), gidx * pg:(gidx + 1) * pg]
        cnt = jnp.minimum(pos + 1, win).astype(F32)
        pooled.append(sums[gidx] / cnt - cur)
    pool_out = _pool_project(jnp.concatenate(pooled, axis=1), pw_ref, psc_ref[...])

    mix = (jnp.dot(lru_out.astype(BF16), wo_ref[:w, :], preferred_element_type=F32)
           + jnp.dot(pool_out.astype(BF16), wo_ref[w:, :], preferred_element_type=F32))
    o_ref[...] = _ffn_apply(x + mix, (fg_ref, fwg_ref, fwu_ref, fwd_ref, fgf_ref), final_norm)

    conv_ref[...] = xl_ref[pl.ds(hist + tm - (CONV_WIDTH - 1), CONV_WIDTH - 1), :]
    lru_ref[...] = carry_ref[...]
    pool_ref[...] = xp_ref[pl.ds(hp + tm - POOL_BUF, POOL_BUF), :]
    xl_ref[:hist, :] = xl_ref[pl.ds(tm, hist), :]
    xp_ref[pl.ds(SUBLANES, hist), :] = xp_ref[pl.ds(SUBLANES + tm, hist), :]


def _cd_prompt(x, g, cdw, ffn_w, bsz, seq, tm, final_norm):
    n, d = x.shape
    w = cdw[1].shape[1]
    nt = seq // tm
    hist = POOL_BUF + 1
    row = lambda b, i: (b * nt + i, 0)
    st = lambda b, i: (b, 0, 0)
    return pl.pallas_call(
        functools.partial(_cd_prompt_kernel, tm=tm, w=w, final_norm=final_norm),
        out_shape=(jax.ShapeDtypeStruct((n, d), F32),
                   jax.ShapeDtypeStruct((bsz, CONV_WIDTH - 1, w), F32),
                   jax.ShapeDtypeStruct((bsz, 1, w), F32),
                   jax.ShapeDtypeStruct((bsz, POOL_BUF, w), F32)),
        grid=(bsz, nt),
        in_specs=[pl.BlockSpec((tm, d), row), _resident(g)] + [_resident(a) for a in cdw]
                 + _ffn_specs(ffn_w),
        out_specs=(pl.BlockSpec((tm, d), row),
                   pl.BlockSpec((None, CONV_WIDTH - 1, w), st),
                   pl.BlockSpec((None, 1, w), st),
                   pl.BlockSpec((None, POOL_BUF, w), st)),
        scratch_shapes=[pltpu.VMEM((hist + tm, w), F32)]
                       + [pltpu.VMEM((SUBLANES + hist + tm, w), F32)] * 3
                       + [pltpu.VMEM((tm, w), F32)] * 3 + [pltpu.VMEM((1, w), F32)],
        compiler_params=_cparams(2),
        name="cd_prompt_ffn",
    )(x, g, *cdw, *ffn_w[:5])


def _cd_step_kernel(x_ref, g_ref, win_ref, cw_ref, cb_ref, wg_ref, ba_ref, bx_ref, lam_ref,
                    pw_ref, psc_ref, wo_ref, sc_ref, sl_ref, sp_ref,
                    o_ref, conv_ref, lru_ref, pool_ref, *, w):
    x = x_ref[...]
    hn = _rms(x, g_ref[...], NORM_EPS).astype(BF16)
    proj = jnp.dot(hn, win_ref[...], preferred_element_type=F32)
    gate = proj[:, :w]
    xl = proj[:, w:2 * w]
    xp = proj[:, 2 * w:]
    xc = cb_ref[...] + xl * cw_ref[CONV_WIDTH - 1:CONV_WIDTH, :]
    for j in range(CONV_WIDTH - 1):
        xc = xc + sc_ref[j] * cw_ref[j:j + 1, :]
    a, b = _lru_gates(xc, wg_ref, ba_ref[...], bx_ref[...], lam_ref[...])
    h = a * sl_ref[...] + b
    lru_out = jax.nn.gelu(gate) * h

    pg = w // len(POOL_WINDOWS)
    pooled = []
    for gidx, win in enumerate(POOL_WINDOWS):
        cols = slice(gidx * pg, (gidx + 1) * pg)
        cur = xp[:, cols]
        tot = cur
        for kk in range(1, win):
            tot = tot + sp_ref[POOL_BUF - kk][:, cols]
        pooled.append(tot / float(win) - cur)
    pool_out = _pool_project(jnp.concatenate(pooled, axis=1), pw_ref, psc_ref[...])

    mix = (jnp.dot(lru_out.astype(BF16), wo_ref[:w, :], preferred_element_type=F32)
           + jnp.dot(pool_out.astype(BF16), wo_ref[w:, :], preferred_element_type=F32))
    o_ref[...] = x + mix
    for j in range(CONV_WIDTH - 2):
        conv_ref[j] = sc_ref[j + 1]
    conv_ref[CONV_WIDTH - 2] = xl
    lru_ref[...] = h
    for j in range(POOL_BUF - 1):
        pool_ref[j] = sp_ref[j + 1]
    pool_ref[POOL_BUF - 1] = xp


def _cd_step(x, g, cdw, conv_buf, lru_h0, pool_buf):
    n, d = x.shape
    win_bf, cw, cb, wg_bf, ba, bx, lam, pw_bf, psc, wo_bf = cdw
    w = cw.shape[1]
    sc = conv_buf.transpose(1, 0, 2)
    sp = pool_buf.transpose(1, 0, 2)
    o, conv_n, lru_n, pool_n = pl.pallas_call(
        functools.partial(_cd_step_kernel, w=w),
        out_shape=(jax.ShapeDtypeStruct((n, d), F32),
                   jax.ShapeDtypeStruct((CONV_WIDTH - 1, n, w), F32),
                   jax.ShapeDtypeStruct((n, w), F32),
                   jax.ShapeDtypeStruct((POOL_BUF, n, w), F32)),
        compiler_params=pltpu.CompilerParams(vmem_limit_bytes=VMEM_LIMIT),
        name="cd_step",
    )(x, g, win_bf, cw, cb, wg_bf, ba, bx, lam, pw_bf, psc, wo_bf, sc, lru_h0, sp)
    return o, conv_n.transpose(1, 0, 2), lru_n, pool_n.transpose(1, 0, 2)


def _block_diag_tiles(w, tile):
    nb, c, _ = w.shape
    per = tile // c
    w4 = w.reshape(nb // per, per, c, c)
    placed = jnp.einsum('tpij,pq->tpiqj', w4, jnp.eye(per, dtype=w.dtype))
    return placed.reshape(nb // per, tile, tile)


def _rope_tables(pos, rows):
    half = HEAD_DIM // 2
    inv = ROPE_THETA ** (-jnp.arange(half, dtype=F32) / half)
    ang = pos.astype(F32)[:, None] * inv[None, :]
    cos = jnp.cos(ang)
    sin = jnp.sin(ang)
    reps = LANES // HEAD_DIM
    cos_t = jnp.tile(jnp.concatenate([cos, cos], axis=1), (1, reps))
    sin_t = jnp.tile(jnp.concatenate([-sin, sin], axis=1), (1, reps))
    if cos_t.shape[0] != rows:
        cos_t = jnp.broadcast_to(cos_t, (rows, LANES))
        sin_t = jnp.broadcast_to(sin_t, (rows, LANES))
    return cos_t, sin_t


def _row_tile(n, target):
    t = min(n, target)
    while n % t:
        t //= 2
    return t


def kernel(x_prompt, x_sample, cache_k, cache_v, page_table, state_s5_re, state_s5_im, state_conv, state_lru, state_pool, norm_mix, norm_ffn, norm_final, w_in_ab, w_out_ab, s5_a_re, s5_a_im, s5_log_dt, s5_b_re, s5_b_im, s5_c_re, s5_c_im, s5_d, s5_w_glu, s5_b_glu, diff_lq1, diff_lk1, diff_lq2, diff_lk2, diff_subln, w_in_cd, w_out_cd, conv_w, conv_b, lru_wa, lru_ba, lru_wx, lru_bx, lru_lambda, pool_w, pool_scale, ffn_w_gate, ffn_w_up, ffn_w_down):
    bsz, seq, d = x_prompt.shape
    nseq, dec_seq, _ = x_sample.shape
    assert dec_seq == 1
    depth = norm_mix.shape[0]
    npages = page_table.shape[1]
    page_size = cache_k.shape[2]
    past_len = npages * page_size
    assert past_len >= POOL_BUF
    s5w = s5_d.shape[1]
    qkw = cache_k.shape[3] * cache_k.shape[4]
    vw_total = cache_v.shape[3] * cache_v.shape[4]
    heads = cache_v.shape[3]
    n_p = bsz * seq
    row2 = lambda a: a.reshape(1, -1)

    xp = x_prompt.reshape(n_p, d)
    xs = x_sample.reshape(nseq, d)
    tm_p = _row_tile(seq, 512)
    cos_p, sin_p = _rope_tables(jnp.arange(seq, dtype=jnp.int32), seq)
    cos_s, sin_s = _rope_tables(jnp.full((1,), past_len, jnp.int32), nseq)

    outs_p = {'k': [], 'v': [], 're': [], 'im': [], 'conv': [], 'lru': [], 'pool': []}
    outs_s = {'k': [], 'v': [], 're': [], 'im': [], 'conv': [], 'lru': [], 'pool': []}

    ffn_gate_bf = ffn_w_gate.astype(BF16)
    ffn_up_bf = ffn_w_up.astype(BF16)
    ffn_down_bf = ffn_w_down.astype(BF16)

    for l in range(depth):
        j = l // 2
        g_mix = row2(norm_mix[l])
        last = l == depth - 1
        ffn_w = (row2(norm_ffn[l]), ffn_gate_bf, ffn_up_bf, ffn_down_bf, row2(norm_final), l)
        if l % 2 == 0:
            lam_init = 0.8 - 0.6 * math.exp(-0.3 * l)
            w_in = w_in_ab[j].astype(BF16)
            w_out = w_out_ab[j].astype(BF16)
            w_glu = s5_w_glu[j].astype(BF16)
            b_glu = row2(s5_b_glu[j])
            lamp = jnp.stack([diff_lq1[j], diff_lk1[j], diff_lq2[j], diff_lk2[j]]).astype(F32)
            gain = row2(diff_subln[j])
            s5p = (s5_a_re[j], s5_a_im[j], s5_log_dt[j], s5_b_re[j], s5_b_im[j],
                   s5_c_re[j], s5_c_im[j], s5_d[j])
            u, q_bf, k, v, k_bf, v_bf = _inproj_ab(xp, g_mix, w_in, cos_p, sin_p, tm_p, s5w, qkw)
            t_attn = _row_tile(seq, 1024)
            attn, k_t = _attn_prompt(q_bf, k_bf, v_bf, k, lamp, gain, bsz, seq, t_attn, t_attn,
                                     lam_init)
            y, h_re, h_im = _s5_prompt(u, _s5_prompt_tables(s5p, seq), bsz, seq)
            xp = _post_ab(xp, y, attn, w_glu, b_glu, w_out, ffn_w, tm_p, last)
            outs_p['k'].append(k_t.reshape(bsz, qkw // HEAD_DIM, HEAD_DIM, seq).transpose(0, 3, 1, 2))
            outs_p['v'].append(v.reshape(bsz, seq, heads, vw_total // heads))
            outs_p['re'].append(h_re)
            outs_p['im'].append(h_im)
            u, q_bf, k, v, _, _ = _inproj_ab(xs, g_mix, w_in, cos_s, sin_s, nseq, s5w, qkw)
            ck = jnp.transpose(cache_k[j], (0, 2, 3, 1)).reshape(-1, qkw, page_size)
            cv = cache_v[j].reshape(-1, page_size * heads, vw_total // heads)
            attn = _attn_decode(q_bf.astype(F32), k, v.reshape(nseq, vw_total), ck, cv, page_table,
                                lamp, gain,
                                math.gcd(npages, 32), lam_init)
            y, h_re, h_im = _s5_step(u, state_s5_re[j], state_s5_im[j], s5p)
            xs = _post_ab(xs, y, attn, w_glu, b_glu, w_out, ffn_w, nseq, last)
            outs_s['k'].append(k.reshape(nseq, 1, qkw // HEAD_DIM, HEAD_DIM))
            outs_s['v'].append(v.reshape(nseq, 1, heads, vw_total // heads))
            outs_s['re'].append(h_re)
            outs_s['im'].append(h_im)
        else:
            w = conv_w.shape[2]
            wg = jnp.concatenate([_block_diag_tiles(lru_wa[j], MXU_DIM),
                                  _block_diag_tiles(lru_wx[j], MXU_DIM)], axis=2).astype(BF16)
            cdw = (w_in_cd[j].astype(BF16), conv_w[j], row2(conv_b[j]), wg, row2(lru_ba[j]),
                   row2(lru_bx[j]), row2(lru_lambda[j]),
                   _block_diag_tiles(pool_w[j], MXU_DIM).astype(BF16), row2(pool_scale[j]),
                   w_out_cd[j].astype(BF16))
            xp, c_new, h_new, p_new = _cd_prompt(xp, g_mix, cdw, ffn_w, bsz, seq, tm_p, last)
            outs_p['conv'].append(c_new)
            outs_p['lru'].append(h_new.reshape(bsz, w))
            outs_p['pool'].append(p_new)
            xs, c_new, h_new, p_new = _cd_step(xs, g_mix, cdw, state_conv[j], state_lru[j], state_pool[j])
            outs_s['conv'].append(c_new)
            outs_s['lru'].append(h_new)
            outs_s['pool'].append(p_new)
            xs = _ffn(xs, ffn_w, nseq, last)

    sp = {name: jnp.stack(vals) for name, vals in outs_p.items()}
    ss = {name: jnp.stack(vals) for name, vals in outs_s.items()}
    y_prompt = xp.reshape(bsz, seq, d)
    y_sample = xs.reshape(nseq, 1, d)
    return (y_prompt, y_sample, sp['k'], sp['v'], ss['k'], ss['v'], sp['re'], sp['im'], ss['re'], ss['im'],
            sp['conv'], ss['conv'], sp['lru'], ss['lru'], sp['pool'], ss['pool'])
```

```python
import functools
import math

import jax
import jax.numpy as jnp
from jax import lax
from jax.experimental import pallas as pl
from jax.experimental.pallas import tpu as pltpu

F32 = jnp.float32
BF16 = jnp.bfloat16

HEAD_DIM = 64
ROPE_THETA = 10000.0
NORM_EPS = 1e-6
SUBLN_EPS = 1e-5
S5_GROUP = 16
S5_CHUNK = 16
S5_SCAN_STRIDES = (1, 2, 4)
LRU_C = 8.0
CONV_WIDTH = 4
POOL_WINDOWS = (2, 4, 8, 16)
POOL_BUF = max(POOL_WINDOWS) - 1

LANES = 128
SUBLANES = 8
MXU_DIM = 256
VMEM_LIMIT = 56 * 1024 * 1024

NEG_BIG = -1e30
LOG2E = math.log2(math.e)


def _cparams(n_axes, flags=None):
    return pltpu.CompilerParams(dimension_semantics=("arbitrary",) * n_axes,
                                vmem_limit_bytes=VMEM_LIMIT, flags=flags)


def _rms(x, g, eps):
    ms = jnp.mean(x * x, axis=-1, keepdims=True)
    return x * lax.rsqrt(ms + eps) * g


def _lane_tile(x, reps):
    return jnp.concatenate([x] * reps, axis=1)


def _rope(x, cos, sin_signed):
    width = x.shape[1]
    lane = lax.broadcasted_iota(jnp.int32, x.shape, 1)
    half = HEAD_DIM // 2
    rot = jnp.where((lane & half) == 0, pltpu.roll(x, width - half, 1), pltpu.roll(x, half, 1))
    return x * cos + rot * sin_signed


def _inproj_ab_kernel(x_ref, g_ref, w_ref, cos_ref, sin_ref,
                      u_ref, q_ref, k_ref, v_ref, kb_ref, vb_ref, *, s5w, qkw):
    h = _rms(x_ref[...], g_ref[...], NORM_EPS).astype(BF16)
    proj = jnp.dot(h, w_ref[...], preferred_element_type=F32)
    reps = qkw // LANES
    cos = _lane_tile(cos_ref[...], reps)
    sin = _lane_tile(sin_ref[...], reps)
    u_ref[...] = proj[:, :s5w]
    q = _rope(proj[:, s5w:s5w + qkw], cos, sin)
    k = _rope(proj[:, s5w + qkw:s5w + 2 * qkw], cos, sin)
    v = proj[:, s5w + 2 * qkw:]
    q_ref[...] = (q * (HEAD_DIM ** -0.5 * LOG2E)).astype(BF16)
    k_ref[...] = k
    nvh = v.shape[1] // LANES
    for hh in range(nvh):
        v_ref[pl.ds(hh, v.shape[0], stride=nvh), :] = v[:, hh * LANES:(hh + 1) * LANES]
    kb_ref[...] = k.astype(BF16)
    vb_ref[...] = v.astype(BF16)


def _inproj_ab(x, g, w_bf, cos_tab, sin_tab, tm, s5w, qkw):
    n, d = x.shape
    wn = w_bf.shape[1]
    vw = wn - s5w - 2 * qkw
    nvh = vw // LANES
    period = cos_tab.shape[0] // tm
    row = lambda i: (i, 0)
    const = lambda i: (0, 0)
    tab = lambda i: (i % period, 0)
    return pl.pallas_call(
        functools.partial(_inproj_ab_kernel, s5w=s5w, qkw=qkw),
        out_shape=(jax.ShapeDtypeStruct((n, s5w), F32), jax.ShapeDtypeStruct((n, qkw), BF16),
                   jax.ShapeDtypeStruct((n, qkw), F32), jax.ShapeDtypeStruct((n * nvh, LANES), F32),
                   jax.ShapeDtypeStruct((n, qkw), BF16), jax.ShapeDtypeStruct((n, vw), BF16)),
        grid=(n // tm,),
        in_specs=[pl.BlockSpec((tm, d), row), pl.BlockSpec((1, d), const),
                  pl.BlockSpec((d, wn), const), pl.BlockSpec((tm, LANES), tab),
                  pl.BlockSpec((tm, LANES), tab)],
        out_specs=(pl.BlockSpec((tm, s5w), row), pl.BlockSpec((tm, qkw), row),
                   pl.BlockSpec((tm, qkw), row), pl.BlockSpec((tm * nvh, LANES), row),
                   pl.BlockSpec((tm, qkw), row), pl.BlockSpec((tm, vw), row)),
        compiler_params=_cparams(1),
        name="inproj_ab",
    )(x, g, w_bf, cos_tab, sin_tab)


def _diff_lambda(lamp, lam_init):
    s1 = jnp.sum(lamp[0:1] * lamp[1:2], axis=1, keepdims=True)
    s2 = jnp.sum(lamp[2:3] * lamp[3:4], axis=1, keepdims=True)
    return jnp.exp(s1) - jnp.exp(s2) + lam_init


def _subln(o, gain, lam_init):
    ms = jnp.mean(o * o, axis=-1, keepdims=True)
    return o * lax.rsqrt(ms + SUBLN_EPS) * gain * (1.0 - lam_init)


ONES_ROWS = 16


def _attn_prompt_kernel(lamp_ref, q_ref, k_ref, v_ref, gain_ref, kf_ref, o_ref, kt_ref,
                        vt_ref, m_ref, acc_ref, *, tq, tk, lam_init):
    qi = pl.program_id(2)
    seq, vw = v_ref.shape
    per_q = tq // tk

    @pl.when(qi == 0)
    def _():
        for c in range(seq // tk):
            blk = v_ref[c * tk:(c + 1) * tk, :].astype(F32)
            vt_ref[c, :vw, :] = blk.T.astype(BF16)
            vt_ref[c, vw:, :] = jnp.ones((ONES_ROWS, tk), BF16)
            kt_ref[:, c * tk:(c + 1) * tk] = kf_ref[c * tk:(c + 1) * tk, :].T

    q = q_ref[...]
    lane = lax.broadcasted_iota(jnp.int32, q.shape, 1)
    zero = jnp.zeros_like(q)
    qs = (jnp.where(lane < HEAD_DIM, q, zero), jnp.where(lane >= HEAD_DIM, q, zero))
    m_ref[...] = jnp.full(m_ref.shape, NEG_BIG, F32)
    acc_ref[...] = jnp.zeros(acc_ref.shape, F32)

    def chunk(c, diag):
        off = pl.multiple_of(c * tk, tk)
        kc = k_ref[pl.ds(off, tk), :]
        vc = vt_ref[c]

        def scores(j):
            st = lax.dot_general(kc, qs[j], (((1,), (1,)), ((), ())), preferred_element_type=F32)
            if diag is not None:
                key = lax.broadcasted_iota(jnp.int32, st.shape, 0) + diag * tk
                qry = lax.broadcasted_iota(jnp.int32, st.shape, 1)
                st = jnp.where(key <= qry, st, -jnp.inf)
            return st

        def softmax(j, st):
            m_prev = m_ref[j]
            m_new = jnp.maximum(m_prev, jnp.max(st, axis=0, keepdims=True))
            m_ref[j] = m_new
            return jnp.exp2(m_prev - m_new), jnp.exp2(st - m_new).astype(BF16)

        def accumulate(j, alpha, p):
            pv = jnp.dot(vc, p, preferred_element_type=F32)
            acc_ref[j] = alpha * acc_ref[j] + pv

        alpha0, p0 = softmax(0, scores(0))
        st1 = scores(1)
        accumulate(0, alpha0, p0)
        alpha1, p1 = softmax(1, st1)
        accumulate(1, alpha1, p1)

    def body(c, carry):
        chunk(c, None)
        return carry

    def diagonal_chunk(c):
        h = tk // 2
        off = pl.multiple_of(c * tk, tk)
        k_lo = k_ref[pl.ds(off, h), :]
        k_hi = k_ref[pl.ds(off + h, h), :]
        vc = vt_ref[c]

        def masked_scores(kpart, qpart):
            st = lax.dot_general(kpart, qpart, (((1,), (1,)), ((), ())), preferred_element_type=F32)
            key = lax.broadcasted_iota(jnp.int32, st.shape, 0)
            qry = lax.broadcasted_iota(jnp.int32, st.shape, 1)
            return jnp.where(key <= qry, st, -jnp.inf)

        for j in range(2):
            st_lo = masked_scores(k_lo, qs[j])
            st_hi = masked_scores(k_hi, qs[j][h:, :])
            m_prev = m_ref[j]
            m_lo = jnp.maximum(m_prev, jnp.max(st_lo, axis=0, keepdims=True))
            m_new = jnp.concatenate(
                [m_lo[:, :h], jnp.maximum(m_lo[:, h:], jnp.max(st_hi, axis=0, keepdims=True))], axis=1)
            m_ref[j] = m_new
            alpha = jnp.exp2(m_prev - m_new)
            p_lo = jnp.exp2(st_lo - m_new).astype(BF16)
            p_hi = jnp.exp2(st_hi - m_new[:, h:]).astype(BF16)
            pv = jnp.dot(vc[:, :h], p_lo, preferred_element_type=F32)
            pv_hi = jnp.dot(vc[:, h:], p_hi, preferred_element_type=F32)
            pv = jnp.concatenate([pv[:, :h], pv[:, h:] + pv_hi], axis=1)
            acc_ref[j] = alpha * acc_ref[j] + pv

    lax.fori_loop(0, qi * per_q, body, 0)
    if per_q == 1:
        diagonal_chunk(qi)
    else:
        for d in range(per_q):
            chunk(qi * per_q + d, d)

    lam = _diff_lambda(lamp_ref[...], lam_init)
    a0 = acc_ref[0]
    a1 = acc_ref[1]
    ot = a0[:vw] / a0[vw:vw + 1] - lam * (a1[:vw] / a1[vw:vw + 1])
    o_ref[...] = _subln(ot.T, gain_ref[...], lam_init).astype(o_ref.dtype)


def _attn_prompt(q_bf, k_bf, v_bf, k_f32, lamp, gain, bsz, seq, tq, tk, lam_init):
    n, qkw = q_bf.shape
    vw_total = v_bf.shape[1]
    heads = qkw // (2 * HEAD_DIM)
    vw = vw_total // heads
    nq = seq // tq
    pair = lambda b, h, i: (b, h)
    return pl.pallas_call(
        functools.partial(_attn_prompt_kernel, tq=tq, tk=tk, lam_init=lam_init),
        out_shape=(jax.ShapeDtypeStruct((n, vw_total), BF16),
                   jax.ShapeDtypeStruct((bsz * qkw, seq), F32)),
        grid=(bsz, heads, nq),
        in_specs=[pl.BlockSpec((4, HEAD_DIM), lambda b, h, i: (0, 0)),
                  pl.BlockSpec((tq, 2 * HEAD_DIM), lambda b, h, i: (b * nq + i, h)),
                  pl.BlockSpec((seq, 2 * HEAD_DIM), pair),
                  pl.BlockSpec((seq, vw), pair),
                  pl.BlockSpec((1, vw), lambda b, h, i: (0, 0)),
                  pl.BlockSpec((seq, 2 * HEAD_DIM), pair)],
        out_specs=(pl.BlockSpec((tq, vw), lambda b, h, i: (b * nq + i, h)),
                   pl.BlockSpec((2 * HEAD_DIM, seq), lambda b, h, i: (b * heads + h, 0))),
        scratch_shapes=[pltpu.VMEM((seq // tk, vw + ONES_ROWS, tk), BF16),
                        pltpu.VMEM((2, 1, tq), F32),
                        pltpu.VMEM((2, vw + ONES_ROWS, tq), F32)],
        compiler_params=_cparams(3),
        name="attn_prompt",
    )(lamp, q_bf, k_bf, v_bf, gain, k_f32)


def _attn_decode_kernel(pt_ref, lamp_ref, q_ref, kself_ref, vself_ref, gain_ref, *rest,
                        pages, lam_init, vw):
    k_refs = rest[:pages]
    v_refs = rest[pages:2 * pages]
    o_ref = rest[2 * pages]
    m_ref, l_ref, acc_ref = rest[2 * pages + 1:]
    j = pl.program_id(1)
    nh2 = m_ref.shape[0]
    width = q_ref.shape[1]

    @pl.when(j == 0)
    def _():
        m_ref[...] = jnp.full(m_ref.shape, NEG_BIG, F32)
        l_ref[...] = jnp.zeros(l_ref.shape, F32)
        acc_ref[...] = jnp.zeros(acc_ref.shape, F32)

    rowh = lax.broadcasted_iota(jnp.int32, (nh2, width), 0)
    laneh = lax.broadcasted_iota(jnp.int32, (nh2, width), 1) // HEAD_DIM
    q8 = jnp.where(rowh == laneh, jnp.broadcast_to(q_ref[...], (nh2, width)), 0.0)
    q8b = q8.astype(BF16)

    s_parts = [jnp.dot(q8b, k_refs[i][...].astype(BF16), preferred_element_type=F32)
               for i in range(pages)]
    s = jnp.concatenate(s_parts, axis=1)
    m_prev = m_ref[...]
    m_new = jnp.maximum(m_prev, jnp.max(s, axis=1, keepdims=True))
    alpha = jnp.exp2(m_prev - m_new)
    p = jnp.exp2(s - _lane_tile(m_new, s.shape[1] // LANES))
    l_ref[...] = alpha * l_ref[...] + jnp.sum(p, axis=1, keepdims=True)
    ps = k_refs[0].shape[1]
    nvh = width // vw
    pb = p.astype(BF16)
    pv = None
    for i in range(pages):
        cols = [jnp.dot(pb[:, i * ps:(i + 1) * ps],
                        v_refs[i][pl.ds(h, ps, stride=nvh), :].astype(BF16),
                        preferred_element_type=F32) for h in range(nvh)]
        part = jnp.concatenate(cols, axis=1)
        pv = part if pv is None else pv + part
    acc_ref[...] = _lane_tile(alpha, width // LANES) * acc_ref[...] + pv
    m_ref[...] = m_new

    @pl.when(j == pl.num_programs(1) - 1)
    def _():
        s_self = jnp.sum(q8 * kself_ref[...], axis=1, keepdims=True)
        m_old = m_ref[...]
        m_fin = jnp.maximum(m_old, s_self)
        a_fin = jnp.exp2(m_old - m_fin)
        p_self = jnp.exp2(s_self - m_fin)
        l_fin = a_fin * l_ref[...] + p_self
        acc = (_lane_tile(a_fin, width // LANES) * acc_ref[...]
               + _lane_tile(p_self, width // LANES) * vself_ref[...])
        o8 = acc / _lane_tile(l_fin, width // LANES)
        lam = _diff_lambda(lamp_ref[...], lam_init)
        vhead = lax.broadcasted_iota(jnp.int32, (nh2, width), 1) // vw
        sign = jnp.where((rowh & 1) == 0, 1.0, -lam)
        coef = jnp.where(vhead == rowh // 2, sign, 0.0)
        o = jnp.sum(o8 * coef, axis=0, keepdims=True)
        gain = gain_ref[...]
        outs = [_subln(o[:, h * vw:(h + 1) * vw], gain, lam_init) for h in range(width // vw)]
        o_ref[...] = jnp.concatenate(outs, axis=1)


def _attn_decode(q, k_self, v_self, cache_kt, cache_v2, page_table, lamp, gain, pages, lam_init):
    nseq, width = q.shape
    npages = page_table.shape[1]
    ps = cache_kt.shape[2]
    vw = gain.shape[1]
    nh2 = width // HEAD_DIM
    pt = page_table.reshape(-1)
    q3, k3, v3 = (a.reshape(nseq, 1, width) for a in (q, k_self, v_self))
    seq_spec = pl.BlockSpec((None, 1, width), lambda b, j, pt: (b, 0, 0))

    def page_spec(i, shape):
        return pl.BlockSpec((None,) + shape,
                            lambda b, j, pt: (pt[b * npages + j * pages + i], 0, 0))

    grid_spec = pltpu.PrefetchScalarGridSpec(
        num_scalar_prefetch=1,
        grid=(nseq, npages // pages),
        in_specs=[pl.BlockSpec((4, HEAD_DIM), lambda b, j, pt: (0, 0)), seq_spec, seq_spec, seq_spec,
                  pl.BlockSpec((1, vw), lambda b, j, pt: (0, 0))]
                 + [page_spec(i, cache_kt.shape[1:]) for i in range(pages)]
                 + [page_spec(i, cache_v2.shape[1:]) for i in range(pages)],
        out_specs=seq_spec,
        scratch_shapes=[pltpu.VMEM((nh2, LANES), F32), pltpu.VMEM((nh2, LANES), F32),
                        pltpu.VMEM((nh2, width), F32)],
    )
    out = pl.pallas_call(
        functools.partial(_attn_decode_kernel, pages=pages, lam_init=lam_init, vw=vw),
        out_shape=jax.ShapeDtypeStruct((nseq, 1, width), F32),
        grid_spec=grid_spec,
        compiler_params=_cparams(2),
        name="attn_decode",
    )(pt, lamp, q3, k3, v3, gain, *([cache_kt] * pages), *([cache_v2] * pages))
    return out.reshape(nseq, width)


def _cmul(a, b):
    return a[0] * b[0] - a[1] * b[1], a[0] * b[1] + a[1] * b[0]


def _s5_discretize(a_re, a_im, log_dt, b_re, b_im):
    a_re, a_im = a_re.astype(F32), a_im.astype(F32)
    step = jnp.exp(log_dt.astype(F32))[:, None]
    llb = (a_re * step, a_im * step)
    lb = _s5_power(llb, 1.0)
    num = (lb[0] - 1.0, lb[1])
    den = a_re * a_re + a_im * a_im
    quo = ((num[0] * a_re + num[1] * a_im) / den, (num[1] * a_re - num[0] * a_im) / den)
    b_bar = _cmul((quo[0][:, :, None], quo[1][:, :, None]), (b_re.astype(F32), b_im.astype(F32)))
    return llb, b_bar


def _s5_power(llb, k):
    mag = jnp.exp(llb[0] * k)
    return mag * jnp.cos(llb[1] * k), mag * jnp.sin(llb[1] * k)


def _s5_prompt_tables(params, seq):
    a_re, a_im, log_dt, b_re, b_im, c_re, c_im, d_skip = params
    g, p = a_re.shape
    ch, t = S5_GROUP, S5_CHUNK
    llb, b_bar = _s5_discretize(a_re, a_im, log_dt, b_re, b_im)
    c = (c_re.astype(F32), c_im.astype(F32))
    lags = jnp.arange(t + 1, dtype=F32)[:, None, None]
    powk = _s5_power((llb[0][None], llb[1][None]), lags)
    cp = _cmul((c[0][None], c[1][None]),
               (powk[0][:, :, None, :], powk[1][:, :, None, :]))
    klag = (jnp.einsum('kgcp,gpd->gdkc', cp[0][:t], b_bar[0])
            - jnp.einsum('kgcp,gpd->gdkc', cp[1][:t], b_bar[1]))
    skip = d_skip.astype(F32).reshape(g, ch)[:, None, :] * jnp.eye(ch, dtype=F32)[None]
    klag = klag.at[:, :, 0, :].add(skip)
    idx = jnp.arange(t)
    place = ((idx[None, None, :] - idx[:, None, None] == idx[None, :, None])[:, :, None, :, None]
             & jnp.eye(ch, dtype=bool)[None, None, :, None, :])
    place = place.reshape(t, t * ch, t * ch).astype(F32)
    m_mat = jnp.einsum('xq,sqr->sxr', klag.reshape(g * ch, t * ch), place)
    pw_rev = (powk[0][:t][::-1].transpose(1, 0, 2)[:, :, None, :],
              powk[1][:t][::-1].transpose(1, 0, 2)[:, :, None, :])
    bb_t = (b_bar[0].transpose(0, 2, 1)[:, None], b_bar[1].transpose(0, 2, 1)[:, None])
    e_c = _cmul(pw_rev, bb_t)
    e_re = e_c[0].reshape(g, t * ch, p)
    e_im = e_c[1].reshape(g, t * ch, p)
    e_mat = jnp.concatenate([e_re, e_im, e_im, e_re], axis=-1)
    f_re = cp[0][1:t + 1].transpose(1, 0, 2, 3).reshape(g, t * ch, p)
    f_im = cp[1][1:t + 1].transpose(1, 0, 2, 3).reshape(g, t * ch, p)
    f_mat = jnp.concatenate([f_re, -f_im], axis=-1).transpose(0, 2, 1)
    ks = (jnp.arange(1, SUBLANES + 1, dtype=F32) * t)[None, :, None]
    pw = _s5_power((llb[0][:, None, :], llb[1][:, None, :]), ks)
    a1 = jnp.concatenate([pw[0], pw[0]], axis=-1)
    a2 = jnp.concatenate([-pw[1], pw[1]], axis=-1)
    rows = jnp.arange(SUBLANES)[None, :, None]
    parts = []
    for d in S5_SCAN_STRIDES:
        parts.append(jnp.where(rows >= d, a1[:, d - 1:d, :], 0.0))
        parts.append(jnp.where(rows >= d, a2[:, d - 1:d, :], 0.0))
    tab = jnp.concatenate(parts + [a1, a2], axis=1)
    return m_mat.astype(BF16), e_mat.astype(BF16), f_mat.astype(BF16), tab


def _s5_lane_permutation():
    n = SUBLANES * SUBLANES * S5_GROUP
    idx = jnp.arange(n).reshape(SUBLANES, SUBLANES, S5_GROUP).transpose(1, 0, 2).reshape(n)
    return jax.nn.one_hot(idx, n, dtype=BF16)


def _s5_prompt_kernel(u_ref, perm_ref, e_ref, m_ref, f_ref, tab_ref, y_ref, hl_ref,
                      us_ref, ee_ref, h_ref, ycat_ref):
    t = S5_CHUNK
    ng = SUBLANES
    nsub = u_ref.shape[0] // t
    sw = tab_ref.shape[2]
    half_t = t // 2
    perm = perm_ref[...]
    for half in range(2):
        xs = [u_ref[pl.ds(half * half_t + s, nsub, stride=t), :].astype(BF16) for s in range(half_t)]
        uh = jnp.dot(jnp.concatenate(xs, axis=1), perm, preferred_element_type=F32).astype(BF16)
        for gl in range(ng):
            us_ref[gl, :, half * LANES:(half + 1) * LANES] = uh[:, gl * LANES:(gl + 1) * LANES]
    for gl in range(ng):
        ee_ref[gl] = jnp.dot(us_ref[gl], e_ref[gl], preferred_element_type=F32)
        h_ref[gl, :SUBLANES, :] = jnp.zeros((SUBLANES, sw), F32)

    def step(i, carry):
        row = pl.multiple_of(i * SUBLANES, SUBLANES)
        out = []
        for gl in range(ng):
            h = ee_ref[gl, pl.ds(row, SUBLANES), :sw]
            hs = ee_ref[gl, pl.ds(row, SUBLANES), sw:]
            for di, d in enumerate(S5_SCAN_STRIDES):
                t1 = tab_ref[gl, 2 * di * SUBLANES:(2 * di + 1) * SUBLANES, :]
                t2 = tab_ref[gl, (2 * di + 1) * SUBLANES:(2 * di + 2) * SUBLANES, :]
                sh = pltpu.roll(h, d, 0)
                shs = pltpu.roll(hs, d, 0)
                h, hs = h + t1 * sh + t2 * shs, hs + t1 * shs - t2 * sh
            nd = 2 * len(S5_SCAN_STRIDES)
            q1 = tab_ref[gl, nd * SUBLANES:(nd + 1) * SUBLANES, :]
            q2 = tab_ref[gl, (nd + 1) * SUBLANES:(nd + 2) * SUBLANES, :]
            cb, cbs = carry[2 * gl], carry[2 * gl + 1]
            hf = h + q1 * cb + q2 * cbs
            hfs = hs + q1 * cbs - q2 * cb
            h_ref[gl, pl.ds(SUBLANES + row, SUBLANES), :] = hf
            out.append(jnp.broadcast_to(hf[SUBLANES - 1:SUBLANES, :], hf.shape))
            out.append(jnp.broadcast_to(hfs[SUBLANES - 1:SUBLANES, :], hfs.shape))
        return tuple(out)

    zero = jnp.zeros((SUBLANES, sw), F32)
    carry = lax.fori_loop(0, nsub // SUBLANES, step, (zero,) * (2 * ng))
    for gl in range(ng):
        hl_ref[gl:gl + 1, :] = carry[2 * gl][0:1, :]
        hin = h_ref[gl, pl.ds(SUBLANES - 1, nsub), :].astype(BF16)
        m_g = m_ref[:, gl * S5_GROUP:(gl + 1) * S5_GROUP, :].reshape(t * S5_GROUP, t * S5_GROUP)
        y = (jnp.dot(us_ref[gl], m_g, preferred_element_type=F32)
             + jnp.dot(hin, f_ref[gl], preferred_element_type=F32))
        for half in range(2):
            ycat_ref[half, :, gl * LANES:(gl + 1) * LANES] = y[:, half * LANES:(half + 1) * LANES]
    for half in range(2):
        yc = ycat_ref[half]
        hi = yc.astype(BF16)
        lo = (yc - hi.astype(F32)).astype(BF16)
        z = (jnp.dot(hi, perm, preferred_element_type=F32)
             + jnp.dot(lo, perm, preferred_element_type=F32))
        for s in range(half_t):
            y_ref[pl.ds(half * half_t + s, nsub, stride=t), :] = z[:, s * LANES:(s + 1) * LANES]


def _s5_prompt(u, tables, bsz, seq):
    m_mat, e_mat, f_mat, tab = tables
    g = e_mat.shape[0]
    p = f_mat.shape[1] // 2
    t, ch = S5_CHUNK, S5_GROUP
    ng = LANES // ch
    assert ng == SUBLANES and t * ch == 2 * LANES
    n, w = u.shape
    nsub = seq // t
    perm = _s5_lane_permutation()
    npm = perm.shape[0]
    ntab = tab.shape[1]
    slab = lambda s, b: (s, 0, 0)
    y, hl = pl.pallas_call(
        _s5_prompt_kernel,
        out_shape=(jax.ShapeDtypeStruct((n, w), F32), jax.ShapeDtypeStruct((bsz, g, 2 * p), F32)),
        grid=(w // LANES, bsz),
        in_specs=[pl.BlockSpec((seq, LANES), lambda s, b: (b, s)),
                  pl.BlockSpec((npm, npm), lambda s, b: (0, 0)),
                  pl.BlockSpec((ng, t * ch, 4 * p), slab),
                  pl.BlockSpec((t, ng * ch, t * ch), lambda s, b: (0, s, 0)),
                  pl.BlockSpec((ng, 2 * p, t * ch), slab),
                  pl.BlockSpec((ng, ntab, 2 * p), slab)],
        out_specs=(pl.BlockSpec((seq, LANES), lambda s, b: (b, s)),
                   pl.BlockSpec((None, ng, 2 * p), lambda s, b: (b, s, 0))),
        scratch_shapes=[pltpu.VMEM((ng, nsub, t * ch), BF16),
                        pltpu.VMEM((ng, nsub, 4 * p), F32),
                        pltpu.VMEM((ng, SUBLANES + nsub, 2 * p), F32),
                        pltpu.VMEM((2, nsub, ng * LANES), F32)],
        compiler_params=_cparams(2),
        name="s5_prompt",
    )(u, perm, e_mat, m_mat, f_mat, tab)
    return y, hl[..., :p], hl[..., p:]


def _s5_step_kernel(u_ref, h_ref, hs_ref, bt_ref, ct_ref, a1_ref, a2_ref, d_ref, y_ref, hn_ref):
    hp = lax.Precision.HIGHEST
    u = u_ref[...]
    bu = jnp.einsum('gbc,gcp->gbp', u, bt_ref[...], precision=hp, preferred_element_type=F32)
    hn = a1_ref[...] * h_ref[...] + a2_ref[...] * hs_ref[...] + bu
    hn_ref[...] = hn
    y = jnp.einsum('gbp,gpc->gbc', hn, ct_ref[...], precision=hp, preferred_element_type=F32)
    y_ref[...] = y + d_ref[...] * u


def _s5_step(u, h0_re, h0_im, params):
    a_re, a_im, log_dt, b_re, b_im, c_re, c_im, d_skip = params
    g, p = a_re.shape
    ch = S5_GROUP
    bsz = u.shape[0]
    llb, b_bar = _s5_discretize(a_re, a_im, log_dt, b_re, b_im)
    lam_bar = _s5_power(llb, 1.0)
    bt = jnp.concatenate([b_bar[0].transpose(0, 2, 1), b_bar[1].transpose(0, 2, 1)], axis=-1)
    ct = jnp.concatenate([c_re.astype(F32), -c_im.astype(F32)], axis=-1).transpose(0, 2, 1)
    a1 = jnp.concatenate([lam_bar[0]] * 2, axis=-1)[:, None, :]
    a2 = jnp.concatenate([-lam_bar[1], lam_bar[1]], axis=-1)[:, None, :]
    h = jnp.concatenate([h0_re, h0_im], axis=-1).transpose(1, 0, 2).astype(F32)
    hs = jnp.concatenate([h0_im, h0_re], axis=-1).transpose(1, 0, 2).astype(F32)
    ug = u.reshape(bsz, g, ch).transpose(1, 0, 2)
    dg = d_skip.astype(F32).reshape(g, 1, ch)
    y, hn = pl.pallas_call(
        _s5_step_kernel,
        out_shape=(jax.ShapeDtypeStruct((g, bsz, ch), F32), jax.ShapeDtypeStruct((g, bsz, 2 * p), F32)),
        compiler_params=pltpu.CompilerParams(vmem_limit_bytes=VMEM_LIMIT),
        name="s5_step",
    )(ug, h, hs, bt, ct, a1, a2, dg)
    hn = hn.transpose(1, 0, 2)
    return y.transpose(1, 0, 2).reshape(bsz, g * ch), hn[..., :p], hn[..., p:]


def _resident(a):
    zeros = (0,) * a.ndim
    return pl.BlockSpec(a.shape, lambda *_: zeros, pipeline_mode=pl.Buffered(1))


def _ffn_apply(x, ffn_refs, final_norm):
    g_ref, wg_ref, wu_ref, wd_ref, gf_ref = ffn_refs
    hn = _rms(x, g_ref[...], NORM_EPS).astype(BF16)
    hidden = wg_ref.shape[1]
    th = MXU_DIM if hidden % MXU_DIM == 0 else hidden
    acc = None
    for c in range(hidden // th):
        cols = slice(c * th, (c + 1) * th)
        gate = jnp.dot(hn, wg_ref[:, cols], preferred_element_type=F32)
        up = jnp.dot(hn, wu_ref[:, cols], preferred_element_type=F32)
        act = (jax.nn.silu(gate) * up).astype(BF16)
        part = jnp.dot(act, wd_ref[cols, :], preferred_element_type=F32)
        acc = part if acc is None else acc + part
    out = x + acc
    if final_norm:
        out = _rms(out, gf_ref[...], NORM_EPS)
    return out


def _ffn_specs(ffn_w):
    g, wg, wu, wd, gf, layer = ffn_w

    def one_layer(a):
        return pl.BlockSpec((None,) + a.shape[1:], lambda *_: (layer, 0, 0),
                            pipeline_mode=pl.Buffered(1))

    return [_resident(g), one_layer(wg), one_layer(wu), one_layer(wd), _resident(gf)]


def _ffn_kernel(x_ref, *rest, final_norm):
    o_ref = rest[-1]
    o_ref[...] = _ffn_apply(x_ref[...], rest[:-1], final_norm)


def _ffn(x, ffn_w, tm, final_norm):
    n, d = x.shape
    return pl.pallas_call(
        functools.partial(_ffn_kernel, final_norm=final_norm),
        out_shape=jax.ShapeDtypeStruct((n, d), F32),
        grid=(n // tm,),
        in_specs=[pl.BlockSpec((tm, d), lambda i: (i, 0))] + _ffn_specs(ffn_w),
        out_specs=pl.BlockSpec((tm, d), lambda i: (i, 0)),
        compiler_params=_cparams(1),
        name="ffn",
    )(x, *ffn_w[:5])


def _post_ab_kernel(x_ref, y_ref, a_ref, wglu_ref, bglu_ref, wo_ref, *rest, s5w, final_norm):
    o_ref = rest[-1]
    z = jax.nn.gelu(y_ref[...])
    gl = jnp.dot(z.astype(BF16), wglu_ref[...], preferred_element_type=F32) + bglu_ref[...]
    s5o = (z * jax.nn.sigmoid(gl)).astype(BF16)
    mix = (jnp.dot(s5o, wo_ref[:s5w, :], preferred_element_type=F32)
           + jnp.dot(a_ref[...].astype(BF16), wo_ref[s5w:, :], preferred_element_type=F32))
    o_ref[...] = _ffn_apply(x_ref[...] + mix, rest[:-1], final_norm)


def _post_ab(x, y, attn, wglu_bf, bglu, wo_bf, ffn_w, tm, final_norm):
    n, d = x.shape
    s5w = y.shape[1]
    aw = attn.shape[1]
    row = lambda i: (i, 0)
    return pl.pallas_call(
        functools.partial(_post_ab_kernel, s5w=s5w, final_norm=final_norm),
        out_shape=jax.ShapeDtypeStruct((n, d), F32),
        grid=(n // tm,),
        in_specs=[pl.BlockSpec((tm, d), row), pl.BlockSpec((tm, s5w), row), pl.BlockSpec((tm, aw), row),
                  _resident(wglu_bf), _resident(bglu), _resident(wo_bf)] + _ffn_specs(ffn_w),
        out_specs=pl.BlockSpec((tm, d), row),
        compiler_params=_cparams(1),
        name="post_ab_ffn",
    )(x, y, attn, wglu_bf, bglu, wo_bf, *ffn_w[:5])


def _lru_gates(xc, wg_ref, ba, bx, lam_param):
    xcb = xc.astype(BF16)
    halves = [jnp.dot(xcb[:, h * MXU_DIM:(h + 1) * MXU_DIM], wg_ref[h], preferred_element_type=F32)
              for h in range(wg_ref.shape[0])]
    r_pre = jnp.concatenate([hv[:, :MXU_DIM] for hv in halves], axis=1)
    i_pre = jnp.concatenate([hv[:, MXU_DIM:] for hv in halves], axis=1)
    r = jax.nn.sigmoid(r_pre + ba)
    i = jax.nn.sigmoid(i_pre + bx)
    log_a = -LRU_C * r * jax.nn.softplus(-lam_param)
    a = jnp.exp(log_a)
    b = jnp.sqrt(-jnp.tanh(log_a) * (a * a + 1.0)) * (i * xc)
    return a, b


def _pool_project(pooled, pw_ref, scale):
    pb = pooled.astype(BF16)
    halves = [jnp.dot(pb[:, h * MXU_DIM:(h + 1) * MXU_DIM], pw_ref[h], preferred_element_type=F32)
              for h in range(pw_ref.shape[0])]
    return jnp.concatenate(halves, axis=1) * scale


def _cd_prompt_kernel(x_ref, g_ref, win_ref, cw_ref, cb_ref, wg_ref, ba_ref, bx_ref, lam_ref,
                      pw_ref, psc_ref, wo_ref, fg_ref, fwg_ref, fwu_ref, fwd_ref, fgf_ref,
                      o_ref, conv_ref, lru_ref, pool_ref,
                      xl_ref, xp_ref, s2_ref, s4_ref, a_ref, b_ref, hs_ref, carry_ref,
                      *, tm, w, final_norm):
    ti = pl.program_id(1)
    hist = POOL_BUF + 1
    hp = hist + SUBLANES

    @pl.when(ti == 0)
    def _():
        xl_ref[:hist, :] = jnp.zeros((hist, w), F32)
        xp_ref[:hp, :] = jnp.zeros((hp, w), F32)
        s2_ref[:SUBLANES, :] = jnp.zeros((SUBLANES, w), F32)
        s4_ref[:SUBLANES, :] = jnp.zeros((SUBLANES, w), F32)
        carry_ref[...] = jnp.zeros(carry_ref.shape, F32)

    x = x_ref[...]
    hn = _rms(x, g_ref[...], NORM_EPS).astype(BF16)
    proj = jnp.dot(hn, win_ref[...], preferred_element_type=F32)
    gate = proj[:, :w]
    xl_ref[hist:, :] = proj[:, w:2 * w]
    xp_ref[hp:, :] = proj[:, 2 * w:]

    xc = cb_ref[...] + xl_ref[pl.ds(hist - (CONV_WIDTH - 1), tm), :] * cw_ref[0:1, :]
    for j in range(1, CONV_WIDTH):
        xc = xc + xl_ref[pl.ds(hist - (CONV_WIDTH - 1) + j, tm), :] * cw_ref[j:j + 1, :]

    a, b = _lru_gates(xc, wg_ref, ba_ref[...], bx_ref[...], lam_ref[...])
    a_ref[...] = a
    b_ref[...] = b

    def group(gi, carry):
        row = pl.multiple_of(gi * SUBLANES, SUBLANES)
        ag = a_ref[pl.ds(row, SUBLANES), :]
        bg = b_ref[pl.ds(row, SUBLANES), :]
        ridx = lax.broadcasted_iota(jnp.int32, ag.shape, 0)
        for d in (1, 2, 4):
            a_sh = jnp.where(ridx >= d, pltpu.roll(ag, d, 0), 1.0)
            b_sh = jnp.where(ridx >= d, pltpu.roll(bg, d, 0), 0.0)
            bg = ag * b_sh + bg
            ag = ag * a_sh
        h = bg + ag * carry
        hs_ref[pl.ds(row, SUBLANES), :] = h
        return jnp.broadcast_to(h[SUBLANES - 1:SUBLANES, :], h.shape)

    carry = lax.fori_loop(0, tm // SUBLANES, group, jnp.broadcast_to(carry_ref[...], (SUBLANES, w)))
    carry_ref[...] = carry[0:1, :]
    lru_out = jax.nn.gelu(gate) * hs_ref[...]

    assert POOL_WINDOWS == (2, 4, 8, 16)
    pg = w // len(POOL_WINDOWS)
    span = tm + hist
    s2_ref[pl.ds(SUBLANES, span), :] = (xp_ref[pl.ds(SUBLANES, span), :]
                                        + xp_ref[pl.ds(SUBLANES - 1, span), :])
    s4_ref[pl.ds(SUBLANES, span), pg:] = (s2_ref[pl.ds(SUBLANES, span), pg:]
                                          + s2_ref[pl.ds(SUBLANES - 2, span), pg:])
    s2_ref[pl.ds(SUBLANES, span), 2 * pg:] = (s4_ref[pl.ds(SUBLANES, span), 2 * pg:]
                                              + s4_ref[pl.ds(SUBLANES - 4, span), 2 * pg:])
    sums = (s2_ref[pl.ds(hp, tm), :pg],
            s4_ref[pl.ds(hp, tm), pg:2 * pg],
            s2_ref[pl.ds(hp, tm), 2 * pg:3 * pg],
            s2_ref[pl.ds(hp, tm), 3 * pg:] + s2_ref[pl.ds(hp - SUBLANES, tm), 3 * pg:])
    pos = ti * tm + lax.broadcasted_iota(jnp.int32, (tm, pg), 0)
    pooled = []
    for gidx, win in enumerate(POOL_WINDOWS):
        cur = xp_ref[pl.ds(hp, tm), gidx * pg:(gidx + 1) * pg]
        cnt = jnp.minimum(pos + 1, win).astype(F32)
        pooled.append(sums[gidx] / cnt - cur)
    pool_out = _pool_project(jnp.concatenate(pooled, axis=1), pw_ref, psc_ref[...])

    mix = (jnp.dot(lru_out.astype(BF16), wo_ref[:w, :], preferred_element_type=F32)
           + jnp.dot(pool_out.astype(BF16), wo_ref[w:, :], preferred_element_type=F32))
    o_ref[...] = _ffn_apply(x + mix, (fg_ref, fwg_ref, fwu_ref, fwd_ref, fgf_ref), final_norm)

    conv_ref[...] = xl_ref[pl.ds(hist + tm - (CONV_WIDTH - 1), CONV_WIDTH - 1), :]
    lru_ref[...] = carry_ref[...]
    pool_ref[...] = xp_ref[pl.ds(hp + tm - POOL_BUF, POOL_BUF), :]
    xl_ref[:hist, :] = xl_ref[pl.ds(tm, hist), :]
    xp_ref[pl.ds(SUBLANES, hist), :] = xp_ref[pl.ds(SUBLANES + tm, hist), :]


def _cd_prompt(x, g, cdw, ffn_w, bsz, seq, tm, final_norm):
    n, d = x.shape
    w = cdw[1].shape[1]
    nt = seq // tm
    hist = POOL_BUF + 1
    row = lambda b, i: (b * nt + i, 0)
    st = lambda b, i: (b, 0, 0)
    return pl.pallas_call(
        functools.partial(_cd_prompt_kernel, tm=tm, w=w, final_norm=final_norm),
        out_shape=(jax.ShapeDtypeStruct((n, d), F32),
                   jax.ShapeDtypeStruct((bsz, CONV_WIDTH - 1, w), F32),
                   jax.ShapeDtypeStruct((bsz, 1, w), F32),
                   jax.ShapeDtypeStruct((bsz, POOL_BUF, w), F32)),
        grid=(bsz, nt),
        in_specs=[pl.BlockSpec((tm, d), row), _resident(g)] + [_resident(a) for a in cdw]
                 + _ffn_specs(ffn_w),
        out_specs=(pl.BlockSpec((tm, d), row),
                   pl.BlockSpec((None, CONV_WIDTH - 1, w), st),
                   pl.BlockSpec((None, 1, w), st),
                   pl.BlockSpec((None, POOL_BUF, w), st)),
        scratch_shapes=[pltpu.VMEM((hist + tm, w), F32)]
                       + [pltpu.VMEM((SUBLANES + hist + tm, w), F32)] * 3
                       + [pltpu.VMEM((tm, w), F32)] * 3 + [pltpu.VMEM((1, w), F32)],
        compiler_params=_cparams(2),
        name="cd_prompt_ffn",
    )(x, g, *cdw, *ffn_w[:5])


def _cd_step_kernel(x_ref, g_ref, win_ref, cw_ref, cb_ref, wg_ref, ba_ref, bx_ref, lam_ref,
                    pw_ref, psc_ref, wo_ref, sc_ref, sl_ref, sp_ref,
                    o_ref, conv_ref, lru_ref, pool_ref, *, w):
    x = x_ref[...]
    hn = _rms(x, g_ref[...], NORM_EPS).astype(BF16)
    proj = jnp.dot(hn, win_ref[...], preferred_element_type=F32)
    gate = proj[:, :w]
    xl = proj[:, w:2 * w]
    xp = proj[:, 2 * w:]
    xc = cb_ref[...] + xl * cw_ref[CONV_WIDTH - 1:CONV_WIDTH, :]
    for j in range(CONV_WIDTH - 1):
        xc = xc + sc_ref[j] * cw_ref[j:j + 1, :]
    a, b = _lru_gates(xc, wg_ref, ba_ref[...], bx_ref[...], lam_ref[...])
    h = a * sl_ref[...] + b
    lru_out = jax.nn.gelu(gate) * h

    pg = w // len(POOL_WINDOWS)
    pooled = []
    for gidx, win in enumerate(POOL_WINDOWS):
        cols = slice(gidx * pg, (gidx + 1) * pg)
        cur = xp[:, cols]
        tot = cur
        for kk in range(1, win):
            tot = tot + sp_ref[POOL_BUF - kk][:, cols]
        pooled.append(tot / float(win) - cur)
    pool_out = _pool_project(jnp.concatenate(pooled, axis=1), pw_ref, psc_ref[...])

    mix = (jnp.dot(lru_out.astype(BF16), wo_ref[:w, :], preferred_element_type=F32)
           + jnp.dot(pool_out.astype(BF16), wo_ref[w:, :], preferred_element_type=F32))
    o_ref[...] = x + mix
    for j in range(CONV_WIDTH - 2):
        conv_ref[j] = sc_ref[j + 1]
    conv_ref[CONV_WIDTH - 2] = xl
    lru_ref[...] = h
    for j in range(POOL_BUF - 1):
        pool_ref[j] = sp_ref[j + 1]
    pool_ref[POOL_BUF - 1] = xp


def _cd_step(x, g, cdw, conv_buf, lru_h0, pool_buf):
    n, d = x.shape
    win_bf, cw, cb, wg_bf, ba, bx, lam, pw_bf, psc, wo_bf = cdw
    w = cw.shape[1]
    sc = conv_buf.transpose(1, 0, 2)
    sp = pool_buf.transpose(1, 0, 2)
    o, conv_n, lru_n, pool_n = pl.pallas_call(
        functools.partial(_cd_step_kernel, w=w),
        out_shape=(jax.ShapeDtypeStruct((n, d), F32),
                   jax.ShapeDtypeStruct((CONV_WIDTH - 1, n, w), F32),
                   jax.ShapeDtypeStruct((n, w), F32),
                   jax.ShapeDtypeStruct((POOL_BUF, n, w), F32)),
        compiler_params=pltpu.CompilerParams(vmem_limit_bytes=VMEM_LIMIT),
        name="cd_step",
    )(x, g, win_bf, cw, cb, wg_bf, ba, bx, lam, pw_bf, psc, wo_bf, sc, lru_h0, sp)
    return o, conv_n.transpose(1, 0, 2), lru_n, pool_n.transpose(1, 0, 2)


def _block_diag_tiles(w, tile):
    nb, c, _ = w.shape
    per = tile // c
    w4 = w.reshape(nb // per, per, c, c)
    placed = jnp.einsum('tpij,pq->tpiqj', w4, jnp.eye(per, dtype=w.dtype))
    return placed.reshape(nb // per, tile, tile)


def _rope_tables(pos, rows):
    half = HEAD_DIM // 2
    inv = ROPE_THETA ** (-jnp.arange(half, dtype=F32) / half)
    ang = pos.astype(F32)[:, None] * inv[None, :]
    cos = jnp.cos(ang)
    sin = jnp.sin(ang)
    reps = LANES // HEAD_DIM
    cos_t = jnp.tile(jnp.concatenate([cos, cos], axis=1), (1, reps))
    sin_t = jnp.tile(jnp.concatenate([-sin, sin], axis=1), (1, reps))
    if cos_t.shape[0] != rows:
        cos_t = jnp.broadcast_to(cos_t, (rows, LANES))
        sin_t = jnp.broadcast_to(sin_t, (rows, LANES))
    return cos_t, sin_t


def _row_tile(n, target):
    t = min(n, target)
    while n % t:
        t //= 2
    return t


def kernel(x_prompt, x_sample, cache_k, cache_v, page_table, state_s5_re, state_s5_im, state_conv, state_lru, state_pool, norm_mix, norm_ffn, norm_final, w_in_ab, w_out_ab, s5_a_re, s5_a_im, s5_log_dt, s5_b_re, s5_b_im, s5_c_re, s5_c_im, s5_d, s5_w_glu, s5_b_glu, diff_lq1, diff_lk1, diff_lq2, diff_lk2, diff_subln, w_in_cd, w_out_cd, conv_w, conv_b, lru_wa, lru_ba, lru_wx, lru_bx, lru_lambda, pool_w, pool_scale, ffn_w_gate, ffn_w_up, ffn_w_down):
    bsz, seq, d = x_prompt.shape
    nseq, dec_seq, _ = x_sample.shape
    assert dec_seq == 1
    depth = norm_mix.shape[0]
    npages = page_table.shape[1]
    page_size = cache_k.shape[2]
    past_len = npages * page_size
    assert past_len >= POOL_BUF
    s5w = s5_d.shape[1]
    qkw = cache_k.shape[3] * cache_k.shape[4]
    vw_total = cache_v.shape[3] * cache_v.shape[4]
    heads = cache_v.shape[3]
    n_p = bsz * seq
    row2 = lambda a: a.reshape(1, -1)

    xp = x_prompt.reshape(n_p, d)
    xs = x_sample.reshape(nseq, d)
    tm_p = _row_tile(seq, 512)
    cos_p, sin_p = _rope_tables(jnp.arange(seq, dtype=jnp.int32), seq)
    cos_s, sin_s = _rope_tables(jnp.full((1,), past_len, jnp.int32), nseq)

    outs_p = {'k': [], 'v': [], 're': [], 'im': [], 'conv': [], 'lru': [], 'pool': []}
    outs_s = {'k': [], 'v': [], 're': [], 'im': [], 'conv': [], 'lru': [], 'pool': []}

    ffn_gate_bf = ffn_w_gate.astype(BF16)
    ffn_up_bf = ffn_w_up.astype(BF16)
    ffn_down_bf = ffn_w_down.astype(BF16)

    for l in range(depth):
        j = l // 2
        g_mix = row2(norm_mix[l])
        last = l == depth - 1
        ffn_w = (row2(norm_ffn[l]), ffn_gate_bf, ffn_up_bf, ffn_down_bf, row2(norm_final), l)
        if l % 2 == 0:
            lam_init = 0.8 - 0.6 * math.exp(-0.3 * l)
            w_in = w_in_ab[j].astype(BF16)
            w_out = w_out_ab[j].astype(BF16)
            w_glu = s5_w_glu[j].astype(BF16)
            b_glu = row2(s5_b_glu[j])
            lamp = jnp.stack([diff_lq1[j], diff_lk1[j], diff_lq2[j], diff_lk2[j]]).astype(F32)
            gain = row2(diff_subln[j])
            s5p = (s5_a_re[j], s5_a_im[j], s5_log_dt[j], s5_b_re[j], s5_b_im[j],
                   s5_c_re[j], s5_c_im[j], s5_d[j])
            u, q_bf, k, v, k_bf, v_bf = _inproj_ab(xp, g_mix, w_in, cos_p, sin_p, tm_p, s5w, qkw)
            t_attn = _row_tile(seq, 1024)
            attn, k_t = _attn_prompt(q_bf, k_bf, v_bf, k, lamp, gain, bsz, seq, t_attn, t_attn,
                                     lam_init)
            y, h_re, h_im = _s5_prompt(u, _s5_prompt_tables(s5p, seq), bsz, seq)
            xp = _post_ab(xp, y, attn, w_glu, b_glu, w_out, ffn_w, tm_p, last)
            outs_p['k'].append(k_t.reshape(bsz, qkw // HEAD_DIM, HEAD_DIM, seq).transpose(0, 3, 1, 2))
            outs_p['v'].append(v.reshape(bsz, seq, heads, vw_total // heads))
            outs_p['re'].append(h_re)
            outs_p['im'].append(h_im)
            u, q_bf, k, v, _, _ = _inproj_ab(xs, g_mix, w_in, cos_s, sin_s, nseq, s5w, qkw)
            ck = jnp.transpose(cache_k[j], (0, 2, 3, 1)).reshape(-1, qkw, page_size)
            cv = cache_v[j].reshape(-1, page_size * heads, vw_total // heads)
            attn = _attn_decode(q_bf.astype(F32), k, v.reshape(nseq, vw_total), ck, cv, page_table,
                                lamp, gain,
                                math.gcd(npages, 32), lam_init)
            y, h_re, h_im = _s5_step(u, state_s5_re[j], state_s5_im[j], s5p)
            xs = _post_ab(xs, y, attn, w_glu, b_glu, w_out, ffn_w, nseq, last)
            outs_s['k'].append(k.reshape(nseq, 1, qkw // HEAD_DIM, HEAD_DIM))
            outs_s['v'].append(v.reshape(nseq, 1, heads, vw_total // heads))
            outs_s['re'].append(h_re)
            outs_s['im'].append(h_im)
        else:
            w = conv_w.shape[2]
            wg = jnp.concatenate([_block_diag_tiles(lru_wa[j], MXU_DIM),
                                  _block_diag_tiles(lru_wx[j], MXU_DIM)], axis=2).astype(BF16)
            cdw = (w_in_cd[j].astype(BF16), conv_w[j], row2(conv_b[j]), wg, row2(lru_ba[j]),
                   row2(lru_bx[j]), row2(lru_lambda[j]),
                   _block_diag_tiles(pool_w[j], MXU_DIM).astype(BF16), row2(pool_scale[j]),
                   w_out_cd[j].astype(BF16))
            xp, c_new, h_new, p_new = _cd_prompt(xp, g_mix, cdw, ffn_w, bsz, seq, tm_p, last)
            outs_p['conv'].append(c_new)
            outs_p['lru'].append(h_new.reshape(bsz, w))
            outs_p['pool'].append(p_new)
            xs, c_new, h_new, p_new = _cd_step(xs, g_mix, cdw, state_conv[j], state_lru[j], state_pool[j])
            outs_s['conv'].append(c_new)
            outs_s['lru'].append(h_new)
            outs_s['pool'].append(p_new)
            xs = _ffn(xs, ffn_w, nseq, last)

    sp = {name: jnp.stack(vals) for name, vals in outs_p.items()}
    ss = {name: jnp.stack(vals) for name, vals in outs_s.items()}
    y_prompt = xp.reshape(bsz, seq, d)
    y_sample = xs.reshape(nseq, 1, d)
    return (y_prompt, y_sample, sp['k'], sp['v'], ss['k'], ss['v'], sp['re'], sp['im'], ss['re'], ss['im'],
            sp['conv'], ss['conv'], sp['lru'], ss['lru'], sp['pool'], ss['pool'])
```

```python
import functools
import math

import jax
import jax.numpy as jnp
from jax import lax
from jax.experimental import pallas as pl
from jax.experimental.pallas import tpu as pltpu

F32 = jnp.float32
BF16 = jnp.bfloat16

HEAD_DIM = 64
ROPE_THETA = 10000.0
NORM_EPS = 1e-6
SUBLN_EPS = 1e-5
S5_GROUP = 16
S5_CHUNK = 16
S5_SCAN_STRIDES = (1, 2, 4)
LRU_C = 8.0
CONV_WIDTH = 4
POOL_WINDOWS = (2, 4, 8, 16)
POOL_BUF = max(POOL_WINDOWS) - 1

LANES = 128
SUBLANES = 8
MXU_DIM = 256
VMEM_LIMIT = 56 * 1024 * 1024

NEG_BIG = -1e30
LOG2E = math.log2(math.e)


def _cparams(n_axes):
    return pltpu.CompilerParams(dimension_semantics=("arbitrary",) * n_axes,
                                vmem_limit_bytes=VMEM_LIMIT)


def _rms(x, g, eps):
    ms = jnp.mean(x * x, axis=-1, keepdims=True)
    return x * lax.rsqrt(ms + eps) * g


def _lane_tile(x, reps):
    return jnp.concatenate([x] * reps, axis=1)


def _rope(x, cos, sin_signed):
    width = x.shape[1]
    lane = lax.broadcasted_iota(jnp.int32, x.shape, 1)
    half = HEAD_DIM // 2
    rot = jnp.where((lane & half) == 0, pltpu.roll(x, width - half, 1), pltpu.roll(x, half, 1))
    return x * cos + rot * sin_signed


def _inproj_ab_kernel(x_ref, g_ref, w_ref, cos_ref, sin_ref,
                      u_ref, q_ref, k_ref, v_ref, kb_ref, vb_ref, *, s5w, qkw):
    h = _rms(x_ref[...], g_ref[...], NORM_EPS).astype(BF16)
    proj = jnp.dot(h, w_ref[...], preferred_element_type=F32)
    reps = qkw // LANES
    cos = _lane_tile(cos_ref[...], reps)
    sin = _lane_tile(sin_ref[...], reps)
    u_ref[...] = proj[:, :s5w]
    q = _rope(proj[:, s5w:s5w + qkw], cos, sin)
    k = _rope(proj[:, s5w + qkw:s5w + 2 * qkw], cos, sin)
    v = proj[:, s5w + 2 * qkw:]
    q_ref[...] = (q * (HEAD_DIM ** -0.5 * LOG2E)).astype(BF16)
    k_ref[...] = k
    nvh = v.shape[1] // LANES
    for hh in range(nvh):
        v_ref[pl.ds(hh, v.shape[0], stride=nvh), :] = v[:, hh * LANES:(hh + 1) * LANES]
    kb_ref[...] = k.astype(BF16)
    vb_ref[...] = v.astype(BF16)


def _inproj_ab(x, g, w_bf, cos_tab, sin_tab, tm, s5w, qkw):
    n, d = x.shape
    wn = w_bf.shape[1]
    vw = wn - s5w - 2 * qkw
    nvh = vw // LANES
    period = cos_tab.shape[0] // tm
    row = lambda i: (i, 0)
    const = lambda i: (0, 0)
    tab = lambda i: (i % period, 0)
    return pl.pallas_call(
        functools.partial(_inproj_ab_kernel, s5w=s5w, qkw=qkw),
        out_shape=(jax.ShapeDtypeStruct((n, s5w), F32), jax.ShapeDtypeStruct((n, qkw), BF16),
                   jax.ShapeDtypeStruct((n, qkw), F32), jax.ShapeDtypeStruct((n * nvh, LANES), F32),
                   jax.ShapeDtypeStruct((n, qkw), BF16), jax.ShapeDtypeStruct((n, vw), BF16)),
        grid=(n // tm,),
        in_specs=[pl.BlockSpec((tm, d), row), pl.BlockSpec((1, d), const),
                  pl.BlockSpec((d, wn), const), pl.BlockSpec((tm, LANES), tab),
                  pl.BlockSpec((tm, LANES), tab)],
        out_specs=(pl.BlockSpec((tm, s5w), row), pl.BlockSpec((tm, qkw), row),
                   pl.BlockSpec((tm, qkw), row), pl.BlockSpec((tm * nvh, LANES), row),
                   pl.BlockSpec((tm, qkw), row), pl.BlockSpec((tm, vw), row)),
        compiler_params=_cparams(1),
        name="inproj_ab",
    )(x, g, w_bf, cos_tab, sin_tab)


def _diff_lambda(lamp, lam_init):
    s1 = jnp.sum(lamp[0:1] * lamp[1:2], axis=1, keepdims=True)
    s2 = jnp.sum(lamp[2:3] * lamp[3:4], axis=1, keepdims=True)
    return jnp.exp(s1) - jnp.exp(s2) + lam_init


def _subln(o, gain, lam_init):
    ms = jnp.mean(o * o, axis=-1, keepdims=True)
    return o * lax.rsqrt(ms + SUBLN_EPS) * gain * (1.0 - lam_init)


ONES_ROWS = 16


def _attn_prompt_kernel(lamp_ref, q_ref, k_ref, v_ref, gain_ref, kf_ref, o_ref, kt_ref,
                        vt_ref, m_ref, acc_ref, *, tq, tk, lam_init):
    qi = pl.program_id(2)
    seq, vw = v_ref.shape
    per_q = tq // tk

    @pl.when(qi == 0)
    def _():
        for c in range(seq // tk):
            blk = v_ref[c * tk:(c + 1) * tk, :].astype(F32)
            vt_ref[c, :vw, :] = blk.T.astype(BF16)
            vt_ref[c, vw:, :] = jnp.ones((ONES_ROWS, tk), BF16)
            kt_ref[:, c * tk:(c + 1) * tk] = kf_ref[c * tk:(c + 1) * tk, :].T

    q = q_ref[...]
    lane = lax.broadcasted_iota(jnp.int32, q.shape, 1)
    zero = jnp.zeros_like(q)
    qs = (jnp.where(lane < HEAD_DIM, q, zero), jnp.where(lane >= HEAD_DIM, q, zero))
    m_ref[...] = jnp.full(m_ref.shape, NEG_BIG, F32)
    acc_ref[...] = jnp.zeros(acc_ref.shape, F32)

    def chunk(c, diag):
        off = pl.multiple_of(c * tk, tk)
        kc = k_ref[pl.ds(off, tk), :]
        vc = vt_ref[c]

        def scores(j):
            st = lax.dot_general(kc, qs[j], (((1,), (1,)), ((), ())), preferred_element_type=F32)
            if diag is not None:
                key = lax.broadcasted_iota(jnp.int32, st.shape, 0) + diag * tk
                qry = lax.broadcasted_iota(jnp.int32, st.shape, 1)
                st = jnp.where(key <= qry, st, -jnp.inf)
            return st

        def softmax(j, st):
            m_prev = m_ref[j]
            m_new = jnp.maximum(m_prev, jnp.max(st, axis=0, keepdims=True))
            m_ref[j] = m_new
            return jnp.exp2(m_prev - m_new), jnp.exp2(st - m_new).astype(BF16)

        def accumulate(j, alpha, p):
            pv = jnp.dot(vc, p, preferred_element_type=F32)
            acc_ref[j] = alpha * acc_ref[j] + pv

        alpha0, p0 = softmax(0, scores(0))
        st1 = scores(1)
        accumulate(0, alpha0, p0)
        alpha1, p1 = softmax(1, st1)
        accumulate(1, alpha1, p1)

    def body(c, carry):
        chunk(c, None)
        return carry

    def diagonal_chunk(c):
        h = tk // 2
        off = pl.multiple_of(c * tk, tk)
        k_lo = k_ref[pl.ds(off, h), :]
        k_hi = k_ref[pl.ds(off + h, h), :]
        vc = vt_ref[c]

        def masked_scores(kpart, qpart):
            st = lax.dot_general(kpart, qpart, (((1,), (1,)), ((), ())), preferred_element_type=F32)
            key = lax.broadcasted_iota(jnp.int32, st.shape, 0)
            qry = lax.broadcasted_iota(jnp.int32, st.shape, 1)
            return jnp.where(key <= qry, st, -jnp.inf)

        for j in range(2):
            st_lo = masked_scores(k_lo, qs[j])
            st_hi = masked_scores(k_hi, qs[j][h:, :])
            m_prev = m_ref[j]
            m_lo = jnp.maximum(m_prev, jnp.max(st_lo, axis=0, keepdims=True))
            m_new = jnp.concatenate(
                [m_lo[:, :h], jnp.maximum(m_lo[:, h:], jnp.max(st_hi, axis=0, keepdims=True))], axis=1)
            m_ref[j] = m_new
            alpha = jnp.exp2(m_prev - m_new)
            p_lo = jnp.exp2(st_lo - m_new).astype(BF16)
            p_hi = jnp.exp2(st_hi - m_new[:, h:]).astype(BF16)
            pv = jnp.dot(vc[:, :h], p_lo, preferred_element_type=F32)
            pv_hi = jnp.dot(vc[:, h:], p_hi, preferred_element_type=F32)
            pv = jnp.concatenate([pv[:, :h], pv[:, h:] + pv_hi], axis=1)
            acc_ref[j] = alpha * acc_ref[j] + pv

    lax.fori_loop(0, qi * per_q, body, 0)
    if per_q == 1:
        diagonal_chunk(qi)
    else:
        for d in range(per_q):
            chunk(qi * per_q + d, d)

    lam = _diff_lambda(lamp_ref[...], lam_init)
    a0 = acc_ref[0]
    a1 = acc_ref[1]
    ot = a0[:vw] / a0[vw:vw + 1] - lam * (a1[:vw] / a1[vw:vw + 1])
    o_ref[...] = _subln(ot.T, gain_ref[...], lam_init).astype(o_ref.dtype)


def _attn_prompt(q_bf, k_bf, v_bf, k_f32, lamp, gain, bsz, seq, tq, tk, lam_init):
    n, qkw = q_bf.shape
    vw_total = v_bf.shape[1]
    heads = qkw // (2 * HEAD_DIM)
    vw = vw_total // heads
    nq = seq // tq
    pair = lambda b, h, i: (b, h)
    return pl.pallas_call(
        functools.partial(_attn_prompt_kernel, tq=tq, tk=tk, lam_init=lam_init),
        out_shape=(jax.ShapeDtypeStruct((n, vw_total), BF16),
                   jax.ShapeDtypeStruct((bsz * qkw, seq), F32)),
        grid=(bsz, heads, nq),
        in_specs=[pl.BlockSpec((4, HEAD_DIM), lambda b, h, i: (0, 0)),
                  pl.BlockSpec((tq, 2 * HEAD_DIM), lambda b, h, i: (b * nq + i, h)),
                  pl.BlockSpec((seq, 2 * HEAD_DIM), pair),
                  pl.BlockSpec((seq, vw), pair),
                  pl.BlockSpec((1, vw), lambda b, h, i: (0, 0)),
                  pl.BlockSpec((seq, 2 * HEAD_DIM), pair)],
        out_specs=(pl.BlockSpec((tq, vw), lambda b, h, i: (b * nq + i, h)),
                   pl.BlockSpec((2 * HEAD_DIM, seq), lambda b, h, i: (b * heads + h, 0))),
        scratch_shapes=[pltpu.VMEM((seq // tk, vw + ONES_ROWS, tk), BF16),
                        pltpu.VMEM((2, 1, tq), F32),
                        pltpu.VMEM((2, vw + ONES_ROWS, tq), F32)],
        compiler_params=_cparams(3),
        name="attn_prompt",
    )(lamp, q_bf, k_bf, v_bf, gain, k_f32)


def _attn_decode_kernel(pt_ref, lamp_ref, q_ref, kself_ref, vself_ref, gain_ref, *rest,
                        pages, lam_init, vw):
    k_refs = rest[:pages]
    v_refs = rest[pages:2 * pages]
    o_ref = rest[2 * pages]
    m_ref, l_ref, acc_ref = rest[2 * pages + 1:]
    j = pl.program_id(1)
    nh2 = m_ref.shape[0]
    width = q_ref.shape[1]

    @pl.when(j == 0)
    def _():
        m_ref[...] = jnp.full(m_ref.shape, NEG_BIG, F32)
        l_ref[...] = jnp.zeros(l_ref.shape, F32)
        acc_ref[...] = jnp.zeros(acc_ref.shape, F32)

    rowh = lax.broadcasted_iota(jnp.int32, (nh2, width), 0)
    laneh = lax.broadcasted_iota(jnp.int32, (nh2, width), 1) // HEAD_DIM
    q8 = jnp.where(rowh == laneh, jnp.broadcast_to(q_ref[...], (nh2, width)), 0.0)
    q8b = q8.astype(BF16)

    s_parts = [jnp.dot(q8b, k_refs[i][...].astype(BF16), preferred_element_type=F32)
               for i in range(pages)]
    s = jnp.concatenate(s_parts, axis=1)
    m_prev = m_ref[...]
    m_new = jnp.maximum(m_prev, jnp.max(s, axis=1, keepdims=True))
    alpha = jnp.exp2(m_prev - m_new)
    p = jnp.exp2(s - _lane_tile(m_new, s.shape[1] // LANES))
    l_ref[...] = alpha * l_ref[...] + jnp.sum(p, axis=1, keepdims=True)
    ps = k_refs[0].shape[1]
    nvh = width // vw
    pb = p.astype(BF16)
    pv = None
    for i in range(pages):
        cols = [jnp.dot(pb[:, i * ps:(i + 1) * ps],
                        v_refs[i][pl.ds(h, ps, stride=nvh), :].astype(BF16),
                        preferred_element_type=F32) for h in range(nvh)]
        part = jnp.concatenate(cols, axis=1)
        pv = part if pv is None else pv + part
    acc_ref[...] = _lane_tile(alpha, width // LANES) * acc_ref[...] + pv
    m_ref[...] = m_new

    @pl.when(j == pl.num_programs(1) - 1)
    def _():
        s_self = jnp.sum(q8 * kself_ref[...], axis=1, keepdims=True)
        m_old = m_ref[...]
        m_fin = jnp.maximum(m_old, s_self)
        a_fin = jnp.exp2(m_old - m_fin)
        p_self = jnp.exp2(s_self - m_fin)
        l_fin = a_fin * l_ref[...] + p_self
        acc = (_lane_tile(a_fin, width // LANES) * acc_ref[...]
               + _lane_tile(p_self, width // LANES) * vself_ref[...])
        o8 = acc / _lane_tile(l_fin, width // LANES)
        lam = _diff_lambda(lamp_ref[...], lam_init)
        vhead = lax.broadcasted_iota(jnp.int32, (nh2, width), 1) // vw
        sign = jnp.where((rowh & 1) == 0, 1.0, -lam)
        coef = jnp.where(vhead == rowh // 2, sign, 0.0)
        o = jnp.sum(o8 * coef, axis=0, keepdims=True)
        gain = gain_ref[...]
        outs = [_subln(o[:, h * vw:(h + 1) * vw], gain, lam_init) for h in range(width // vw)]
        o_ref[...] = jnp.concatenate(outs, axis=1)


def _attn_decode(q, k_self, v_self, cache_kt, cache_v2, page_table, lamp, gain, pages, lam_init):
    nseq, width = q.shape
    npages = page_table.shape[1]
    ps = cache_kt.shape[2]
    vw = gain.shape[1]
    nh2 = width // HEAD_DIM
    pt = page_table.reshape(-1)
    q3, k3, v3 = (a.reshape(nseq, 1, width) for a in (q, k_self, v_self))
    seq_spec = pl.BlockSpec((None, 1, width), lambda b, j, pt: (b, 0, 0))

    def page_spec(i, shape):
        return pl.BlockSpec((None,) + shape,
                            lambda b, j, pt: (pt[b * npages + j * pages + i], 0, 0))

    grid_spec = pltpu.PrefetchScalarGridSpec(
        num_scalar_prefetch=1,
        grid=(nseq, npages // pages),
        in_specs=[pl.BlockSpec((4, HEAD_DIM), lambda b, j, pt: (0, 0)), seq_spec, seq_spec, seq_spec,
                  pl.BlockSpec((1, vw), lambda b, j, pt: (0, 0))]
                 + [page_spec(i, cache_kt.shape[1:]) for i in range(pages)]
                 + [page_spec(i, cache_v2.shape[1:]) for i in range(pages)],
        out_specs=seq_spec,
        scratch_shapes=[pltpu.VMEM((nh2, LANES), F32), pltpu.VMEM((nh2, LANES), F32),
                        pltpu.VMEM((nh2, width), F32)],
    )
    out = pl.pallas_call(
        functools.partial(_attn_decode_kernel, pages=pages, lam_init=lam_init, vw=vw),
        out_shape=jax.ShapeDtypeStruct((nseq, 1, width), F32),
        grid_spec=grid_spec,
        compiler_params=_cparams(2),
        name="attn_decode",
    )(pt, lamp, q3, k3, v3, gain, *([cache_kt] * pages), *([cache_v2] * pages))
    return out.reshape(nseq, width)


def _cmul(a, b):
    return a[0] * b[0] - a[1] * b[1], a[0] * b[1] + a[1] * b[0]


def _s5_discretize(a_re, a_im, log_dt, b_re, b_im):
    a_re, a_im = a_re.astype(F32), a_im.astype(F32)
    step = jnp.exp(log_dt.astype(F32))[:, None]
    llb = (a_re * step, a_im * step)
    lb = _s5_power(llb, 1.0)
    num = (lb[0] - 1.0, lb[1])
    den = a_re * a_re + a_im * a_im
    quo = ((num[0] * a_re + num[1] * a_im) / den, (num[1] * a_re - num[0] * a_im) / den)
    b_bar = _cmul((quo[0][:, :, None], quo[1][:, :, None]), (b_re.astype(F32), b_im.astype(F32)))
    return llb, b_bar


def _s5_power(llb, k):
    mag = jnp.exp(llb[0] * k)
    return mag * jnp.cos(llb[1] * k), mag * jnp.sin(llb[1] * k)


def _s5_prompt_tables(params, seq):
    a_re, a_im, log_dt, b_re, b_im, c_re, c_im, d_skip = params
    g, p = a_re.shape
    ch, t = S5_GROUP, S5_CHUNK
    llb, b_bar = _s5_discretize(a_re, a_im, log_dt, b_re, b_im)
    c = (c_re.astype(F32), c_im.astype(F32))
    lags = jnp.arange(t + 1, dtype=F32)[:, None, None]
    powk = _s5_power((llb[0][None], llb[1][None]), lags)
    cp = _cmul((c[0][None], c[1][None]),
               (powk[0][:, :, None, :], powk[1][:, :, None, :]))
    klag = (jnp.einsum('kgcp,gpd->gdkc', cp[0][:t], b_bar[0])
            - jnp.einsum('kgcp,gpd->gdkc', cp[1][:t], b_bar[1]))
    skip = d_skip.astype(F32).reshape(g, ch)[:, None, :] * jnp.eye(ch, dtype=F32)[None]
    klag = klag.at[:, :, 0, :].add(skip)
    s_i = jnp.arange(t)[:, None, None]
    src = jnp.arange(t * ch)[None, :, None]
    dst = jnp.arange(t * ch)[None, None, :]
    place = ((dst // ch - s_i == src // ch) & (src % ch == dst % ch)).astype(F32)
    m_mat = jnp.einsum('xq,sqr->sxr', klag.reshape(g * ch, t * ch), place)
    pw_rev = (powk[0][:t][::-1].transpose(1, 0, 2)[:, :, None, :],
              powk[1][:t][::-1].transpose(1, 0, 2)[:, :, None, :])
    bb_t = (b_bar[0].transpose(0, 2, 1)[:, None], b_bar[1].transpose(0, 2, 1)[:, None])
    e_c = _cmul(pw_rev, bb_t)
    e_re = e_c[0].reshape(g, t * ch, p)
    e_im = e_c[1].reshape(g, t * ch, p)
    e_mat = jnp.concatenate([e_re, e_im, e_im, e_re], axis=-1)
    f_re = cp[0][1:t + 1].transpose(1, 0, 2, 3).reshape(g, t * ch, p)
    f_im = cp[1][1:t + 1].transpose(1, 0, 2, 3).reshape(g, t * ch, p)
    f_mat = jnp.concatenate([f_re, -f_im], axis=-1).transpose(0, 2, 1)
    ks = (jnp.arange(1, SUBLANES + 1, dtype=F32) * t)[None, :, None]
    pw = _s5_power((llb[0][:, None, :], llb[1][:, None, :]), ks)
    a1 = jnp.concatenate([pw[0], pw[0]], axis=-1)
    a2 = jnp.concatenate([-pw[1], pw[1]], axis=-1)
    rows = jnp.arange(SUBLANES)[None, :, None]
    parts = []
    for d in S5_SCAN_STRIDES:
        parts.append(jnp.where(rows >= d, a1[:, d - 1:d, :], 0.0))
        parts.append(jnp.where(rows >= d, a2[:, d - 1:d, :], 0.0))
    tab = jnp.concatenate(parts + [a1, a2], axis=1)
    return m_mat.astype(BF16), e_mat.astype(BF16), f_mat.astype(BF16), tab


def _s5_lane_permutation():
    n = SUBLANES * SUBLANES * S5_GROUP
    idx = jnp.arange(n).reshape(SUBLANES, SUBLANES, S5_GROUP).transpose(1, 0, 2).reshape(n)
    return jax.nn.one_hot(idx, n, dtype=BF16)


def _s5_prompt_kernel(u_ref, perm_ref, e_ref, m_ref, f_ref, tab_ref, y_ref, hl_ref,
                      us_ref, ee_ref, h_ref, ycat_ref):
    t = S5_CHUNK
    ng = SUBLANES
    nsub = u_ref.shape[0] // t
    sw = tab_ref.shape[2]
    half_t = t // 2
    perm = perm_ref[...]
    for half in range(2):
        xs = [u_ref[pl.ds(half * half_t + s, nsub, stride=t), :].astype(BF16) for s in range(half_t)]
        uh = jnp.dot(jnp.concatenate(xs, axis=1), perm, preferred_element_type=F32).astype(BF16)
        for gl in range(ng):
            us_ref[gl, :, half * LANES:(half + 1) * LANES] = uh[:, gl * LANES:(gl + 1) * LANES]
    for gl in range(ng):
        ee_ref[gl] = jnp.dot(us_ref[gl], e_ref[gl], preferred_element_type=F32)
        h_ref[gl, :SUBLANES, :] = jnp.zeros((SUBLANES, sw), F32)

    def step(i, carry):
        row = pl.multiple_of(i * SUBLANES, SUBLANES)
        out = []
        for gl in range(ng):
            h = ee_ref[gl, pl.ds(row, SUBLANES), :sw]
            hs = ee_ref[gl, pl.ds(row, SUBLANES), sw:]
            for di, d in enumerate(S5_SCAN_STRIDES):
                t1 = tab_ref[gl, 2 * di * SUBLANES:(2 * di + 1) * SUBLANES, :]
                t2 = tab_ref[gl, (2 * di + 1) * SUBLANES:(2 * di + 2) * SUBLANES, :]
                sh = pltpu.roll(h, d, 0)
                shs = pltpu.roll(hs, d, 0)
                h, hs = h + t1 * sh + t2 * shs, hs + t1 * shs - t2 * sh
            nd = 2 * len(S5_SCAN_STRIDES)
            q1 = tab_ref[gl, nd * SUBLANES:(nd + 1) * SUBLANES, :]
            q2 = tab_ref[gl, (nd + 1) * SUBLANES:(nd + 2) * SUBLANES, :]
            cb, cbs = carry[2 * gl], carry[2 * gl + 1]
            hf = h + q1 * cb + q2 * cbs
            hfs = hs + q1 * cbs - q2 * cb
            h_ref[gl, pl.ds(SUBLANES + row, SUBLANES), :] = hf
            out.append(jnp.broadcast_to(hf[SUBLANES - 1:SUBLANES, :], hf.shape))
            out.append(jnp.broadcast_to(hfs[SUBLANES - 1:SUBLANES, :], hfs.shape))
        return tuple(out)

    zero = jnp.zeros((SUBLANES, sw), F32)
    carry = lax.fori_loop(0, nsub // SUBLANES, step, (zero,) * (2 * ng))
    for gl in range(ng):
        hl_ref[gl:gl + 1, :] = carry[2 * gl][0:1, :]
        hin = h_ref[gl, pl.ds(SUBLANES - 1, nsub), :].astype(BF16)
        m_g = m_ref[:, gl * S5_GROUP:(gl + 1) * S5_GROUP, :].reshape(t * S5_GROUP, t * S5_GROUP)
        y = (jnp.dot(us_ref[gl], m_g, preferred_element_type=F32)
             + jnp.dot(hin, f_ref[gl], preferred_element_type=F32))
        for half in range(2):
            ycat_ref[half, :, gl * LANES:(gl + 1) * LANES] = y[:, half * LANES:(half + 1) * LANES]
    for half in range(2):
        yc = ycat_ref[half]
        hi = yc.astype(BF16)
        lo = (yc - hi.astype(F32)).astype(BF16)
        z = (jnp.dot(hi, perm, preferred_element_type=F32)
             + jnp.dot(lo, perm, preferred_element_type=F32))
        for s in range(half_t):
            y_ref[pl.ds(half * half_t + s, nsub, stride=t), :] = z[:, s * LANES:(s + 1) * LANES]


def _s5_prompt(u, tables, bsz, seq):
    m_mat, e_mat, f_mat, tab = tables
    g = e_mat.shape[0]
    p = f_mat.shape[1] // 2
    t, ch = S5_CHUNK, S5_GROUP
    ng = LANES // ch
    assert ng == SUBLANES and t * ch == 2 * LANES
    n, w = u.shape
    nsub = seq // t
    perm = _s5_lane_permutation()
    npm = perm.shape[0]
    ntab = tab.shape[1]
    slab = lambda s, b: (s, 0, 0)
    y, hl = pl.pallas_call(
        _s5_prompt_kernel,
        out_shape=(jax.ShapeDtypeStruct((n, w), F32), jax.ShapeDtypeStruct((bsz, g, 2 * p), F32)),
        grid=(w // LANES, bsz),
        in_specs=[pl.BlockSpec((seq, LANES), lambda s, b: (b, s)),
                  pl.BlockSpec((npm, npm), lambda s, b: (0, 0)),
                  pl.BlockSpec((ng, t * ch, 4 * p), slab),
                  pl.BlockSpec((t, ng * ch, t * ch), lambda s, b: (0, s, 0)),
                  pl.BlockSpec((ng, 2 * p, t * ch), slab),
                  pl.BlockSpec((ng, ntab, 2 * p), slab)],
        out_specs=(pl.BlockSpec((seq, LANES), lambda s, b: (b, s)),
                   pl.BlockSpec((None, ng, 2 * p), lambda s, b: (b, s, 0))),
        scratch_shapes=[pltpu.VMEM((ng, nsub, t * ch), BF16),
                        pltpu.VMEM((ng, nsub, 4 * p), F32),
                        pltpu.VMEM((ng, SUBLANES + nsub, 2 * p), F32),
                        pltpu.VMEM((2, nsub, ng * LANES), F32)],
        compiler_params=_cparams(2),
        name="s5_prompt",
    )(u, perm, e_mat, m_mat, f_mat, tab)
    return y, hl[..., :p], hl[..., p:]


def _s5_step_kernel(u_ref, h_ref, hs_ref, bt_ref, ct_ref, a1_ref, a2_ref, d_ref, y_ref, hn_ref):
    hp = lax.Precision.HIGHEST
    u = u_ref[...]
    bu = jnp.einsum('gbc,gcp->gbp', u, bt_ref[...], precision=hp, preferred_element_type=F32)
    hn = a1_ref[...] * h_ref[...] + a2_ref[...] * hs_ref[...] + bu
    hn_ref[...] = hn
    y = jnp.einsum('gbp,gpc->gbc', hn, ct_ref[...], precision=hp, preferred_element_type=F32)
    y_ref[...] = y + d_ref[...] * u


def _s5_step(u, h0_re, h0_im, params):
    a_re, a_im, log_dt, b_re, b_im, c_re, c_im, d_skip = params
    g, p = a_re.shape
    ch = S5_GROUP
    bsz = u.shape[0]
    llb, b_bar = _s5_discretize(a_re, a_im, log_dt, b_re, b_im)
    lam_bar = _s5_power(llb, 1.0)
    bt = jnp.concatenate([b_bar[0].transpose(0, 2, 1), b_bar[1].transpose(0, 2, 1)], axis=-1)
    ct = jnp.concatenate([c_re.astype(F32), -c_im.astype(F32)], axis=-1).transpose(0, 2, 1)
    a1 = jnp.concatenate([lam_bar[0]] * 2, axis=-1)[:, None, :]
    a2 = jnp.concatenate([-lam_bar[1], lam_bar[1]], axis=-1)[:, None, :]
    h = jnp.concatenate([h0_re, h0_im], axis=-1).transpose(1, 0, 2).astype(F32)
    hs = jnp.concatenate([h0_im, h0_re], axis=-1).transpose(1, 0, 2).astype(F32)
    ug = u.reshape(bsz, g, ch).transpose(1, 0, 2)
    dg = d_skip.astype(F32).reshape(g, 1, ch)
    y, hn = pl.pallas_call(
        _s5_step_kernel,
        out_shape=(jax.ShapeDtypeStruct((g, bsz, ch), F32), jax.ShapeDtypeStruct((g, bsz, 2 * p), F32)),
        compiler_params=pltpu.CompilerParams(vmem_limit_bytes=VMEM_LIMIT),
        name="s5_step",
    )(ug, h, hs, bt, ct, a1, a2, dg)
    hn = hn.transpose(1, 0, 2)
    return y.transpose(1, 0, 2).reshape(bsz, g * ch), hn[..., :p], hn[..., p:]


def _resident(a):
    zeros = (0,) * a.ndim
    return pl.BlockSpec(a.shape, lambda *_: zeros, pipeline_mode=pl.Buffered(1))


def _ffn_apply(x, ffn_refs, final_norm):
    g_ref, wg_ref, wu_ref, wd_ref, gf_ref = ffn_refs
    hn = _rms(x, g_ref[...], NORM_EPS).astype(BF16)
    hidden = wg_ref.shape[1]
    th = MXU_DIM if hidden % MXU_DIM == 0 else hidden
    acc = None
    for c in range(hidden // th):
        cols = slice(c * th, (c + 1) * th)
        gate = jnp.dot(hn, wg_ref[:, cols], preferred_element_type=F32)
        up = jnp.dot(hn, wu_ref[:, cols], preferred_element_type=F32)
        act = (jax.nn.silu(gate) * up).astype(BF16)
        part = jnp.dot(act, wd_ref[cols, :], preferred_element_type=F32)
        acc = part if acc is None else acc + part
    out = x + acc
    if final_norm:
        out = _rms(out, gf_ref[...], NORM_EPS)
    return out


def _ffn_specs(ffn_w):
    g, wg, wu, wd, gf, layer = ffn_w

    def one_layer(a):
        return pl.BlockSpec((None,) + a.shape[1:], lambda *_: (layer, 0, 0),
                            pipeline_mode=pl.Buffered(1))

    return [_resident(g), one_layer(wg), one_layer(wu), one_layer(wd), _resident(gf)]


def _ffn_kernel(x_ref, *rest, final_norm):
    o_ref = rest[-1]
    o_ref[...] = _ffn_apply(x_ref[...], rest[:-1], final_norm)


def _ffn(x, ffn_w, tm, final_norm):
    n, d = x.shape
    return pl.pallas_call(
        functools.partial(_ffn_kernel, final_norm=final_norm),
        out_shape=jax.ShapeDtypeStruct((n, d), F32),
        grid=(n // tm,),
        in_specs=[pl.BlockSpec((tm, d), lambda i: (i, 0))] + _ffn_specs(ffn_w),
        out_specs=pl.BlockSpec((tm, d), lambda i: (i, 0)),
        compiler_params=_cparams(1),
        name="ffn",
    )(x, *ffn_w[:5])


def _post_ab_kernel(x_ref, y_ref, a_ref, wglu_ref, bglu_ref, wo_ref, *rest, s5w, final_norm):
    o_ref = rest[-1]
    z = jax.nn.gelu(y_ref[...])
    gl = jnp.dot(z.astype(BF16), wglu_ref[...], preferred_element_type=F32) + bglu_ref[...]
    s5o = (z * jax.nn.sigmoid(gl)).astype(BF16)
    mix = (jnp.dot(s5o, wo_ref[:s5w, :], preferred_element_type=F32)
           + jnp.dot(a_ref[...].astype(BF16), wo_ref[s5w:, :], preferred_element_type=F32))
    o_ref[...] = _ffn_apply(x_ref[...] + mix, rest[:-1], final_norm)


def _post_ab(x, y, attn, wglu_bf, bglu, wo_bf, ffn_w, tm, final_norm):
    n, d = x.shape
    s5w = y.shape[1]
    aw = attn.shape[1]
    row = lambda i: (i, 0)
    return pl.pallas_call(
        functools.partial(_post_ab_kernel, s5w=s5w, final_norm=final_norm),
        out_shape=jax.ShapeDtypeStruct((n, d), F32),
        grid=(n // tm,),
        in_specs=[pl.BlockSpec((tm, d), row), pl.BlockSpec((tm, s5w), row), pl.BlockSpec((tm, aw), row),
                  _resident(wglu_bf), _resident(bglu), _resident(wo_bf)] + _ffn_specs(ffn_w),
        out_specs=pl.BlockSpec((tm, d), row),
        compiler_params=_cparams(1),
        name="post_ab_ffn",
    )(x, y, attn, wglu_bf, bglu, wo_bf, *ffn_w[:5])


def _lru_gates(xc, wg_ref, ba, bx, lam_param):
    xcb = xc.astype(BF16)
    halves = [jnp.dot(xcb[:, h * MXU_DIM:(h + 1) * MXU_DIM], wg_ref[h], preferred_element_type=F32)
              for h in range(wg_ref.shape[0])]
    r_pre = jnp.concatenate([hv[:, :MXU_DIM] for hv in halves], axis=1)
    i_pre = jnp.concatenate([hv[:, MXU_DIM:] for hv in halves], axis=1)
    r = jax.nn.sigmoid(r_pre + ba)
    i = jax.nn.sigmoid(i_pre + bx)
    log_a = -LRU_C * r * jax.nn.softplus(-lam_param)
    a = jnp.exp(log_a)
    b = jnp.sqrt(-jnp.tanh(log_a) * (a * a + 1.0)) * (i * xc)
    return a, b


def _pool_project(pooled, pw_ref, scale):
    pb = pooled.astype(BF16)
    halves = [jnp.dot(pb[:, h * MXU_DIM:(h + 1) * MXU_DIM], pw_ref[h], preferred_element_type=F32)
              for h in range(pw_ref.shape[0])]
    return jnp.concatenate(halves, axis=1) * scale


def _cd_prompt_kernel(x_ref, g_ref, win_ref, cw_ref, cb_ref, wg_ref, ba_ref, bx_ref, lam_ref,
                      pw_ref, psc_ref, wo_ref, fg_ref, fwg_ref, fwu_ref, fwd_ref, fgf_ref,
                      o_ref, conv_ref, lru_ref, pool_ref,
                      xl_ref, xp_ref, s2_ref, s4_ref, a_ref, b_ref, hs_ref, carry_ref,
                      *, tm, w, final_norm):
    ti = pl.program_id(1)
    hist = POOL_BUF + 1
    hp = hist + SUBLANES

    @pl.when(ti == 0)
    def _():
        xl_ref[:hist, :] = jnp.zeros((hist, w), F32)
        xp_ref[:hp, :] = jnp.zeros((hp, w), F32)
        s2_ref[:SUBLANES, :] = jnp.zeros((SUBLANES, w), F32)
        s4_ref[:SUBLANES, :] = jnp.zeros((SUBLANES, w), F32)
        carry_ref[...] = jnp.zeros(carry_ref.shape, F32)

    x = x_ref[...]
    hn = _rms(x, g_ref[...], NORM_EPS).astype(BF16)
    proj = jnp.dot(hn, win_ref[...], preferred_element_type=F32)
    gate = proj[:, :w]
    xl_ref[hist:, :] = proj[:, w:2 * w]
    xp_ref[hp:, :] = proj[:, 2 * w:]

    xc = cb_ref[...] + xl_ref[pl.ds(hist - (CONV_WIDTH - 1), tm), :] * cw_ref[0:1, :]
    for j in range(1, CONV_WIDTH):
        xc = xc + xl_ref[pl.ds(hist - (CONV_WIDTH - 1) + j, tm), :] * cw_ref[j:j + 1, :]

    a, b = _lru_gates(xc, wg_ref, ba_ref[...], bx_ref[...], lam_ref[...])
    a_ref[...] = a
    b_ref[...] = b

    def group(gi, carry):
        row = pl.multiple_of(gi * SUBLANES, SUBLANES)
        ag = a_ref[pl.ds(row, SUBLANES), :]
        bg = b_ref[pl.ds(row, SUBLANES), :]
        ridx = lax.broadcasted_iota(jnp.int32, ag.shape, 0)
        for d in (1, 2, 4):
            a_sh = jnp.where(ridx >= d, pltpu.roll(ag, d, 0), 1.0)
            b_sh = jnp.where(ridx >= d, pltpu.roll(bg, d, 0), 0.0)
            bg = ag * b_sh + bg
            ag = ag * a_sh
        h = bg + ag * carry
        hs_ref[pl.ds(row, SUBLANES), :] = h
        return jnp.broadcast_to(h[SUBLANES - 1:SUBLANES, :], h.shape)

    carry = lax.fori_loop(0, tm // SUBLANES, group, jnp.broadcast_to(carry_ref[...], (SUBLANES, w)))
    carry_ref[...] = carry[0:1, :]
    lru_out = jax.nn.gelu(gate) * hs_ref[...]

    assert POOL_WINDOWS == (2, 4, 8, 16)
    pg = w // len(POOL_WINDOWS)
    span = tm + hist
    s2_ref[pl.ds(SUBLANES, span), :] = (xp_ref[pl.ds(SUBLANES, span), :]
                                        + xp_ref[pl.ds(SUBLANES - 1, span), :])
    s4_ref[pl.ds(SUBLANES, span), pg:] = (s2_ref[pl.ds(SUBLANES, span), pg:]
                                          + s2_ref[pl.ds(SUBLANES - 2, span), pg:])
    s2_ref[pl.ds(SUBLANES, span), 2 * pg:] = (s4_ref[pl.ds(SUBLANES, span), 2 * pg:]
                                              + s4_ref[pl.ds(SUBLANES - 4, span), 2 * pg:])
    sums = (s2_ref[pl.ds(hp, tm), :pg],
            s4_ref[pl.ds(hp, tm), pg:2 * pg],
            s2_ref[pl.ds(hp, tm), 2 * pg:3 * pg],
            s2_ref[pl.ds(hp, tm), 3 * pg:] + s2_ref[pl.ds(hp - SUBLANES, tm), 3 * pg:])
    pos = ti * tm + lax.broadcasted_iota(jnp.int32, (tm, pg), 0)
    pooled = []
    for gidx, win in enumerate(POOL_WINDOWS):
        cur = xp_ref[pl.ds(hp, tm), gidx * pg:(gidx + 1) * pg]
        cnt = jnp.minimum(pos + 1, win).astype(F32)
        pooled.append(sums[gidx] / cnt - cur)
    pool_out = _pool_project(jnp.concatenate(pooled, axis=1), pw_ref, psc_ref[...])

    mix = (jnp.dot(lru_out.astype(BF16), wo_ref[:w, :], preferred_element_type=F32)
           + jnp.dot(pool_out.astype(BF16), wo_ref[w:, :], preferred_element_type=F32))
    o_ref[...] = _ffn_apply(x + mix, (fg_ref, fwg_ref, fwu_ref, fwd_ref, fgf_ref), final_norm)

    conv_ref[...] = xl_ref[pl.ds(hist + tm - (CONV_WIDTH - 1), CONV_WIDTH - 1), :]
    lru_ref[...] = carry_ref[...]
    pool_ref[...] = xp_ref[pl.ds(hp + tm - POOL_BUF, POOL_BUF), :]
    xl_ref[:hist, :] = xl_ref[pl.ds(tm, hist), :]
    xp_ref[pl.ds(SUBLANES, hist), :] = xp_ref[pl.ds(SUBLANES + tm, hist), :]


def _cd_prompt(x, g, cdw, ffn_w, bsz, seq, tm, final_norm):
    n, d = x.shape
    w = cdw[1].shape[1]
    nt = seq // tm
    hist = POOL_BUF + 1
    row = lambda b, i: (b * nt + i, 0)
    st = lambda b, i: (b, 0, 0)
    return pl.pallas_call(
        functools.partial(_cd_prompt_kernel, tm=tm, w=w, final_norm=final_norm),
        out_shape=(jax.ShapeDtypeStruct((n, d), F32),
                   jax.ShapeDtypeStruct((bsz, CONV_WIDTH - 1, w), F32),
                   jax.ShapeDtypeStruct((bsz, 1, w), F32),
                   jax.ShapeDtypeStruct((bsz, POOL_BUF, w), F32)),
        grid=(bsz, nt),
        in_specs=[pl.BlockSpec((tm, d), row), _resident(g)] + [_resident(a) for a in cdw]
                 + _ffn_specs(ffn_w),
        out_specs=(pl.BlockSpec((tm, d), row),
                   pl.BlockSpec((None, CONV_WIDTH - 1, w), st),
                   pl.BlockSpec((None, 1, w), st),
                   pl.BlockSpec((None, POOL_BUF, w), st)),
        scratch_shapes=[pltpu.VMEM((hist + tm, w), F32)]
                       + [pltpu.VMEM((SUBLANES + hist + tm, w), F32)] * 3
                       + [pltpu.VMEM((tm, w), F32)] * 3 + [pltpu.VMEM((1, w), F32)],
        compiler_params=_cparams(2),
        name="cd_prompt_ffn",
    )(x, g, *cdw, *ffn_w[:5])


def _cd_step_kernel(x_ref, g_ref, win_ref, cw_ref, cb_ref, wg_ref, ba_ref, bx_ref, lam_ref,
                    pw_ref, psc_ref, wo_ref, sc_ref, sl_ref, sp_ref,
                    o_ref, conv_ref, lru_ref, pool_ref, *, w):
    x = x_ref[...]
    hn = _rms(x, g_ref[...], NORM_EPS).astype(BF16)
    proj = jnp.dot(hn, win_ref[...], preferred_element_type=F32)
    gate = proj[:, :w]
    xl = proj[:, w:2 * w]
    xp = proj[:, 2 * w:]
    xc = cb_ref[...] + xl * cw_ref[CONV_WIDTH - 1:CONV_WIDTH, :]
    for j in range(CONV_WIDTH - 1):
        xc = xc + sc_ref[j] * cw_ref[j:j + 1, :]
    a, b = _lru_gates(xc, wg_ref, ba_ref[...], bx_ref[...], lam_ref[...])
    h = a * sl_ref[...] + b
    lru_out = jax.nn.gelu(gate) * h

    pg = w // len(POOL_WINDOWS)
    pooled = []
    for gidx, win in enumerate(POOL_WINDOWS):
        cols = slice(gidx * pg, (gidx + 1) * pg)
        cur = xp[:, cols]
        tot = cur
        for kk in range(1, win):
            tot = tot + sp_ref[POOL_BUF - kk][:, cols]
        pooled.append(tot / float(win) - cur)
    pool_out = _pool_project(jnp.concatenate(pooled, axis=1), pw_ref, psc_ref[...])

    mix = (jnp.dot(lru_out.astype(BF16), wo_ref[:w, :], preferred_element_type=F32)
           + jnp.dot(pool_out.astype(BF16), wo_ref[w:, :], preferred_element_type=F32))
    o_ref[...] = x + mix
    for j in range(CONV_WIDTH - 2):
        conv_ref[j] = sc_ref[j + 1]
    conv_ref[CONV_WIDTH - 2] = xl
    lru_ref[...] = h
    for j in range(POOL_BUF - 1):
        pool_ref[j] = sp_ref[j + 1]
    pool_ref[POOL_BUF - 1] = xp


def _cd_step(x, g, cdw, conv_buf, lru_h0, pool_buf):
    n, d = x.shape
    win_bf, cw, cb, wg_bf, ba, bx, lam, pw_bf, psc, wo_bf = cdw
    w = cw.shape[1]
    sc = conv_buf.transpose(1, 0, 2)
    sp = pool_buf.transpose(1, 0, 2)
    o, conv_n, lru_n, pool_n = pl.pallas_call(
        functools.partial(_cd_step_kernel, w=w),
        out_shape=(jax.ShapeDtypeStruct((n, d), F32),
                   jax.ShapeDtypeStruct((CONV_WIDTH - 1, n, w), F32),
                   jax.ShapeDtypeStruct((n, w), F32),
                   jax.ShapeDtypeStruct((POOL_BUF, n, w), F32)),
        compiler_params=pltpu.CompilerParams(vmem_limit_bytes=VMEM_LIMIT),
        name="cd_step",
    )(x, g, win_bf, cw, cb, wg_bf, ba, bx, lam, pw_bf, psc, wo_bf, sc, lru_h0, sp)
    return o, conv_n.transpose(1, 0, 2), lru_n, pool_n.transpose(1, 0, 2)


def _block_diag_tiles(w, tile):
    nb, c, _ = w.shape
    per = tile // c
    w4 = w.reshape(nb // per, per, c, c)
    placed = jnp.einsum('tpij,pq->tpiqj', w4, jnp.eye(per, dtype=w.dtype))
    return placed.reshape(nb // per, tile, tile)


def _rope_tables(pos, rows):
    half = HEAD_DIM // 2
    inv = ROPE_THETA ** (-jnp.arange(half, dtype=F32) / half)
    ang = pos.astype(F32)[:, None] * inv[None, :]
    cos = jnp.cos(ang)
    sin = jnp.sin(ang)
    reps = LANES // HEAD_DIM
    cos_t = jnp.tile(jnp.concatenate([cos, cos], axis=1), (1, reps))
    sin_t = jnp.tile(jnp.concatenate([-sin, sin], axis=1), (1, reps))
    if cos_t.shape[0] != rows:
        cos_t = jnp.broadcast_to(cos_t, (rows, LANES))
        sin_t = jnp.broadcast_to(sin_t, (rows, LANES))
    return cos_t, sin_t


def _row_tile(n, target):
    t = min(n, target)
    while n % t:
        t //= 2
    return t


def kernel(x_prompt, x_sample, cache_k, cache_v, page_table, state_s5_re, state_s5_im, state_conv, state_lru, state_pool, norm_mix, norm_ffn, norm_final, w_in_ab, w_out_ab, s5_a_re, s5_a_im, s5_log_dt, s5_b_re, s5_b_im, s5_c_re, s5_c_im, s5_d, s5_w_glu, s5_b_glu, diff_lq1, diff_lk1, diff_lq2, diff_lk2, diff_subln, w_in_cd, w_out_cd, conv_w, conv_b, lru_wa, lru_ba, lru_wx, lru_bx, lru_lambda, pool_w, pool_scale, ffn_w_gate, ffn_w_up, ffn_w_down):
    bsz, seq, d = x_prompt.shape
    nseq, dec_seq, _ = x_sample.shape
    assert dec_seq == 1
    depth = norm_mix.shape[0]
    npages = page_table.shape[1]
    page_size = cache_k.shape[2]
    past_len = npages * page_size
    assert past_len >= POOL_BUF
    s5w = s5_d.shape[1]
    qkw = cache_k.shape[3] * cache_k.shape[4]
    vw_total = cache_v.shape[3] * cache_v.shape[4]
    heads = cache_v.shape[3]
    n_p = bsz * seq
    row2 = lambda a: a.reshape(1, -1)

    xp = x_prompt.reshape(n_p, d)
    xs = x_sample.reshape(nseq, d)
    tm_p = _row_tile(seq, 512)
    cos_p, sin_p = _rope_tables(jnp.arange(seq, dtype=jnp.int32), seq)
    cos_s, sin_s = _rope_tables(jnp.full((1,), past_len, jnp.int32), nseq)

    outs_p = {'k': [], 'v': [], 're': [], 'im': [], 'conv': [], 'lru': [], 'pool': []}
    outs_s = {'k': [], 'v': [], 're': [], 'im': [], 'conv': [], 'lru': [], 'pool': []}

    ffn_gate_bf = ffn_w_gate.astype(BF16)
    ffn_up_bf = ffn_w_up.astype(BF16)
    ffn_down_bf = ffn_w_down.astype(BF16)

    for l in range(depth):
        j = l // 2
        g_mix = row2(norm_mix[l])
        last = l == depth - 1
        ffn_w = (row2(norm_ffn[l]), ffn_gate_bf, ffn_up_bf, ffn_down_bf, row2(norm_final), l)
        if l % 2 == 0:
            lam_init = 0.8 - 0.6 * math.exp(-0.3 * l)
            w_in = w_in_ab[j].astype(BF16)
            w_out = w_out_ab[j].astype(BF16)
            w_glu = s5_w_glu[j].astype(BF16)
            b_glu = row2(s5_b_glu[j])
            lamp = jnp.stack([diff_lq1[j], diff_lk1[j], diff_lq2[j], diff_lk2[j]]).astype(F32)
            gain = row2(diff_subln[j])
            s5p = (s5_a_re[j], s5_a_im[j], s5_log_dt[j], s5_b_re[j], s5_b_im[j],
                   s5_c_re[j], s5_c_im[j], s5_d[j])
            u, q_bf, k, v, k_bf, v_bf = _inproj_ab(xp, g_mix, w_in, cos_p, sin_p, tm_p, s5w, qkw)
            t_attn = _row_tile(seq, 1024)
            attn, k_t = _attn_prompt(q_bf, k_bf, v_bf, k, lamp, gain, bsz, seq, t_attn, t_attn,
                                     lam_init)
            y, h_re, h_im = _s5_prompt(u, _s5_prompt_tables(s5p, seq), bsz, seq)
            xp = _post_ab(xp, y, attn, w_glu, b_glu, w_out, ffn_w, tm_p, last)
            outs_p['k'].append(k_t.reshape(bsz, qkw // HEAD_DIM, HEAD_DIM, seq).transpose(0, 3, 1, 2))
            outs_p['v'].append(v.reshape(bsz, seq, heads, vw_total // heads))
            outs_p['re'].append(h_re)
            outs_p['im'].append(h_im)
            u, q_bf, k, v, _, _ = _inproj_ab(xs, g_mix, w_in, cos_s, sin_s, nseq, s5w, qkw)
            ck = jnp.transpose(cache_k[j], (0, 2, 3, 1)).reshape(-1, qkw, page_size)
            cv = cache_v[j].reshape(-1, page_size * heads, vw_total // heads)
            attn = _attn_decode(q_bf.astype(F32), k, v.reshape(nseq, vw_total), ck, cv, page_table,
                                lamp, gain,
                                math.gcd(npages, 32), lam_init)
            y, h_re, h_im = _s5_step(u, state_s5_re[j], state_s5_im[j], s5p)
            xs = _post_ab(xs, y, attn, w_glu, b_glu, w_out, ffn_w, nseq, last)
            outs_s['k'].append(k.reshape(nseq, 1, qkw // HEAD_DIM, HEAD_DIM))
            outs_s['v'].append(v.reshape(nseq, 1, heads, vw_total // heads))
            outs_s['re'].append(h_re)
            outs_s['im'].append(h_im)
        else:
            w = conv_w.shape[2]
            wg = jnp.concatenate([_block_diag_tiles(lru_wa[j], MXU_DIM),
                                  _block_diag_tiles(lru_wx[j], MXU_DIM)], axis=2).astype(BF16)
            cdw = (w_in_cd[j].astype(BF16), conv_w[j], row2(conv_b[j]), wg, row2(lru_ba[j]),
                   row2(lru_bx[j]), row2(lru_lambda[j]),
                   _block_diag_tiles(pool_w[j], MXU_DIM).astype(BF16), row2(pool_scale[j]),
                   w_out_cd[j].astype(BF16))
            xp, c_new, h_new, p_new = _cd_prompt(xp, g_mix, cdw, ffn_w, bsz, seq, tm_p, last)
            outs_p['conv'].append(c_new)
            outs_p['lru'].append(h_new.reshape(bsz, w))
            outs_p['pool'].append(p_new)
            xs, c_new, h_new, p_new = _cd_step(xs, g_mix, cdw, state_conv[j], state_lru[j], state_pool[j])
            outs_s['conv'].append(c_new)
            outs_s['lru'].append(h_new)
            outs_s['pool'].append(p_new)
            xs = _ffn(xs, ffn_w, nseq, last)

    sp = {name: jnp.stack(vals) for name, vals in outs_p.items()}
    ss = {name: jnp.stack(vals) for name, vals in outs_s.items()}
    y_prompt = xp.reshape(bsz, seq, d)
    y_sample = xs.reshape(nseq, 1, d)
    return (y_prompt, y_sample, sp['k'], sp['v'], ss['k'], ss['v'], sp['re'], sp['im'], ss['re'], ss['im'],
            sp['conv'], ss['conv'], sp['lru'], ss['lru'], sp['pool'], ss['pool'])
```
